```python
import jax, jax.numpy as jnp
from jax import lax
import numpy as np

D_MODEL = 1024
BATCH = 16
SEQ = 2048
DEPTH = 1
DEC_BATCH = 8
DEC_SEQ = 8192
PAST_LEN = 128

FOURIER_GROUPS = 4
FOURIER_GROUP_DIM = 128
FOURIER_WIDTH = FOURIER_GROUPS * FOURIER_GROUP_DIM
ATTN_GROUPS = ((128, 1), (512, 4), (2048, 16))
N_ATTN_GROUPS = 3
HEADS_PER_GROUP = 4
HEAD_DIM = 128
ATTN_GROUP_WIDTH = HEADS_PER_GROUP * HEAD_DIM
ATTN_WIDTH = N_ATTN_GROUPS * ATTN_GROUP_WIDTH
ATTN_OUT_WIDTH = HEADS_PER_GROUP * HEAD_DIM
ROPE_DIM = HEAD_DIM // 4
ROPE_THETA = 500000.0
IN_WIDTH = FOURIER_WIDTH + 3 * ATTN_WIDTH + 2 * D_MODEL
N_KEYS = 128
N_EXPERTS = N_KEYS * N_KEYS
PEER_HEADS = 8
PEER_TOPK = 16
PEER_KEY_DIM = 256
PEER_CHUNK = 128
NORM_EPS = 1e-6
NEG_INF = -1e30

kernel_name = 'hybrid_fnet_dilated_attn_peer_encoder'


def rmsnorm(x, g):
    xf = x.astype(jnp.float32)
    y = xf * lax.rsqrt(jnp.mean(xf * xf, axis=-1, keepdims=True) + NORM_EPS)
    return (y * g.astype(jnp.float32)).astype(x.dtype)


def partial_rope(x):
    s = x.shape[1]
    half = ROPE_DIM // 2
    inv_freq = ROPE_THETA ** (-jnp.arange(half, dtype=jnp.float32) * 2.0 / ROPE_DIM)
    ang = jnp.arange(s, dtype=jnp.float32)[:, None] * inv_freq[None, :]
    cos = jnp.cos(ang)[None, :, None, None, :]
    sin = jnp.sin(ang)[None, :, None, None, :]
    xr = x[..., :ROPE_DIM].astype(jnp.float32)
    x1, x2 = xr[..., :half], xr[..., half:]
    rot = jnp.concatenate([x1 * cos - x2 * sin, x2 * cos + x1 * sin], axis=-1).astype(x.dtype)
    return jnp.concatenate([rot, x[..., ROPE_DIM:]], axis=-1)


def dilated_window_attention(q, k, v, window, dilation):
    b, s, h, e = q.shape
    span = window // (2 * dilation)
    L = s // dilation
    nb = -(-L // span)
    Lp = nb * span

    def to_sub(t):
        t = t.reshape(b, L, dilation, h, e).transpose(0, 2, 1, 3, 4)
        return jnp.pad(t, ((0, 0), (0, 0), (0, Lp - L), (0, 0), (0, 0)))

    def band(t):
        t = jnp.pad(t, ((0, 0), (0, 0), (span, span), (0, 0), (0, 0)))
        t = t.reshape(b, dilation, nb + 2, span, h, e)
        return jnp.concatenate([t[:, :, :-2], t[:, :, 1:-1], t[:, :, 2:]], axis=3)

    qb = to_sub(q).reshape(b, dilation, nb, span, h, e)
    kb = band(to_sub(k))
    vb = band(to_sub(v))
    scores = jnp.einsum('bdnqhe,bdnkhe->bdnhqk', qb, kb).astype(jnp.float32) * (HEAD_DIM ** -0.5)
    qpos = jnp.arange(nb)[:, None, None] * span + jnp.arange(span)[None, :, None]
    kpos = jnp.arange(nb)[:, None, None] * span + jnp.arange(3 * span)[None, None, :] - span
    valid = (jnp.abs(kpos - qpos) <= span) & (kpos >= 0) & (kpos < L)
    scores = jnp.where(valid[None, None, :, None], scores, NEG_INF)
    m = jnp.max(scores, axis=-1, keepdims=True)
    p = jnp.exp(scores - m)
    l = jnp.sum(p, axis=-1, keepdims=True)
    o = jnp.einsum('bdnhqk,bdnkhe->bdnhqe', p, vb.astype(jnp.float32)) / l
    o = o.transpose(0, 1, 2, 4, 3, 5).reshape(b, dilation, Lp, h, e)[:, :, :L]
    o = o.transpose(0, 2, 1, 3, 4).reshape(b, s, h, e)
    logden = (m + jnp.log(l))[..., 0]
    logden = logden.transpose(0, 1, 2, 4, 3).reshape(b, dilation, Lp, h)[:, :, :L]
    logden = logden.transpose(0, 2, 1, 3).reshape(b, s, h)
    return o, logden


def fourier_mix(f):
    b, s, _ = f.shape
    fg = f.reshape(b, s, FOURIER_GROUPS, FOURIER_GROUP_DIM).astype(jnp.float32)
    return jnp.fft.fft2(fg, axes=(1, 3), norm='ortho').real.reshape(b, s, FOURIER_WIDTH)


def peer(xn, w_query, sub_keys, expert_u, expert_v):
    b, s, d = xn.shape
    t = b * s
    xt = xn.reshape(t, d)
    q = (xt @ w_query).reshape(t, PEER_HEADS, 2, PEER_KEY_DIM // 2)
    half = jnp.einsum('thcd,hcnd->thcn', q, sub_keys).astype(jnp.float32)
    s1, i1 = lax.top_k(half[:, :, 0], PEER_TOPK)
    s2, i2 = lax.top_k(half[:, :, 1], PEER_TOPK)
    cand = (s1[..., :, None] + s2[..., None, :]).reshape(t, PEER_HEADS, PEER_TOPK * PEER_TOPK)
    top_s, top_j = lax.top_k(cand, PEER_TOPK)
    ia = jnp.take_along_axis(i1, top_j // PEER_TOPK, axis=-1)
    ib = jnp.take_along_axis(i2, top_j % PEER_TOPK, axis=-1)
    experts = (ia * N_KEYS + ib).reshape(t, PEER_HEADS * PEER_TOPK)
    gates = jax.nn.softmax(top_s, axis=-1).reshape(t, PEER_HEADS * PEER_TOPK)
    nc = t // PEER_CHUNK

    def chunk_fn(args):
        xc, ec, gc = args
        a = jnp.einsum('cd,ckd->ck', xc, expert_u[ec]).astype(jnp.float32)
        hact = (jax.nn.gelu(a) * gc).astype(xc.dtype)
        return jnp.einsum('ck,ckd->cd', hact, expert_v[ec])

    out = lax.map(chunk_fn, (xt.reshape(nc, PEER_CHUNK, d),
                             experts.reshape(nc, PEER_CHUNK, PEER_HEADS * PEER_TOPK),
                             gates.reshape(nc, PEER_CHUNK, PEER_HEADS * PEER_TOPK)))
    return out.reshape(b, s, d).astype(xn.dtype)


def hybrid_block(x, g_mix, w_in, q_gain, k_gain, w_fourier, w_attn, w_out,
                 g_ffn, w_query, sub_keys, expert_u, expert_v):
    b, s, _ = x.shape
    xn = rmsnorm(x, g_mix)
    proj = xn @ w_in
    cuts = [FOURIER_WIDTH, FOURIER_WIDTH + ATTN_WIDTH, FOURIER_WIDTH + 2 * ATTN_WIDTH,
            FOURIER_WIDTH + 3 * ATTN_WIDTH, FOURIER_WIDTH + 3 * ATTN_WIDTH + D_MODEL]
    f, q, k, v, gate_a, gate_b = jnp.split(proj, cuts, axis=-1)
    hshape = (b, s, N_ATTN_GROUPS, HEADS_PER_GROUP, HEAD_DIM)
    q = partial_rope(rmsnorm(q.reshape(hshape), q_gain[:, None, :]))
    k = partial_rope(rmsnorm(k.reshape(hshape), k_gain[:, None, :]))
    v = v.reshape(hshape)
    outs, logdens = [], []
    for gi, (window, dilation) in enumerate(ATTN_GROUPS):
        o, ld = dilated_window_attention(q[:, :, gi], k[:, :, gi], v[:, :, gi], window, dilation)
        outs.append(o)
        logdens.append(ld)
    wgt = jax.nn.softmax(jnp.stack(logdens, axis=0), axis=0)
    attn = jnp.sum(wgt[..., None] * jnp.stack(outs, axis=0), axis=0)
    attn = attn.reshape(b, s, ATTN_OUT_WIDTH).astype(x.dtype) @ w_attn
    four = fourier_mix(f).astype(x.dtype) @ w_fourier
    mix = jax.nn.sigmoid(gate_a) * four + jax.nn.sigmoid(gate_b) * attn
    h = x + mix @ w_out
    return h + peer(rmsnorm(h, g_ffn), w_query, sub_keys, expert_u, expert_v)


def setup_inputs(seed: int = 0) -> dict:
    key = jax.random.key(seed)
    ks = jax.random.split(key, 16)
    f32 = jnp.float32

    def nrm(k, shape, scale):
        return jax.random.normal(k, shape, f32) * scale

    return {
        'x_prompt': nrm(ks[0], (BATCH, SEQ, D_MODEL), 1.0),
        'x_sample': nrm(ks[1], (DEC_BATCH, DEC_SEQ, D_MODEL), 1.0),
        'g_mix': 1.0 + nrm(ks[2], (DEPTH, D_MODEL), 0.02),
        'w_in': nrm(ks[3], (DEPTH, D_MODEL, IN_WIDTH), D_MODEL ** -0.5),
        'q_gain': 1.0 + nrm(ks[4], (DEPTH, N_ATTN_GROUPS, HEAD_DIM), 0.02),
        'k_gain': 1.0 + nrm(ks[5], (DEPTH, N_ATTN_GROUPS, HEAD_DIM), 0.02),
        'w_fourier': nrm(ks[6], (DEPTH, FOURIER_WIDTH, D_MODEL), FOURIER_WIDTH ** -0.5),
        'w_attn': nrm(ks[7], (DEPTH, ATTN_OUT_WIDTH, D_MODEL), ATTN_OUT_WIDTH ** -0.5),
        'w_out': nrm(ks[8], (DEPTH, D_MODEL, D_MODEL), D_MODEL ** -0.5),
        'g_ffn': 1.0 + nrm(ks[9], (DEPTH, D_MODEL), 0.02),
        'w_query': nrm(ks[10], (DEPTH, D_MODEL, PEER_HEADS * PEER_KEY_DIM), D_MODEL ** -0.5),
        'sub_keys': nrm(ks[11], (DEPTH, PEER_HEADS, 2, N_KEYS, PEER_KEY_DIM // 2), (PEER_KEY_DIM // 2) ** -0.5),
        'expert_u': nrm(ks[12], (DEPTH, N_EXPERTS, D_MODEL), D_MODEL ** -0.5),
        'expert_v': nrm(ks[13], (DEPTH, N_EXPERTS, D_MODEL), 0.25),
    }


def reference(x_prompt, x_sample, g_mix, w_in, q_gain, k_gain, w_fourier, w_attn, w_out,
              g_ffn, w_query, sub_keys, expert_u, expert_v):
    y_prompt = x_prompt
    y_sample = x_sample
    for layer in range(DEPTH):
        y_prompt = hybrid_block(y_prompt, g_mix[layer], w_in[layer], q_gain[layer], k_gain[layer],
                                w_fourier[layer], w_attn[layer], w_out[layer], g_ffn[layer],
                                w_query[layer], sub_keys[layer], expert_u[layer], expert_v[layer])
        y_sample = hybrid_block(y_sample, g_mix[layer], w_in[layer], q_gain[layer], k_gain[layer],
                                w_fourier[layer], w_attn[layer], w_out[layer], g_ffn[layer],
                                w_query[layer], sub_keys[layer], expert_u[layer], expert_v[layer])
    return (y_prompt, y_sample)
```

```python
import functools
import math

import jax
import jax.numpy as jnp
from jax import lax
from jax.experimental import pallas as pl
from jax.experimental.pallas import tpu as pltpu

F32 = jnp.float32
BF16 = jnp.bfloat16

D_MODEL = 1024
FOURIER_GROUPS = 4
FOURIER_GROUP_DIM = 128
FOURIER_WIDTH = FOURIER_GROUPS * FOURIER_GROUP_DIM
ATTN_GROUPS = ((128, 1), (512, 4), (2048, 16))
N_ATTN_GROUPS = 3
HEADS_PER_GROUP = 4
HEAD_DIM = 128
ATTN_GROUP_WIDTH = HEADS_PER_GROUP * HEAD_DIM
ATTN_WIDTH = N_ATTN_GROUPS * ATTN_GROUP_WIDTH
ROPE_DIM = HEAD_DIM // 4
ROPE_THETA = 500000.0
N_KEYS = 128
N_EXPERTS = N_KEYS * N_KEYS
PEER_HEADS = 8
PEER_TOPK = 16
PEER_KEY_DIM = 256
NORM_EPS = 1e-6
NEG_INF = -1e30

LANES = 128
VMEM_LIMIT = 56 * 1024 * 1024

TM_PROJ = 512
TM_MIX = 512
TM_TOPK = 512
TM_PEER = 512
EB_PEER = 1024
QB_ATTN = 128
DFT_TS = 1024
DFT_TK = 2048


def _cparams(sem):
    return pltpu.CompilerParams(dimension_semantics=sem, vmem_limit_bytes=VMEM_LIMIT)


def _const_spec(shape):
    nd = len(shape)
    return pl.BlockSpec(shape, lambda *_: (0,) * nd)


def _proj_kernel(x_ref, g_ref, wf_ref, wq_ref, wk_ref, wv_ref, wga_ref, wgb_ref,
                 qg_ref, kg_ref, cc_ref, sc_ref, rc_ref, rs1_ref, rs2_ref,
                 fa_ref, fb_ref, q_ref, k_ref, v_ref, sga_ref, sgb_ref):
    x = x_ref[...]
    ms = jnp.mean(x * x, axis=-1, keepdims=True)
    xn = (x * lax.rsqrt(ms + NORM_EPS) * g_ref[...]).astype(BF16)

    f = jnp.dot(xn, wf_ref[...], preferred_element_type=F32).astype(BF16)
    fa_ref[...] = jnp.dot(f, cc_ref[...], preferred_element_type=F32).astype(BF16)
    fb_ref[...] = jnp.dot(f, sc_ref[...], preferred_element_type=F32).astype(BF16)

    rc = rc_ref[...]
    rs1 = rs1_ref[...]
    rs2 = rs2_ref[...]

    def norm_rope(w_ref, gain_ref, out_ref, scale):
        y = jnp.dot(xn, w_ref[...], preferred_element_type=F32)
        for hh in range(N_ATTN_GROUPS * HEADS_PER_GROUP):
            yh = y[:, hh * HEAD_DIM:(hh + 1) * HEAD_DIM]
            hms = jnp.mean(yh * yh, axis=-1, keepdims=True)
            gi = hh // HEADS_PER_GROUP
            yn = yh * lax.rsqrt(hms + NORM_EPS) * gain_ref[gi:gi + 1, :]
            r = (yn * rc + pltpu.roll(yn, LANES - ROPE_DIM // 2, 1) * rs1
                 + pltpu.roll(yn, ROPE_DIM // 2, 1) * rs2)
            out_ref[:, hh * HEAD_DIM:(hh + 1) * HEAD_DIM] = (r * scale).astype(BF16)

    norm_rope(wq_ref, qg_ref, q_ref, HEAD_DIM ** -0.5)
    norm_rope(wk_ref, kg_ref, k_ref, 1.0)
    v_ref[...] = jnp.dot(xn, wv_ref[...], preferred_element_type=F32).astype(BF16)
    sga_ref[...] = jax.nn.sigmoid(
        jnp.dot(xn, wga_ref[...], preferred_element_type=F32)).astype(BF16)
    sgb_ref[...] = jax.nn.sigmoid(
        jnp.dot(xn, wgb_ref[...], preferred_element_type=F32)).astype(BF16)


def _rope_tables(s):
    half = ROPE_DIM // 2
    inv_freq = ROPE_THETA ** (-jnp.arange(half, dtype=F32) * 2.0 / ROPE_DIM)
    ang = jnp.arange(s, dtype=F32)[:, None] * inv_freq[None, :]
    cos, sin = jnp.cos(ang), jnp.sin(ang)
    zeros = jnp.zeros((s, HEAD_DIM - ROPE_DIM), F32)
    zh = jnp.zeros((s, half), F32)
    rc = jnp.concatenate([cos, cos, jnp.ones((s, HEAD_DIM - ROPE_DIM), F32)], axis=1)
    rs1 = jnp.concatenate([-sin, zh, zeros], axis=1)
    rs2 = jnp.concatenate([zh, sin, zeros], axis=1)
    return rc, rs1, rs2


def _channel_dft_mats():
    n = FOURIER_GROUP_DIM
    idx = (jnp.arange(n)[:, None] * jnp.arange(n)[None, :]) % n
    ang = idx.astype(F32) * (2.0 * math.pi / n)
    eye = jnp.eye(FOURIER_GROUPS, dtype=F32)
    cc = jnp.kron(eye, jnp.cos(ang) * n ** -0.5)
    sc = jnp.kron(eye, jnp.sin(ang) * n ** -0.5)
    return cc.astype(BF16), sc.astype(BF16)


def _proj(x2, s, g_mix, w_in, q_gain, k_gain, interpret):
    t = x2.shape[0]
    tm = TM_PROJ
    c0 = FOURIER_WIDTH
    c1 = c0 + ATTN_WIDTH
    c2 = c1 + ATTN_WIDTH
    c3 = c2 + ATTN_WIDTH
    c4 = c3 + D_MODEL
    wb = w_in.astype(BF16)
    wf, wq, wk, wv, wga, wgb = (wb[:, :c0], wb[:, c0:c1], wb[:, c1:c2], wb[:, c2:c3],
                                wb[:, c3:c4], wb[:, c4:])
    cc, sc = _channel_dft_mats()
    rc, rs1, rs2 = _rope_tables(s)
    spt = s // tm
    tok = lambda w: pl.BlockSpec((tm, w), lambda i: (i, 0))
    pos = pl.BlockSpec((tm, HEAD_DIM), lambda i: (i % spt, 0))
    out_shape = (
        jax.ShapeDtypeStruct((t, FOURIER_WIDTH), BF16),
        jax.ShapeDtypeStruct((t, FOURIER_WIDTH), BF16),
        jax.ShapeDtypeStruct((t, ATTN_WIDTH), BF16),
        jax.ShapeDtypeStruct((t, ATTN_WIDTH), BF16),
        jax.ShapeDtypeStruct((t, ATTN_WIDTH), BF16),
        jax.ShapeDtypeStruct((t, D_MODEL), BF16),
        jax.ShapeDtypeStruct((t, D_MODEL), BF16),
    )
    return pl.pallas_call(
        _proj_kernel,
        out_shape=out_shape,
        grid=(t // tm,),
        in_specs=[tok(D_MODEL), _const_spec((1, D_MODEL)),
                  _const_spec(wf.shape), _const_spec(wq.shape), _const_spec(wk.shape),
                  _const_spec(wv.shape), _const_spec(wga.shape), _const_spec(wgb.shape),
                  _const_spec(q_gain.shape), _const_spec(k_gain.shape),
                  _const_spec(cc.shape), _const_spec(sc.shape), pos, pos, pos],
        out_specs=(tok(FOURIER_WIDTH), tok(FOURIER_WIDTH), tok(ATTN_WIDTH), tok(ATTN_WIDTH),
                   tok(ATTN_WIDTH), tok(D_MODEL), tok(D_MODEL)),
        compiler_params=_cparams(("arbitrary",)),
        name="proj",
        interpret=interpret,
    )(x2, g_mix.reshape(1, D_MODEL), wf, wq, wk, wv, wga, wgb, q_gain, k_gain, cc, sc,
      rc, rs1, rs2)


def _attn_kernel(q_ref, k_ref, v_ref, o_ref, ld_ref, *, sub_len):
    kb = min(2 * QB_ATTN, sub_len)
    span = QB_ATTN // 2

    def body(i, carry):
        start = pl.multiple_of(i * QB_ATTN, QB_ATTN)
        ks = pl.multiple_of(jnp.clip(start - span, 0, sub_len - kb), span)
        q = q_ref[pl.ds(start, QB_ATTN), :]
        k = k_ref[pl.ds(ks, kb), :]
        v = v_ref[pl.ds(ks, kb), :]
        s = lax.dot_general(q, k, (((1,), (1,)), ((), ())), preferred_element_type=F32)
        qpos = start + lax.broadcasted_iota(jnp.int32, (QB_ATTN, kb), 0)
        kpos = ks + lax.broadcasted_iota(jnp.int32, (QB_ATTN, kb), 1)
        s = jnp.where(jnp.abs(kpos - qpos) <= span, s, NEG_INF)
        m = jnp.max(s, axis=1, keepdims=True)
        p = jnp.exp(s - m)
        l = jnp.sum(p, axis=1, keepdims=True)
        o = jnp.dot(p.astype(BF16), v, preferred_element_type=F32) / l
        o_ref[pl.ds(start, QB_ATTN), :] = o.astype(BF16)
        ld_ref[pl.ds(start, QB_ATTN), :] = jnp.broadcast_to(m + jnp.log(l),
                                                            (QB_ATTN, HEAD_DIM))
        return carry

    lax.fori_loop(0, sub_len // QB_ATTN, body, 0)


def _attn_group(q, k, v, b, s, gi, dilation, interpret):
    sub_len = s // dilation
    assert sub_len % QB_ATTN == 0
    view = lambda a: a.reshape(b, sub_len, dilation * ATTN_WIDTH)
    hpg = HEADS_PER_GROUP
    in_spec = pl.BlockSpec(
        (None, sub_len, HEAD_DIM),
        lambda bi, r, h: (bi, 0, r * (N_ATTN_GROUPS * hpg) + gi * hpg + h))
    out_spec = pl.BlockSpec((None, sub_len, HEAD_DIM), lambda bi, r, h: (bi, 0, r * hpg + h))
    o, ld = pl.pallas_call(
        functools.partial(_attn_kernel, sub_len=sub_len),
        out_shape=(jax.ShapeDtypeStruct((b, sub_len, dilation * ATTN_GROUP_WIDTH), BF16),
                   jax.ShapeDtypeStruct((b, sub_len, dilation * ATTN_GROUP_WIDTH), F32)),
        grid=(b, dilation, hpg),
        in_specs=[in_spec, in_spec, in_spec],
        out_specs=(out_spec, out_spec),
        compiler_params=_cparams(("arbitrary", "arbitrary", "arbitrary")),
        name=f"attn_d{dilation}",
        interpret=interpret,
    )(view(q), view(k), view(v))
    return o.reshape(b * s, ATTN_GROUP_WIDTH), ld.reshape(b * s, ATTN_GROUP_WIDTH)


def _dft_kernel(cs_ref, ss_ref, a_ref, b_ref, y_ref, acc_ref):
    kk = pl.program_id(2)

    @pl.when(kk == 0)
    def _():
        acc_ref[...] = jnp.zeros_like(acc_ref)

    acc_ref[...] += (jnp.dot(cs_ref[...], a_ref[...], preferred_element_type=F32)
                     - jnp.dot(ss_ref[...], b_ref[...], preferred_element_type=F32))

    @pl.when(kk == pl.num_programs(2) - 1)
    def _():
        y_ref[...] = acc_ref[...].astype(BF16)


def _seq_dft_mats(s):
    lo = LANES
    hi = s // lo
    rows = jnp.arange(s, dtype=jnp.int32)[:, None]
    scale = 2.0 * math.pi / s
    ang_hi = ((rows * (jnp.arange(hi, dtype=jnp.int32)[None, :] * lo)) % s).astype(F32) * scale
    ang_lo = ((rows * jnp.arange(lo, dtype=jnp.int32)[None, :]) % s).astype(F32) * scale
    ch, sh = jnp.cos(ang_hi)[:, :, None], jnp.sin(ang_hi)[:, :, None]
    cl, sl = jnp.cos(ang_lo)[:, None, :], jnp.sin(ang_lo)[:, None, :]
    norm = s ** -0.5
    cs = ((ch * cl - sh * sl) * norm).astype(BF16).reshape(s, s)
    ss = ((sh * cl + ch * sl) * norm).astype(BF16).reshape(s, s)
    return cs, ss


def _seq_dft(fa, fb, b, s, interpret):
    ts = min(DFT_TS, s)
    tk = min(DFT_TK, s)
    cs, ss = _seq_dft_mats(s)
    mat = pl.BlockSpec((ts, tk), lambda bi, i, kk: (i, kk))
    vec = pl.BlockSpec((None, tk, FOURIER_WIDTH), lambda bi, i, kk: (bi, kk, 0))
    y = pl.pallas_call(
        _dft_kernel,
        out_shape=jax.ShapeDtypeStruct((b, s, FOURIER_WIDTH), BF16),
        grid=(b, s // ts, s // tk),
        in_specs=[mat, mat, vec, vec],
        out_specs=pl.BlockSpec((None, ts, FOURIER_WIDTH), lambda bi, i, kk: (bi, i, 0)),
        scratch_shapes=[pltpu.VMEM((ts, FOURIER_WIDTH), F32)],
        compiler_params=_cparams(("arbitrary", "arbitrary", "arbitrary")),
        name="seq_dft",
        interpret=interpret,
    )(cs, ss, fa.reshape(b, s, FOURIER_WIDTH), fb.reshape(b, s, FOURIER_WIDTH))
    return y.reshape(b * s, FOURIER_WIDTH)


def _mix_kernel(x_ref, o0_ref, o1_ref, o2_ref, l0_ref, l1_ref, l2_ref, y_ref, sga_ref, sgb_ref,
                wat_ref, wfo_ref, wout_ref, gf_ref, wqy_ref, sk_ref,
                h_ref, hn_ref, st_ref):
    l0, l1, l2 = l0_ref[...], l1_ref[...], l2_ref[...]
    mx = jnp.maximum(jnp.maximum(l0, l1), l2)
    w0, w1, w2 = jnp.exp(l0 - mx), jnp.exp(l1 - mx), jnp.exp(l2 - mx)
    attn = (w0 * o0_ref[...].astype(F32) + w1 * o1_ref[...].astype(F32)
            + w2 * o2_ref[...].astype(F32)) / (w0 + w1 + w2)
    attn = jnp.dot(attn.astype(BF16), wat_ref[...], preferred_element_type=F32)
    four = jnp.dot(y_ref[...], wfo_ref[...], preferred_element_type=F32)
    mix = sga_ref[...].astype(F32) * four + sgb_ref[...].astype(F32) * attn
    h = x_ref[...] + jnp.dot(mix.astype(BF16), wout_ref[...], preferred_element_type=F32)
    h_ref[...] = h
    ms = jnp.mean(h * h, axis=-1, keepdims=True)
    hn = (h * lax.rsqrt(ms + NORM_EPS) * gf_ref[...]).astype(BF16)
    hn_ref[...] = hn
    qy = jnp.dot(hn, wqy_ref[...], preferred_element_type=F32).astype(BF16)
    half = PEER_KEY_DIM // 2
    for hc in range(2 * PEER_HEADS):
        st_ref[hc] = lax.dot_general(sk_ref[hc], qy[:, hc * half:(hc + 1) * half],
                                     (((1,), (1,)), ((), ())), preferred_element_type=F32)


def _mix(x2, o_list, ld_list, yf, sga, sgb, w_attn, w_fourier, w_out, g_ffn, w_query, sub_keys,
         interpret):
    t = x2.shape[0]
    tm = TM_MIX
    tok = lambda w: pl.BlockSpec((tm, w), lambda i: (i, 0))
    sk = sub_keys.reshape(2 * PEER_HEADS, N_KEYS, PEER_KEY_DIM // 2).astype(BF16)
    wat, wfo, wout, wqy = (w_attn.astype(BF16), w_fourier.astype(BF16), w_out.astype(BF16),
                           w_query.astype(BF16))
    return pl.pallas_call(
        _mix_kernel,
        out_shape=(jax.ShapeDtypeStruct((t, D_MODEL), F32),
                   jax.ShapeDtypeStruct((t, D_MODEL), BF16),
                   jax.ShapeDtypeStruct((2 * PEER_HEADS, N_KEYS, t), F32)),
        grid=(t // tm,),
        in_specs=[tok(D_MODEL)] + [tok(ATTN_GROUP_WIDTH)] * 6 + [tok(FOURIER_WIDTH),
                  tok(D_MODEL), tok(D_MODEL),
                  _const_spec(wat.shape), _const_spec(wfo.shape), _const_spec(wout.shape),
                  _const_spec((1, D_MODEL)), _const_spec(wqy.shape), _const_spec(sk.shape)],
        out_specs=(tok(D_MODEL), tok(D_MODEL),
                   pl.BlockSpec((2 * PEER_HEADS, N_KEYS, tm), lambda i: (0, 0, i))),
        compiler_params=_cparams(("arbitrary",)),
        name="mix",
        interpret=interpret,
    )(x2, *o_list, *ld_list, yf, sga, sgb, wat, wfo, wout, g_ffn.reshape(1, D_MODEL), wqy, sk)


N_TOP = PEER_TOPK + 1


def _top_rows(vals, n):
    out = []
    cur = vals
    for _ in range(n):
        m = jnp.max(cur, axis=0, keepdims=True)
        out.append(m)
        cur = jnp.where(cur == m, -jnp.inf, cur)
    return out


def _topk_kernel(st_ref, e1_ref, e2_ref, th_ref):
    for h in range(PEER_HEADS):
        s1 = st_ref[2 * h]
        s2 = st_ref[2 * h + 1]
        a = _top_rows(s1, N_TOP)
        b = _top_rows(s2, N_TOP)
        ea = [jnp.exp(ai - a[0]) for ai in a]
        eb = [jnp.exp(bi - b[0]) for bi in b]
        cand = [ea[i] * eb[j] for i in range(N_TOP) for j in range(N_TOP)
                if (i + 1) * (j + 1) <= N_TOP]
        cand = jnp.concatenate(cand, axis=0)
        top = _top_rows(cand, N_TOP)
        z = top[0]
        for r in range(1, PEER_TOPK):
            z = z + top[r]
        inv_z = 1.0 / z
        e1_ref[h] = jnp.exp(s1 - a[0])
        e2_ref[h] = jnp.exp(s2 - b[0]) * inv_z
        th_ref[h:h + 1, :] = 0.5 * (top[PEER_TOPK - 1] + top[PEER_TOPK]) * inv_z


def _topk(st, interpret):
    t = st.shape[2]
    tm = TM_TOPK
    blk = lambda n: pl.BlockSpec((n, N_KEYS, tm), lambda i: (0, 0, i))
    return pl.pallas_call(
        _topk_kernel,
        out_shape=(jax.ShapeDtypeStruct((PEER_HEADS, N_KEYS, t), F32),
                   jax.ShapeDtypeStruct((PEER_HEADS, N_KEYS, t), F32),
                   jax.ShapeDtypeStruct((PEER_HEADS, t), F32)),
        grid=(t // tm,),
        in_specs=[blk(2 * PEER_HEADS)],
        out_specs=(blk(PEER_HEADS), blk(PEER_HEADS),
                   pl.BlockSpec((PEER_HEADS, tm), lambda i: (0, i))),
        compiler_params=_cparams(("arbitrary",)),
        name="peer_topk",
        interpret=interpret,
    )(st)


def _peer_kernel(hn_ref, u_ref, vt_ref, e1_ref, e2_ref, th_ref, h_ref, y_ref,
                 at_ref, ht_ref, acc_ref):
    j = pl.program_id(1)
    tm = hn_ref.shape[0]

    @pl.when(j == 0)
    def _():
        acc_ref[...] = jnp.zeros_like(acc_ref)

    at_ref[...] = lax.dot_general(u_ref[...], hn_ref[...], (((1,), (1,)), ((), ())),
                                  preferred_element_type=F32)

    for ia in range(u_ref.shape[0] // N_KEYS):
        rows = slice(ia * N_KEYS, (ia + 1) * N_KEYS)
        for tc in range(tm // LANES):
            lanes = slice(tc * LANES, (tc + 1) * LANES)
            gate = jnp.zeros((N_KEYS, LANES), F32)
            for h in range(PEER_HEADS):
                p = e1_ref[h, ia:ia + 1, lanes] * e2_ref[h, :, lanes]
                gate = gate + jnp.where(p >= th_ref[h:h + 1, lanes], p, 0.0)
            ht_ref[rows, lanes] = (jax.nn.gelu(at_ref[rows, lanes]) * gate).astype(BF16)

    acc_ref[...] += jnp.dot(vt_ref[...], ht_ref[...], preferred_element_type=F32)

    @pl.when(j == pl.num_programs(1) - 1)
    def _():
        y_ref[...] = h_ref[...] + acc_ref[...].T


def _peer(hn, h, e1, e2, th, expert_u, expert_v, interpret):
    t = hn.shape[0]
    tm, eb = TM_PEER, EB_PEER
    u = expert_u.astype(BF16)
    vt = expert_v.astype(BF16).T
    return pl.pallas_call(
        _peer_kernel,
        out_shape=jax.ShapeDtypeStruct((t, D_MODEL), F32),
        grid=(t // tm, N_EXPERTS // eb),
        in_specs=[pl.BlockSpec((tm, D_MODEL), lambda i, j: (i, 0)),
                  pl.BlockSpec((eb, D_MODEL), lambda i, j: (j, 0)),
                  pl.BlockSpec((D_MODEL, eb), lambda i, j: (0, j)),
                  pl.BlockSpec((PEER_HEADS, eb // N_KEYS, tm), lambda i, j: (0, j, i)),
                  pl.BlockSpec((PEER_HEADS, N_KEYS, tm), lambda i, j: (0, 0, i)),
                  pl.BlockSpec((PEER_HEADS, tm), lambda i, j: (0, i)),
                  pl.BlockSpec((tm, D_MODEL), lambda i, j: (i, 0))],
        out_specs=pl.BlockSpec((tm, D_MODEL), lambda i, j: (i, 0)),
        scratch_shapes=[pltpu.VMEM((eb, tm), F32), pltpu.VMEM((eb, tm), BF16),
                        pltpu.VMEM((D_MODEL, tm), F32)],
        compiler_params=_cparams(("arbitrary", "arbitrary")),
        name="peer_experts",
        interpret=interpret,
    )(hn, u, vt, e1, e2, th, h)


def _hybrid_block(x, g_mix, w_in, q_gain, k_gain, w_fourier, w_attn, w_out, g_ffn, w_query,
                  sub_keys, expert_u, expert_v, interpret=False):
    b, s, _ = x.shape
    x2 = x.reshape(b * s, D_MODEL)
    fa, fb, q, k, v, sga, sgb = _proj(x2, s, g_mix, w_in, q_gain, k_gain, interpret)
    o_list, ld_list = [], []
    for gi, (_, dilation) in enumerate(ATTN_GROUPS):
        o, ld = _attn_group(q, k, v, b, s, gi, dilation, interpret)
        o_list.append(o)
        ld_list.append(ld)
    yf = _seq_dft(fa, fb, b, s, interpret)
    h, hn, st = _mix(x2, o_list, ld_list, yf, sga, sgb, w_attn, w_fourier, w_out, g_ffn,
                     w_query, sub_keys, interpret)
    e1, e2, th = _topk(st, interpret)
    y = _peer(hn, h, e1, e2, th, expert_u, expert_v, interpret)
    return y.reshape(b, s, D_MODEL)


def kernel(x_prompt, x_sample, g_mix, w_in, q_gain, k_gain, w_fourier, w_attn, w_out, g_ffn,
           w_query, sub_keys, expert_u, expert_v):
    y_prompt, y_sample = x_prompt, x_sample
    for layer in range(g_mix.shape[0]):
        params = (g_mix[layer], w_in[layer], q_gain[layer], k_gain[layer], w_fourier[layer],
                  w_attn[layer], w_out[layer], g_ffn[layer], w_query[layer], sub_keys[layer],
                  expert_u[layer], expert_v[layer])
        y_prompt = _hybrid_block(y_prompt, *params)
        y_sample = _hybrid_block(y_sample, *params)
    return (y_prompt, y_sample)
```

```python
import functools
import math

import jax
import jax.numpy as jnp
from jax import lax
from jax.experimental import pallas as pl
from jax.experimental.pallas import tpu as pltpu

F32 = jnp.float32
BF16 = jnp.bfloat16

D_MODEL = 1024
FOURIER_GROUPS = 4
FOURIER_GROUP_DIM = 128
FOURIER_WIDTH = FOURIER_GROUPS * FOURIER_GROUP_DIM
ATTN_GROUPS = ((128, 1), (512, 4), (2048, 16))
N_ATTN_GROUPS = 3
HEADS_PER_GROUP = 4
HEAD_DIM = 128
ATTN_GROUP_WIDTH = HEADS_PER_GROUP * HEAD_DIM
ATTN_WIDTH = N_ATTN_GROUPS * ATTN_GROUP_WIDTH
ROPE_DIM = HEAD_DIM // 4
ROPE_THETA = 500000.0
N_KEYS = 128
N_EXPERTS = N_KEYS * N_KEYS
PEER_HEADS = 8
PEER_TOPK = 16
PEER_KEY_DIM = 256
NORM_EPS = 1e-6
NEG_INF = -1e30

LANES = 128
VMEM_LIMIT = 56 * 1024 * 1024

TM_PROJ = 512
TM_MIX = 512
TM_TOPK = 512
TM_PEER = 512
EB_PEER = 1024
QB_ATTN = 128
DFT_TS = 1024
DFT_TK = 2048


def _cparams(sem):
    return pltpu.CompilerParams(dimension_semantics=sem, vmem_limit_bytes=VMEM_LIMIT)


def _const_spec(shape):
    nd = len(shape)
    return pl.BlockSpec(shape, lambda *_: (0,) * nd)


def _proj_kernel(x_ref, g_ref, wf_ref, wq_ref, wk_ref, wv_ref, wga_ref, wgb_ref,
                 qg_ref, kg_ref, cc_ref, sc_ref, rc_ref, rs1_ref, rs2_ref,
                 fa_ref, fb_ref, q0_ref, q1_ref, q2_ref, k0_ref, k1_ref, k2_ref,
                 v0_ref, v1_ref, v2_ref, sga_ref, sgb_ref, stage_ref):
    tm = x_ref.shape[0]
    hpg = HEADS_PER_GROUP
    stage_slot = [0]

    def store_group(out_ref, gi, cols):
        dilation = ATTN_GROUPS[gi][1]
        if dilation == 1:
            for hl, c in enumerate(cols):
                out_ref[0, :, hl * HEAD_DIM:(hl + 1) * HEAD_DIM] = c.astype(BF16)
            return
        slot = stage_slot[0]
        stage_slot[0] += 1
        for hl, c in enumerate(cols):
            stage_ref[slot, hl] = c
            for r in range(dilation):
                out_ref[r, :, hl * HEAD_DIM:(hl + 1) * HEAD_DIM] = stage_ref[
                    slot, hl, pl.ds(r, tm // dilation, stride=dilation), :].astype(BF16)

    x = x_ref[...]
    ms = jnp.mean(x * x, axis=-1, keepdims=True)
    xn = (x * lax.rsqrt(ms + NORM_EPS) * g_ref[...]).astype(BF16)

    f = jnp.dot(xn, wf_ref[...], preferred_element_type=F32).astype(BF16)
    fa_ref[...] = jnp.dot(f, cc_ref[...], preferred_element_type=F32).astype(BF16)
    fb_ref[...] = jnp.dot(f, sc_ref[...], preferred_element_type=F32).astype(BF16)

    rc = rc_ref[...]
    rs1 = rs1_ref[...]
    rs2 = rs2_ref[...]

    def norm_rope(w_ref, gain_ref, out_refs, scale):
        y = jnp.dot(xn, w_ref[...], preferred_element_type=F32)
        for gi in range(N_ATTN_GROUPS):
            cols = []
            for hh in range(gi * hpg, (gi + 1) * hpg):
                yh = y[:, hh * HEAD_DIM:(hh + 1) * HEAD_DIM]
                hms = jnp.mean(yh * yh, axis=-1, keepdims=True)
                yn = yh * lax.rsqrt(hms + NORM_EPS) * gain_ref[gi:gi + 1, :]
                r = (yn * rc + pltpu.roll(yn, LANES - ROPE_DIM // 2, 1) * rs1
                     + pltpu.roll(yn, ROPE_DIM // 2, 1) * rs2)
                cols.append(r * scale)
            store_group(out_refs[gi], gi, cols)

    norm_rope(wq_ref, qg_ref, (q0_ref, q1_ref, q2_ref), HEAD_DIM ** -0.5)
    norm_rope(wk_ref, kg_ref, (k0_ref, k1_ref, k2_ref), 1.0)
    yv = jnp.dot(xn, wv_ref[...], preferred_element_type=F32)
    for gi, out_ref in enumerate((v0_ref, v1_ref, v2_ref)):
        store_group(out_ref, gi, [yv[:, hh * HEAD_DIM:(hh + 1) * HEAD_DIM]
                                  for hh in range(gi * hpg, (gi + 1) * hpg)])
    sga_ref[...] = jax.nn.sigmoid(
        jnp.dot(xn, wga_ref[...], preferred_element_type=F32)).astype(BF16)
    sgb_ref[...] = jax.nn.sigmoid(
        jnp.dot(xn, wgb_ref[...], preferred_element_type=F32)).astype(BF16)


def _rope_tables(s):
    half = ROPE_DIM // 2
    inv_freq = ROPE_THETA ** (-jnp.arange(half, dtype=F32) * 2.0 / ROPE_DIM)
    ang = jnp.arange(s, dtype=F32)[:, None] * inv_freq[None, :]
    cos, sin = jnp.cos(ang), jnp.sin(ang)
    zeros = jnp.zeros((s, HEAD_DIM - ROPE_DIM), F32)
    zh = jnp.zeros((s, half), F32)
    rc = jnp.concatenate([cos, cos, jnp.ones((s, HEAD_DIM - ROPE_DIM), F32)], axis=1)
    rs1 = jnp.concatenate([-sin, zh, zeros], axis=1)
    rs2 = jnp.concatenate([zh, sin, zeros], axis=1)
    return rc, rs1, rs2


def _channel_dft_mats():
    n = FOURIER_GROUP_DIM
    idx = (jnp.arange(n)[:, None] * jnp.arange(n)[None, :]) % n
    ang = idx.astype(F32) * (2.0 * math.pi / n)
    eye = jnp.eye(FOURIER_GROUPS, dtype=F32)
    cc = jnp.kron(eye, jnp.cos(ang) * n ** -0.5)
    sc = jnp.kron(eye, jnp.sin(ang) * n ** -0.5)
    return cc.astype(BF16), sc.astype(BF16)


def _proj(x2, s, g_mix, w_in, q_gain, k_gain, interpret):
    t = x2.shape[0]
    tm = TM_PROJ
    c0 = FOURIER_WIDTH
    c1 = c0 + ATTN_WIDTH
    c2 = c1 + ATTN_WIDTH
    c3 = c2 + ATTN_WIDTH
    c4 = c3 + D_MODEL
    wb = w_in.astype(BF16)
    wf, wq, wk, wv, wga, wgb = (wb[:, :c0], wb[:, c0:c1], wb[:, c1:c2], wb[:, c2:c3],
                                wb[:, c3:c4], wb[:, c4:])
    cc, sc = _channel_dft_mats()
    rc, rs1, rs2 = _rope_tables(s)
    spt = s // tm
    b = t // s
    tok = lambda w: pl.BlockSpec((tm, w), lambda i: (i, 0))
    pos = pl.BlockSpec((tm, HEAD_DIM), lambda i: (i % spt, 0))
    grp_shapes = [jax.ShapeDtypeStruct((b, d, s // d, ATTN_GROUP_WIDTH), BF16)
                  for _, d in ATTN_GROUPS]
    grp_specs = [pl.BlockSpec((None, d, tm // d, ATTN_GROUP_WIDTH),
                              lambda i: (i // spt, 0, i % spt, 0)) for _, d in ATTN_GROUPS]
    out_shape = ([jax.ShapeDtypeStruct((t, FOURIER_WIDTH), BF16)] * 2 + grp_shapes * 3
                 + [jax.ShapeDtypeStruct((t, D_MODEL), BF16)] * 2)
    n_staged = 3 * sum(1 for _, d in ATTN_GROUPS if d > 1)
    outs = pl.pallas_call(
        _proj_kernel,
        out_shape=out_shape,
        grid=(t // tm,),
        in_specs=[tok(D_MODEL), _const_spec((1, D_MODEL)),
                  _const_spec(wf.shape), _const_spec(wq.shape), _const_spec(wk.shape),
                  _const_spec(wv.shape), _const_spec(wga.shape), _const_spec(wgb.shape),
                  _const_spec(q_gain.shape), _const_spec(k_gain.shape),
                  _const_spec(cc.shape), _const_spec(sc.shape), pos, pos, pos],
        out_specs=[tok(FOURIER_WIDTH)] * 2 + grp_specs * 3 + [tok(D_MODEL)] * 2,
        scratch_shapes=[pltpu.VMEM((n_staged, HEADS_PER_GROUP, tm, HEAD_DIM), F32)],
        compiler_params=_cparams(("arbitrary",)),
        name="proj",
        interpret=interpret,
    )(x2, g_mix.reshape(1, D_MODEL), wf, wq, wk, wv, wga, wgb, q_gain, k_gain, cc, sc,
      rc, rs1, rs2)
    fa, fb = outs[0], outs[1]
    q, k, v = outs[2:5], outs[5:8], outs[8:11]
    return fa, fb, q, k, v, outs[11], outs[12]


LD_LANES = LANES // HEADS_PER_GROUP


def _attn_kernel(q_ref, k_ref, v_ref, o_ref, ld_ref, *, sub_len, heads):
    kb = min(2 * QB_ATTN, sub_len)
    span = QB_ATTN // 2
    first_head = pl.program_id(2) * heads

    if heads < HEADS_PER_GROUP:
        @pl.when(first_head == 0)
        def _():
            ld_ref[...] = jnp.zeros_like(ld_ref)

    def body(i, carry):
        start = pl.multiple_of(i * QB_ATTN, QB_ATTN)
        ks = pl.multiple_of(jnp.clip(start - span, 0, sub_len - kb), span)
        qpos = start + lax.broadcasted_iota(jnp.int32, (QB_ATTN, kb), 0)
        kpos = ks + lax.broadcasted_iota(jnp.int32, (QB_ATTN, kb), 1)
        in_band = jnp.abs(kpos - qpos) <= span
        lane_head = lax.broadcasted_iota(jnp.int32, (QB_ATTN, LANES), 1) // LD_LANES
        if heads < HEADS_PER_GROUP:
            ld_all = ld_ref[pl.ds(start, QB_ATTN), :]
        else:
            ld_all = jnp.zeros((QB_ATTN, LANES), F32)
        for hl in range(heads):
            cols = slice(hl * HEAD_DIM, (hl + 1) * HEAD_DIM)
            q = q_ref[pl.ds(start, QB_ATTN), cols]
            k = k_ref[pl.ds(ks, kb), cols]
            v = v_ref[pl.ds(ks, kb), cols]
            s = lax.dot_general(q, k, (((1,), (1,)), ((), ())), preferred_element_type=F32)
            s = jnp.where(in_band, s, NEG_INF)
            m = jnp.max(s, axis=1, keepdims=True)
            p = jnp.exp(s - m)
            l = jnp.sum(p, axis=1, keepdims=True)
            o = jnp.dot(p.astype(BF16), v, preferred_element_type=F32) / l
            o_ref[pl.ds(start, QB_ATTN), cols] = o.astype(BF16)
            ld_all = jnp.where(lane_head == first_head + hl, m + jnp.log(l), ld_all)
        ld_ref[pl.ds(start, QB_ATTN), :] = ld_all
        return carry

    nq = sub_len // QB_ATTN
    lax.fori_loop(0, nq, body, 0, unroll=min(nq, HEADS_PER_GROUP // heads))


ATTN_MAX_WIDE_LEN = 2048


def _attn_group(q, k, v, dilation, interpret):
    b, _, sub_len, _ = q.shape
    assert sub_len % QB_ATTN == 0
    heads = HEADS_PER_GROUP if sub_len <= ATTN_MAX_WIDE_LEN else 1
    width = heads * HEAD_DIM
    spec = pl.BlockSpec((None, None, sub_len, width), lambda bi, r, hb: (bi, r, 0, hb))
    return pl.pallas_call(
        functools.partial(_attn_kernel, sub_len=sub_len, heads=heads),
        out_shape=(jax.ShapeDtypeStruct(q.shape, BF16),
                   jax.ShapeDtypeStruct((b, dilation, sub_len, LANES), F32)),
        grid=(b, dilation, HEADS_PER_GROUP // heads),
        in_specs=[spec, spec, spec],
        out_specs=(spec, pl.BlockSpec((None, None, sub_len, LANES),
                                      lambda bi, r, hb: (bi, r, 0, 0))),
        compiler_params=_cparams(("arbitrary", "arbitrary", "arbitrary")),
        name=f"attn_d{dilation}",
        interpret=interpret,
    )(q, k, v)


def _dft_kernel(cs_ref, ss_ref, a_ref, b_ref, y_ref, acc_ref):
    kk = pl.program_id(2)

    @pl.when(kk == 0)
    def _():
        acc_ref[...] = jnp.zeros_like(acc_ref)

    acc_ref[...] += (jnp.dot(cs_ref[...], a_ref[...], preferred_element_type=F32)
                     - jnp.dot(ss_ref[...], b_ref[...], preferred_element_type=F32))

    @pl.when(kk == pl.num_programs(2) - 1)
    def _():
        y_ref[...] = acc_ref[...].astype(BF16)


def _seq_dft_mats(s):
    lo = LANES
    hi = s // lo
    rows = jnp.arange(s, dtype=jnp.int32)[:, None]
    scale = 2.0 * math.pi / s
    ang_hi = ((rows * (jnp.arange(hi, dtype=jnp.int32)[None, :] * lo)) % s).astype(F32) * scale
    ang_lo = ((rows * jnp.arange(lo, dtype=jnp.int32)[None, :]) % s).astype(F32) * scale
    ch, sh = jnp.cos(ang_hi)[:, :, None], jnp.sin(ang_hi)[:, :, None]
    cl, sl = jnp.cos(ang_lo)[:, None, :], jnp.sin(ang_lo)[:, None, :]
    norm = s ** -0.5
    cs = ((ch * cl - sh * sl) * norm).astype(BF16).reshape(s, s)
    ss = ((sh * cl + ch * sl) * norm).astype(BF16).reshape(s, s)
    return cs, ss


def _seq_dft(fa, fb, b, s, interpret):
    ts = min(DFT_TS, s)
    tk = min(DFT_TK, s)
    cs, ss = _seq_dft_mats(s)
    mat = pl.BlockSpec((ts, tk), lambda bi, i, kk: (i, kk))
    vec = pl.BlockSpec((None, tk, FOURIER_WIDTH), lambda bi, i, kk: (bi, kk, 0))
    y = pl.pallas_call(
        _dft_kernel,
        out_shape=jax.ShapeDtypeStruct((b, s, FOURIER_WIDTH), BF16),
        grid=(b, s // ts, s // tk),
        in_specs=[mat, mat, vec, vec],
        out_specs=pl.BlockSpec((None, ts, FOURIER_WIDTH), lambda bi, i, kk: (bi, i, 0)),
        scratch_shapes=[pltpu.VMEM((ts, FOURIER_WIDTH), F32)],
        compiler_params=_cparams(("arbitrary", "arbitrary", "arbitrary")),
        name="seq_dft",
        interpret=interpret,
    )(cs, ss, fa.reshape(b, s, FOURIER_WIDTH), fb.reshape(b, s, FOURIER_WIDTH))
    return y.reshape(b * s, FOURIER_WIDTH)


def _mix_kernel(x_ref, o0_ref, o1_ref, o2_ref, l0_ref, l1_ref, l2_ref, y_ref, sga_ref, sgb_ref,
                wat_ref, wfo_ref, wout_ref, gf_ref, wqy_ref, sk_ref, hx_ref,
                h_ref, hnt_ref, st_ref, ostage_ref, lstage_ref):
    tm = x_ref.shape[0]

    def token_order(ref, stage, gi):
        dilation = ATTN_GROUPS[gi][1]
        if dilation == 1:
            return ref[0].astype(F32)
        cols = []
        for c in range(ref.shape[-1] // LANES):
            for r in range(dilation):
                stage[gi - 1, c, pl.ds(r, tm // dilation, stride=dilation), :] = (
                    ref[r, :, c * LANES:(c + 1) * LANES].astype(F32))
            cols.append(stage[gi - 1, c])
        return cols[0] if len(cols) == 1 else jnp.concatenate(cols, axis=1)

    lds = [token_order(ref, lstage_ref, gi) for gi, ref in enumerate((l0_ref, l1_ref, l2_ref))]
    mx = jnp.maximum(jnp.maximum(lds[0], lds[1]), lds[2])
    ws = [jnp.exp(ld - mx) for ld in lds]
    inv = 1.0 / (ws[0] + ws[1] + ws[2])
    attn = jnp.zeros((tm, ATTN_GROUP_WIDTH), F32)
    for gi, ref in enumerate((o0_ref, o1_ref, o2_ref)):
        wexp = jnp.dot((ws[gi] * inv).astype(BF16), hx_ref[...], preferred_element_type=F32)
        attn = attn + wexp * token_order(ref, ostage_ref, gi)
    attn = jnp.dot(attn.astype(BF16), wat_ref[...], preferred_element_type=F32)
    four = jnp.dot(y_ref[...], wfo_ref[...], preferred_element_type=F32)
    mix = sga_ref[...].astype(F32) * four + sgb_ref[...].astype(F32) * attn
    h = x_ref[...] + jnp.dot(mix.astype(BF16), wout_ref[...], preferred_element_type=F32)
    h_ref[...] = h
    ms = jnp.mean(h * h, axis=-1, keepdims=True)
    hn32 = h * lax.rsqrt(ms + NORM_EPS) * gf_ref[...]
    hn = hn32.astype(BF16)
    hnt_ref[...] = hn32.T.astype(BF16)
    qy = jnp.dot(hn, wqy_ref[...], preferred_element_type=F32).astype(BF16)
    half = PEER_KEY_DIM // 2
    for hc in range(2 * PEER_HEADS):
        st_ref[hc] = lax.dot_general(sk_ref[hc], qy[:, hc * half:(hc + 1) * half],
                                     (((1,), (1,)), ((), ())), preferred_element_type=F32)


def _mix(x2, o_list, ld_list, yf, sga, sgb, w_attn, w_fourier, w_out, g_ffn, w_query, sub_keys,
         interpret):
    t = x2.shape[0]
    tm = TM_MIX
    spt = o_list[0].shape[2] // tm
    tok = lambda w: pl.BlockSpec((tm, w), lambda i: (i, 0))
    grp = lambda w: [pl.BlockSpec((None, d, tm // d, w), lambda i: (i // spt, 0, i % spt, 0))
                     for _, d in ATTN_GROUPS]
    head_expand = (jnp.arange(LANES)[:, None]
                   == LD_LANES * (jnp.arange(ATTN_GROUP_WIDTH)[None, :] // HEAD_DIM)).astype(BF16)
    n_staged = sum(1 for _, d in ATTN_GROUPS if d > 1)
    sk = jnp.stack([sub_keys[:, 0][:, jnp.array(_PAIR_ORDER)], sub_keys[:, 1]], axis=1)
    sk = sk.reshape(2 * PEER_HEADS, N_KEYS, PEER_KEY_DIM // 2).astype(BF16)
    wat, wfo, wout, wqy = (w_attn.astype(BF16), w_fourier.astype(BF16), w_out.astype(BF16),
                           w_query.astype(BF16))
    return pl.pallas_call(
        _mix_kernel,
        out_shape=(jax.ShapeDtypeStruct((t, D_MODEL), F32),
                   jax.ShapeDtypeStruct((D_MODEL, t), BF16),
                   jax.ShapeDtypeStruct((2 * PEER_HEADS, N_KEYS, t), F32)),
        grid=(t // tm,),
        in_specs=[tok(D_MODEL)] + grp(ATTN_GROUP_WIDTH) + grp(LANES) + [tok(FOURIER_WIDTH),
                  tok(D_MODEL), tok(D_MODEL),
                  _const_spec(wat.shape), _const_spec(wfo.shape), _const_spec(wout.shape),
                  _const_spec((1, D_MODEL)), _const_spec(wqy.shape), _const_spec(sk.shape),
                  _const_spec(head_expand.shape)],
        out_specs=(tok(D_MODEL), pl.BlockSpec((D_MODEL, tm), lambda i: (0, i)),
                   pl.BlockSpec((2 * PEER_HEADS, N_KEYS, tm), lambda i: (0, 0, i))),
        scratch_shapes=[pltpu.VMEM((n_staged, HEADS_PER_GROUP, tm, HEAD_DIM), F32),
                        pltpu.VMEM((n_staged, 1, tm, LANES), F32)],
        compiler_params=_cparams(("arbitrary",)),
        name="mix",
        interpret=interpret,
    )(x2, *o_list, *ld_list, yf, sga, sgb, wat, wfo, wout, g_ffn.reshape(1, D_MODEL), wqy, sk,
      head_expand)


N_TOP = PEER_TOPK + 1
N_PAIRS = N_KEYS // 2
_PAIR_ORDER = tuple(range(0, N_KEYS, 2)) + tuple(range(1, N_KEYS, 2))


def _top_rows(vals, n):
    out = []
    cur = vals
    for _ in range(n):
        m = jnp.max(cur, axis=0, keepdims=True)
        out.append(m)
        cur = jnp.where(cur == m, -jnp.inf, cur)
    return out


def _topk_kernel(st_ref, e1_ref, e2_ref, th_ref):
    for h in range(PEER_HEADS):
        s1 = st_ref[2 * h]
        s2 = st_ref[2 * h + 1]
        a = _top_rows(s1, N_TOP)
        b = _top_rows(s2, N_TOP)
        ea = [jnp.exp(ai - a[0]) for ai in a]
        eb = [jnp.exp(bi - b[0]) for bi in b]
        cand = [ea[i] * eb[j] for i in range(N_TOP) for j in range(N_TOP)
                if (i + 1) * (j + 1) <= N_TOP]
        cand = jnp.concatenate(cand, axis=0)
        top = _top_rows(cand, N_TOP)
        z = top[0]
        for r in range(1, PEER_TOPK):
            z = z + top[r]
        inv_z = 1.0 / z
        e1 = jnp.exp(s1 - a[0])
        e1_ref[:, 2 * h, :] = e1[:N_PAIRS]
        e1_ref[:, 2 * h + 1, :] = e1[N_PAIRS:]
        e2_ref[h] = jnp.exp(s2 - b[0]) * inv_z
        th_ref[h:h + 1, :] = 0.5 * (top[PEER_TOPK - 1] + top[PEER_TOPK]) * inv_z


def _topk(st, interpret):
    t = st.shape[2]
    tm = TM_TOPK
    blk = lambda n: pl.BlockSpec((n, N_KEYS, tm), lambda i: (0, 0, i))
    return pl.pallas_call(
        _topk_kernel,
        out_shape=(jax.ShapeDtypeStruct((N_PAIRS, 2 * PEER_HEADS, t), F32),
                   jax.ShapeDtypeStruct((PEER_HEADS, N_KEYS, t), F32),
                   jax.ShapeDtypeStruct((PEER_HEADS, t), F32)),
        grid=(t // tm,),
        in_specs=[blk(2 * PEER_HEADS)],
        out_specs=(pl.BlockSpec((N_PAIRS, 2 * PEER_HEADS, tm), lambda i: (0, 0, i)),
                   blk(PEER_HEADS),
                   pl.BlockSpec((PEER_HEADS, tm), lambda i: (0, i))),
        compiler_params=_cparams(("arbitrary",)),
        name="peer_topk",
        interpret=interpret,
    )(st)


SB_PEER = 2 * N_KEYS
KB_PEER = 32


def _peer_kernel(hnt_ref, u_ref, vt_ref, e1_ref, e2_ref, th_ref, h_ref, y_ref,
                 at0_ref, at1_ref, ht0_ref, ht1_ref, acc_ref, join_ref, *, nj):
    s = pl.program_id(0)
    tm = hnt_ref.shape[1]
    j_out = lax.rem(jnp.maximum(s - 2, 0), nj)

    @pl.when(s == 0)
    def _():
        for ref in (at0_ref, at1_ref, ht0_ref, ht1_ref):
            ref[...] = jnp.zeros_like(ref)

    @pl.when(j_out == 0)
    def _():
        acc_ref[...] = jnp.zeros_like(acc_ref)

    def stages(at_w, at_r, ht_w, ht_r):
        def slice_body(q, carry):
            r0 = pl.multiple_of(q * SB_PEER, SB_PEER)
            at_new = jnp.dot(u_ref[pl.ds(r0, SB_PEER), :], hnt_ref[...],
                             preferred_element_type=F32)
            at_w[pl.ds(r0, SB_PEER), :] = at_new
            join = at_new[SB_PEER - 8:, tm - LANES:]
            for par in range(2):
                for tc in range(tm // LANES):
                    lanes = slice(tc * LANES, (tc + 1) * LANES)
                    for kb in range(N_KEYS // KB_PEER):
                        keys = slice(kb * KB_PEER, (kb + 1) * KB_PEER)
                        ra = pl.multiple_of(r0 + par * N_KEYS + kb * KB_PEER, KB_PEER)
                        gate = jnp.zeros((KB_PEER, LANES), F32)
                        for h in range(PEER_HEADS):
                            row = 2 * h + par
                            p = e1_ref[q, row:row + 1, lanes] * e2_ref[h, keys, lanes]
                            gate = gate + jnp.where(p >= th_ref[h:h + 1, lanes], p, 0.0)
                        a = at_r[pl.ds(ra, KB_PEER), lanes]
                        act = jax.nn.gelu(a) * gate
                        ht_w[pl.ds(ra, KB_PEER), lanes] = act.astype(BF16)
                        join = join + act[:8]
            out_new = jnp.dot(vt_ref[pl.ds(r0, SB_PEER), :], ht_r[...],
                              preferred_element_type=F32)
            acc_ref[pl.ds(r0, SB_PEER), :] += out_new
            return carry + join + out_new[SB_PEER - 8:, tm - LANES:]

        jv = lax.fori_loop(0, u_ref.shape[0] // SB_PEER, slice_body, jnp.zeros((8, LANES), F32))
        join_ref[...] = jv

    parity = lax.rem(s, 2)

    @pl.when(parity == 0)
    def _():
        stages(at0_ref, at1_ref, ht1_ref, ht0_ref)

    @pl.when(parity == 1)
    def _():
        stages(at1_ref, at0_ref, ht0_ref, ht1_ref)

    @pl.when((j_out == nj - 1) & (s >= 2))
    def _():
        y_ref[...] = h_ref[...] + acc_ref[...].T


def _peer(hnt, h, e1, e2, th, expert_u, expert_v, interpret):
    t = hnt.shape[1]
    tm, eb = TM_PEER, EB_PEER
    assert eb == D_MODEL
    nj = N_EXPERTS // eb
    n_pairs = (t // tm) * nj
    u = expert_u.astype(BF16)
    vt = expert_v.astype(BF16).T
    tile = lambda s: s // nj
    blk = lambda s: lax.rem(s, nj)
    s1 = lambda s: jnp.minimum(s, n_pairs - 1)
    s2 = lambda s: jnp.clip(s - 1, 0, n_pairs - 1)
    s3 = lambda s: jnp.maximum(s - 2, 0)
    return pl.pallas_call(
        functools.partial(_peer_kernel, nj=nj),
        out_shape=jax.ShapeDtypeStruct((t, D_MODEL), F32),
        grid=(n_pairs + 2,),
        in_specs=[pl.BlockSpec((D_MODEL, tm), lambda s: (0, tile(s1(s)))),
                  pl.BlockSpec((eb, D_MODEL), lambda s: (blk(s1(s)), 0)),
                  pl.BlockSpec((D_MODEL, eb), lambda s: (0, blk(s3(s)))),
                  pl.BlockSpec((eb // SB_PEER, 2 * PEER_HEADS, tm),
                               lambda s: (blk(s2(s)), 0, tile(s2(s)))),
                  pl.BlockSpec((PEER_HEADS, N_KEYS, tm), lambda s: (0, 0, tile(s2(s)))),
                  pl.BlockSpec((PEER_HEADS, tm), lambda s: (0, tile(s2(s)))),
                  pl.BlockSpec((tm, D_MODEL), lambda s: (tile(s3(s)), 0))],
        out_specs=pl.BlockSpec((tm, D_MODEL), lambda s: (tile(s3(s)), 0)),
        scratch_shapes=[pltpu.VMEM((eb, tm), F32), pltpu.VMEM((eb, tm), F32),
                        pltpu.VMEM((eb, tm), BF16), pltpu.VMEM((eb, tm), BF16),
                        pltpu.VMEM((D_MODEL, tm), F32), pltpu.VMEM((8, LANES), F32)],
        compiler_params=_cparams(("arbitrary",)),
        name="peer_experts",
        interpret=interpret,
    )(hnt, u, vt, e1, e2, th, h)


def _hybrid_block(x, g_mix, w_in, q_gain, k_gain, w_fourier, w_attn, w_out, g_ffn, w_query,
                  sub_keys, expert_u, expert_v, interpret=False):
    b, s, _ = x.shape
    x2 = x.reshape(b * s, D_MODEL)
    fa, fb, q, k, v, sga, sgb = _proj(x2, s, g_mix, w_in, q_gain, k_gain, interpret)
    o_list, ld_list = [], []
    for gi, (_, dilation) in enumerate(ATTN_GROUPS):
        o, ld = _attn_group(q[gi], k[gi], v[gi], dilation, interpret)
        o_list.append(o)
        ld_list.append(ld)
    yf = _seq_dft(fa, fb, b, s, interpret)
    h, hnt, st = _mix(x2, o_list, ld_list, yf, sga, sgb, w_attn, w_fourier, w_out, g_ffn,
                      w_query, sub_keys, interpret)
    e1, e2, th = _topk(st, interpret)
    y = _peer(hnt, h, e1, e2, th, expert_u, expert_v, interpret)
    return y.reshape(b, s, D_MODEL)


def kernel(x_prompt, x_sample, g_mix, w_in, q_gain, k_gain, w_fourier, w_attn, w_out, g_ffn,
           w_query, sub_keys, expert_u, expert_v):
    y_prompt, y_sample = x_prompt, x_sample
    for layer in range(g_mix.shape[0]):
        params = (g_mix[layer], w_in[layer], q_gain[layer], k_gain[layer], w_fourier[layer],
                  w_attn[layer], w_out[layer], g_ffn[layer], w_query[layer], sub_keys[layer],
                  expert_u[layer], expert_v[layer])
        y_prompt = _hybrid_block(y_prompt, *params)
        y_sample = _hybrid_block(y_sample, *params)
    return (y_prompt, y_sample)
```

```python
import functools
import math

import jax
import jax.numpy as jnp
from jax import lax
from jax.experimental import pallas as pl
from jax.experimental.pallas import tpu as pltpu

F32 = jnp.float32
BF16 = jnp.bfloat16

D_MODEL = 1024
FOURIER_GROUPS = 4
FOURIER_GROUP_DIM = 128
FOURIER_WIDTH = FOURIER_GROUPS * FOURIER_GROUP_DIM
ATTN_GROUPS = ((128, 1), (512, 4), (2048, 16))
N_ATTN_GROUPS = 3
HEADS_PER_GROUP = 4
HEAD_DIM = 128
ATTN_GROUP_WIDTH = HEADS_PER_GROUP * HEAD_DIM
ATTN_WIDTH = N_ATTN_GROUPS * ATTN_GROUP_WIDTH
ROPE_DIM = HEAD_DIM // 4
ROPE_THETA = 500000.0
N_KEYS = 128
N_EXPERTS = N_KEYS * N_KEYS
PEER_HEADS = 8
PEER_TOPK = 16
PEER_KEY_DIM = 256
NORM_EPS = 1e-6
NEG_INF = -1e30

LANES = 128
VMEM_LIMIT = 56 * 1024 * 1024

TM_PROJ = 512
TM_MIX = 512
TM_TOPK = 512
TM_PEER = 512
EB_PEER = 1024
QB_ATTN = 128
DFT_TS = 1024
DFT_TK = 2048


def _cparams(sem):
    return pltpu.CompilerParams(dimension_semantics=sem, vmem_limit_bytes=VMEM_LIMIT)


def _const_spec(shape):
    nd = len(shape)
    return pl.BlockSpec(shape, lambda *_: (0,) * nd)


def _proj_kernel(x_ref, g_ref, wf_ref, wq_ref, wk_ref, wv_ref, wga_ref, wgb_ref,
                 qg_ref, kg_ref, cc_ref, sc_ref, rc_ref, rs1_ref, rs2_ref,
                 fa_ref, fb_ref, q0_ref, q1_ref, q2_ref, k0_ref, k1_ref, k2_ref,
                 v0_ref, v1_ref, v2_ref, sga_ref, sgb_ref, stage_ref):
    tm = x_ref.shape[0]
    hpg = HEADS_PER_GROUP
    stage_slot = [0]

    def store_group(out_ref, gi, cols):
        dilation = ATTN_GROUPS[gi][1]
        if dilation == 1:
            for hl, c in enumerate(cols):
                out_ref[0, :, hl * HEAD_DIM:(hl + 1) * HEAD_DIM] = c.astype(BF16)
            return
        slot = stage_slot[0]
        stage_slot[0] += 1
        for hl, c in enumerate(cols):
            stage_ref[slot, hl] = c
            for r in range(dilation):
                out_ref[r, :, hl * HEAD_DIM:(hl + 1) * HEAD_DIM] = stage_ref[
                    slot, hl, pl.ds(r, tm // dilation, stride=dilation), :].astype(BF16)

    x = x_ref[...]
    ms = jnp.mean(x * x, axis=-1, keepdims=True)
    xn = (x * lax.rsqrt(ms + NORM_EPS) * g_ref[...]).astype(BF16)

    f = jnp.dot(xn, wf_ref[...], preferred_element_type=F32).astype(BF16)
    fa_ref[...] = jnp.dot(f, cc_ref[...], preferred_element_type=F32).astype(BF16)
    fb_ref[...] = jnp.dot(f, sc_ref[...], preferred_element_type=F32).astype(BF16)

    rc = rc_ref[...]
    rs1 = rs1_ref[...]
    rs2 = rs2_ref[...]

    def norm_rope(w_ref, gain_ref, out_refs, scale):
        y = jnp.dot(xn, w_ref[...], preferred_element_type=F32)
        for gi in range(N_ATTN_GROUPS):
            cols = []
            for hh in range(gi * hpg, (gi + 1) * hpg):
                yh = y[:, hh * HEAD_DIM:(hh + 1) * HEAD_DIM]
                hms = jnp.mean(yh * yh, axis=-1, keepdims=True)
                yn = yh * lax.rsqrt(hms + NORM_EPS) * gain_ref[gi:gi + 1, :]
                r = (yn * rc + pltpu.roll(yn, LANES - ROPE_DIM // 2, 1) * rs1
                     + pltpu.roll(yn, ROPE_DIM // 2, 1) * rs2)
                cols.append(r * scale)
            store_group(out_refs[gi], gi, cols)

    norm_rope(wq_ref, qg_ref, (q0_ref, q1_ref, q2_ref), HEAD_DIM ** -0.5)
    norm_rope(wk_ref, kg_ref, (k0_ref, k1_ref, k2_ref), 1.0)
    yv = jnp.dot(xn, wv_ref[...], preferred_element_type=F32)
    for gi, out_ref in enumerate((v0_ref, v1_ref, v2_ref)):
        store_group(out_ref, gi, [yv[:, hh * HEAD_DIM:(hh + 1) * HEAD_DIM]
                                  for hh in range(gi * hpg, (gi + 1) * hpg)])
    sga_ref[...] = jax.nn.sigmoid(
        jnp.dot(xn, wga_ref[...], preferred_element_type=F32)).astype(BF16)
    sgb_ref[...] = jax.nn.sigmoid(
        jnp.dot(xn, wgb_ref[...], preferred_element_type=F32)).astype(BF16)


def _rope_tables(s):
    half = ROPE_DIM // 2
    inv_freq = ROPE_THETA ** (-jnp.arange(half, dtype=F32) * 2.0 / ROPE_DIM)
    ang = jnp.arange(s, dtype=F32)[:, None] * inv_freq[None, :]
    cos, sin = jnp.cos(ang), jnp.sin(ang)
    zeros = jnp.zeros((s, HEAD_DIM - ROPE_DIM), F32)
    zh = jnp.zeros((s, half), F32)
    rc = jnp.concatenate([cos, cos, jnp.ones((s, HEAD_DIM - ROPE_DIM), F32)], axis=1)
    rs1 = jnp.concatenate([-sin, zh, zeros], axis=1)
    rs2 = jnp.concatenate([zh, sin, zeros], axis=1)
    return rc, rs1, rs2


def _channel_dft_mats():
    n = FOURIER_GROUP_DIM
    idx = (jnp.arange(n)[:, None] * jnp.arange(n)[None, :]) % n
    ang = idx.astype(F32) * (2.0 * math.pi / n)
    eye = jnp.eye(FOURIER_GROUPS, dtype=F32)
    cc = jnp.kron(eye, jnp.cos(ang) * n ** -0.5)
    sc = jnp.kron(eye, jnp.sin(ang) * n ** -0.5)
    return cc.astype(BF16), sc.astype(BF16)


def _proj(x2, s, g_mix, w_in, q_gain, k_gain, interpret):
    t = x2.shape[0]
    tm = TM_PROJ
    c0 = FOURIER_WIDTH
    c1 = c0 + ATTN_WIDTH
    c2 = c1 + ATTN_WIDTH
    c3 = c2 + ATTN_WIDTH
    c4 = c3 + D_MODEL
    wb = w_in.astype(BF16)
    wf, wq, wk, wv, wga, wgb = (wb[:, :c0], wb[:, c0:c1], wb[:, c1:c2], wb[:, c2:c3],
                                wb[:, c3:c4], wb[:, c4:])
    cc, sc = _channel_dft_mats()
    rc, rs1, rs2 = _rope_tables(s)
    spt = s // tm
    b = t // s
    tok = lambda w: pl.BlockSpec((tm, w), lambda i: (i, 0))
    pos = pl.BlockSpec((tm, HEAD_DIM), lambda i: (i % spt, 0))
    grp_shapes = [jax.ShapeDtypeStruct((b, d, s // d, ATTN_GROUP_WIDTH), BF16)
                  for _, d in ATTN_GROUPS]
    grp_specs = [pl.BlockSpec((None, d, tm // d, ATTN_GROUP_WIDTH),
                              lambda i: (i // spt, 0, i % spt, 0)) for _, d in ATTN_GROUPS]
    out_shape = ([jax.ShapeDtypeStruct((t, FOURIER_WIDTH), BF16)] * 2 + grp_shapes * 3
                 + [jax.ShapeDtypeStruct((t, D_MODEL), BF16)] * 2)
    n_staged = 3 * sum(1 for _, d in ATTN_GROUPS if d > 1)
    outs = pl.pallas_call(
        _proj_kernel,
        out_shape=out_shape,
        grid=(t // tm,),
        in_specs=[tok(D_MODEL), _const_spec((1, D_MODEL)),
                  _const_spec(wf.shape), _const_spec(wq.shape), _const_spec(wk.shape),
                  _const_spec(wv.shape), _const_spec(wga.shape), _const_spec(wgb.shape),
                  _const_spec(q_gain.shape), _const_spec(k_gain.shape),
                  _const_spec(cc.shape), _const_spec(sc.shape), pos, pos, pos],
        out_specs=[tok(FOURIER_WIDTH)] * 2 + grp_specs * 3 + [tok(D_MODEL)] * 2,
        scratch_shapes=[pltpu.VMEM((n_staged, HEADS_PER_GROUP, tm, HEAD_DIM), F32)],
        compiler_params=_cparams(("arbitrary",)),
        name="proj",
        interpret=interpret,
    )(x2, g_mix.reshape(1, D_MODEL), wf, wq, wk, wv, wga, wgb, q_gain, k_gain, cc, sc,
      rc, rs1, rs2)
    fa, fb = outs[0], outs[1]
    q, k, v = outs[2:5], outs[5:8], outs[8:11]
    return fa, fb, q, k, v, outs[11], outs[12]


LD_LANES = LANES // HEADS_PER_GROUP


def _attn_kernel(q_ref, k_ref, v_ref, o_ref, ld_ref, *, sub_len, heads):
    kb = min(2 * QB_ATTN, sub_len)
    span = QB_ATTN // 2
    first_head = pl.program_id(2) * heads

    if heads < HEADS_PER_GROUP:
        @pl.when(first_head == 0)
        def _():
            ld_ref[...] = jnp.zeros_like(ld_ref)

    def body(i, carry):
        start = pl.multiple_of(i * QB_ATTN, QB_ATTN)
        ks = pl.multiple_of(jnp.clip(start - span, 0, sub_len - kb), span)
        qpos = start + lax.broadcasted_iota(jnp.int32, (QB_ATTN, kb), 0)
        kpos = ks + lax.broadcasted_iota(jnp.int32, (QB_ATTN, kb), 1)
        in_band = jnp.abs(kpos - qpos) <= span
        lane_head = lax.broadcasted_iota(jnp.int32, (QB_ATTN, LANES), 1) // LD_LANES
        if heads < HEADS_PER_GROUP:
            ld_all = ld_ref[pl.ds(start, QB_ATTN), :]
        else:
            ld_all = jnp.zeros((QB_ATTN, LANES), F32)
        for hl in range(heads):
            cols = slice(hl * HEAD_DIM, (hl + 1) * HEAD_DIM)
            q = q_ref[pl.ds(start, QB_ATTN), cols]
            k = k_ref[pl.ds(ks, kb), cols]
            v = v_ref[pl.ds(ks, kb), cols]
            s = lax.dot_general(q, k, (((1,), (1,)), ((), ())), preferred_element_type=F32)
            s = jnp.where(in_band, s, NEG_INF)
            m = jnp.max(s, axis=1, keepdims=True)
            p = jnp.exp(s - m)
            l = jnp.sum(p, axis=1, keepdims=True)
            o = jnp.dot(p.astype(BF16), v, preferred_element_type=F32) / l
            o_ref[pl.ds(start, QB_ATTN), cols] = o.astype(BF16)
            ld_all = jnp.where(lane_head == first_head + hl, m + jnp.log(l), ld_all)
        ld_ref[pl.ds(start, QB_ATTN), :] = ld_all
        return carry

    nq = sub_len // QB_ATTN
    lax.fori_loop(0, nq, body, 0, unroll=min(nq, HEADS_PER_GROUP // heads))


ATTN_MAX_WIDE_LEN = 2048


def _attn_group(q, k, v, dilation, interpret):
    b, _, sub_len, _ = q.shape
    assert sub_len % QB_ATTN == 0
    heads = HEADS_PER_GROUP if sub_len <= ATTN_MAX_WIDE_LEN else 1
    width = heads * HEAD_DIM
    spec = pl.BlockSpec((None, None, sub_len, width), lambda bi, r, hb: (bi, r, 0, hb))
    return pl.pallas_call(
        functools.partial(_attn_kernel, sub_len=sub_len, heads=heads),
        out_shape=(jax.ShapeDtypeStruct(q.shape, BF16),
                   jax.ShapeDtypeStruct((b, dilation, sub_len, LANES), F32)),
        grid=(b, dilation, HEADS_PER_GROUP // heads),
        in_specs=[spec, spec, spec],
        out_specs=(spec, pl.BlockSpec((None, None, sub_len, LANES),
                                      lambda bi, r, hb: (bi, r, 0, 0))),
        compiler_params=_cparams(("arbitrary", "arbitrary", "arbitrary")),
        name=f"attn_d{dilation}",
        interpret=interpret,
    )(q, k, v)


def _dft_kernel(cs_ref, ss_ref, a_ref, b_ref, y_ref, acc_ref):
    kk = pl.program_id(2)

    @pl.when(kk == 0)
    def _():
        acc_ref[...] = jnp.zeros_like(acc_ref)

    acc_ref[...] += (jnp.dot(cs_ref[...], a_ref[...], preferred_element_type=F32)
                     - jnp.dot(ss_ref[...], b_ref[...], preferred_element_type=F32))

    @pl.when(kk == pl.num_programs(2) - 1)
    def _():
        y_ref[...] = acc_ref[...].astype(BF16)


def _seq_dft_mats(s):
    lo = LANES
    hi = s // lo
    rows = jnp.arange(s, dtype=jnp.int32)[:, None]
    scale = 2.0 * math.pi / s
    ang_hi = ((rows * (jnp.arange(hi, dtype=jnp.int32)[None, :] * lo)) % s).astype(F32) * scale
    ang_lo = ((rows * jnp.arange(lo, dtype=jnp.int32)[None, :]) % s).astype(F32) * scale
    ch, sh = jnp.cos(ang_hi)[:, :, None], jnp.sin(ang_hi)[:, :, None]
    cl, sl = jnp.cos(ang_lo)[:, None, :], jnp.sin(ang_lo)[:, None, :]
    norm = s ** -0.5
    cs = ((ch * cl - sh * sl) * norm).astype(BF16).reshape(s, s)
    ss = ((sh * cl + ch * sl) * norm).astype(BF16).reshape(s, s)
    return cs, ss


def _seq_dft(fa, fb, b, s, interpret):
    ts = min(DFT_TS, s)
    tk = min(DFT_TK, s)
    cs, ss = _seq_dft_mats(s)
    mat = pl.BlockSpec((ts, tk), lambda bi, i, kk: (i, kk))
    vec = pl.BlockSpec((None, tk, FOURIER_WIDTH), lambda bi, i, kk: (bi, kk, 0))
    y = pl.pallas_call(
        _dft_kernel,
        out_shape=jax.ShapeDtypeStruct((b, s, FOURIER_WIDTH), BF16),
        grid=(b, s // ts, s // tk),
        in_specs=[mat, mat, vec, vec],
        out_specs=pl.BlockSpec((None, ts, FOURIER_WIDTH), lambda bi, i, kk: (bi, i, 0)),
        scratch_shapes=[pltpu.VMEM((ts, FOURIER_WIDTH), F32)],
        compiler_params=_cparams(("arbitrary", "arbitrary", "arbitrary")),
        name="seq_dft",
        interpret=interpret,
    )(cs, ss, fa.reshape(b, s, FOURIER_WIDTH), fb.reshape(b, s, FOURIER_WIDTH))
    return y.reshape(b * s, FOURIER_WIDTH)


def _mix_kernel(x_ref, o0_ref, o1_ref, o2_ref, l0_ref, l1_ref, l2_ref, y_ref, sga_ref, sgb_ref,
                wat_ref, wfo_ref, wout_ref, gf_ref, wqy_ref, sk_ref, hx_ref,
                h_ref, hnt_ref, st_ref, ostage_ref, lstage_ref):
    tm = x_ref.shape[0]

    def token_order(ref, stage, gi):
        dilation = ATTN_GROUPS[gi][1]
        if dilation == 1:
            return ref[0].astype(F32)
        cols = []
        for c in range(ref.shape[-1] // LANES):
            for r in range(dilation):
                stage[gi - 1, c, pl.ds(r, tm // dilation, stride=dilation), :] = (
                    ref[r, :, c * LANES:(c + 1) * LANES].astype(F32))
            cols.append(stage[gi - 1, c])
        return cols[0] if len(cols) == 1 else jnp.concatenate(cols, axis=1)

    lds = [token_order(ref, lstage_ref, gi) for gi, ref in enumerate((l0_ref, l1_ref, l2_ref))]
    mx = jnp.maximum(jnp.maximum(lds[0], lds[1]), lds[2])
    ws = [jnp.exp(ld - mx) for ld in lds]
    inv = 1.0 / (ws[0] + ws[1] + ws[2])
    attn = jnp.zeros((tm, ATTN_GROUP_WIDTH), F32)
    for gi, ref in enumerate((o0_ref, o1_ref, o2_ref)):
        wexp = jnp.dot((ws[gi] * inv).astype(BF16), hx_ref[...], preferred_element_type=F32)
        attn = attn + wexp * token_order(ref, ostage_ref, gi)
    attn = jnp.dot(attn.astype(BF16), wat_ref[...], preferred_element_type=F32)
    four = jnp.dot(y_ref[...], wfo_ref[...], preferred_element_type=F32)
    mix = sga_ref[...].astype(F32) * four + sgb_ref[...].astype(F32) * attn
    h = x_ref[...] + jnp.dot(mix.astype(BF16), wout_ref[...], preferred_element_type=F32)
    h_ref[...] = h
    ms = jnp.mean(h * h, axis=-1, keepdims=True)
    hn32 = h * lax.rsqrt(ms + NORM_EPS) * gf_ref[...]
    hn = hn32.astype(BF16)
    hnt_ref[...] = hn32.T.astype(BF16)
    qy = jnp.dot(hn, wqy_ref[...], preferred_element_type=F32).astype(BF16)
    half = PEER_KEY_DIM // 2
    for hc in range(2 * PEER_HEADS):
        st_ref[hc] = lax.dot_general(sk_ref[hc], qy[:, hc * half:(hc + 1) * half],
                                     (((1,), (1,)), ((), ())), preferred_element_type=F32)


def _mix(x2, o_list, ld_list, yf, sga, sgb, w_attn, w_fourier, w_out, g_ffn, w_query, sub_keys,
         interpret):
    t = x2.shape[0]
    tm = TM_MIX
    spt = o_list[0].shape[2] // tm
    tok = lambda w: pl.BlockSpec((tm, w), lambda i: (i, 0))
    grp = lambda w: [pl.BlockSpec((None, d, tm // d, w), lambda i: (i // spt, 0, i % spt, 0))
                     for _, d in ATTN_GROUPS]
    head_expand = (jnp.arange(LANES)[:, None]
                   == LD_LANES * (jnp.arange(ATTN_GROUP_WIDTH)[None, :] // HEAD_DIM)).astype(BF16)
    n_staged = sum(1 for _, d in ATTN_GROUPS if d > 1)
    sk = jnp.stack([sub_keys[:, 0][:, jnp.array(_PAIR_ORDER)], sub_keys[:, 1]], axis=1)
    sk = sk.reshape(2 * PEER_HEADS, N_KEYS, PEER_KEY_DIM // 2).astype(BF16)
    wat, wfo, wout, wqy = (w_attn.astype(BF16), w_fourier.astype(BF16), w_out.astype(BF16),
                           w_query.astype(BF16))
    return pl.pallas_call(
        _mix_kernel,
        out_shape=(jax.ShapeDtypeStruct((t, D_MODEL), F32),
                   jax.ShapeDtypeStruct((D_MODEL, t), BF16),
                   jax.ShapeDtypeStruct((2 * PEER_HEADS, N_KEYS, t), F32)),
        grid=(t // tm,),
        in_specs=[tok(D_MODEL)] + grp(ATTN_GROUP_WIDTH) + grp(LANES) + [tok(FOURIER_WIDTH),
                  tok(D_MODEL), tok(D_MODEL),
                  _const_spec(wat.shape), _const_spec(wfo.shape), _const_spec(wout.shape),
                  _const_spec((1, D_MODEL)), _const_spec(wqy.shape), _const_spec(sk.shape),
                  _const_spec(head_expand.shape)],
        out_specs=(tok(D_MODEL), pl.BlockSpec((D_MODEL, tm), lambda i: (0, i)),
                   pl.BlockSpec((2 * PEER_HEADS, N_KEYS, tm), lambda i: (0, 0, i))),
        scratch_shapes=[pltpu.VMEM((n_staged, HEADS_PER_GROUP, tm, HEAD_DIM), F32),
                        pltpu.VMEM((n_staged, 1, tm, LANES), F32)],
        compiler_params=_cparams(("arbitrary",)),
        name="mix",
        interpret=interpret,
    )(x2, *o_list, *ld_list, yf, sga, sgb, wat, wfo, wout, g_ffn.reshape(1, D_MODEL), wqy, sk,
      head_expand)


N_PAIRS = N_KEYS // 2
_PAIR_ORDER = tuple(range(0, N_KEYS, 2)) + tuple(range(1, N_KEYS, 2))
SUBLANES = 8


def _batcher_network(n):
    def merge(lo, hi, r):
        step = r * 2
        if step < hi - lo:
            yield from merge(lo, hi, step)
            yield from merge(lo + r, hi, step)
            for i in range(lo + r, hi - r, step):
                yield (i, i + r)
        else:
            yield (lo, lo + r)

    def sort(lo, hi):
        if hi - lo >= 1:
            mid = lo + (hi - lo) // 2
            yield from sort(lo, mid)
            yield from sort(mid + 1, hi)
            yield from merge(lo, hi, 1)

    return tuple(sort(0, n - 1))


_SORT16 = _batcher_network(PEER_TOPK)


def _compare_exchange(v, i, j):
    v[i], v[j] = jnp.maximum(v[i], v[j]), jnp.minimum(v[i], v[j])


def _top16_sorted(keys):
    n = PEER_TOPK
    v = [keys[SUBLANES * k:SUBLANES * (k + 1)] for k in range(n)]
    for i, j in _SORT16:
        _compare_exchange(v, i, j)
    shift = SUBLANES // 2
    while shift >= 1:
        other = [pltpu.roll(x, shift, 0) for x in v]
        v = [jnp.maximum(v[k], other[n - 1 - k]) for k in range(n)]
        d = n // 2
        while d >= 1:
            for i in range(n):
                if not i & d:
                    _compare_exchange(v, i, i + d)
            d //= 2
        shift //= 2
    return [x[0:1] for x in v]


_CAND_PAIRS = tuple((i, j) for i in range(PEER_TOPK) for j in range(PEER_TOPK)
                    if (i + 1) * (j + 1) <= PEER_TOPK)
_CAND_ROWS = -(-len(_CAND_PAIRS) // SUBLANES) * SUBLANES
TIE_STEP = 2.0 ** -17


def _kth_slot_value(vals, k):
    slot = lax.broadcasted_iota(jnp.int32, vals.shape, 0).astype(F32)
    bumped = vals * (1.0 + slot * TIE_STEP)
    cur = bumped
    m = None
    for _ in range(k):
        m = jnp.max(cur, axis=0, keepdims=True)
        cur = jnp.where(cur == m, -1.0, cur)
    return bumped, m


def _candidates(fa, fb):
    rows = [fa[i] * fb[j] for i, j in _CAND_PAIRS]
    rows.append(jnp.zeros((_CAND_ROWS - len(rows),) + rows[0].shape[1:], rows[0].dtype))
    return jnp.concatenate(rows, axis=0)


def _topk_kernel(st_ref, e1_ref, e2_ref, th_ref):
    tm = st_ref.shape[2]
    for h in range(PEER_HEADS):
        for tc in range(tm // LANES):
            lanes = slice(tc * LANES, (tc + 1) * LANES)
            s1 = st_ref[2 * h, :, lanes]
            s2 = st_ref[2 * h + 1, :, lanes]
            a = _top16_sorted(s1)
            b = _top16_sorted(s2)
            ea = [jnp.exp(ai - a[0]) for ai in a]
            eb = [jnp.exp(bi - b[0]) for bi in b]
            cand = _candidates(ea, eb)
            bumped, kth = _kth_slot_value(cand, PEER_TOPK)
            z = jnp.sum(jnp.where(bumped >= kth, cand, 0.0), axis=0, keepdims=True)
            inv_z = 1.0 / z
            cand_b = _candidates([(x * inv_z).astype(BF16) for x in ea],
                                 [x.astype(BF16) for x in eb]).astype(F32)
            _, kth_b = _kth_slot_value(cand_b, PEER_TOPK)
            th_ref[h:h + 1, lanes] = kth_b.astype(BF16).astype(F32)
            e1 = jnp.exp(s1 - a[0]) * inv_z
            e1_ref[:, 2 * h, lanes] = e1[:N_PAIRS]
            e1_ref[:, 2 * h + 1, lanes] = e1[N_PAIRS:]
            e2_ref[h, :, lanes] = jnp.exp(s2 - b[0]).astype(BF16)


def _topk(st, interpret):
    t = st.shape[2]
    tm = TM_TOPK
    blk = lambda n: pl.BlockSpec((n, N_KEYS, tm), lambda i: (0, 0, i))
    return pl.pallas_call(
        _topk_kernel,
        out_shape=(jax.ShapeDtypeStruct((N_PAIRS, 2 * PEER_HEADS, t), F32),
                   jax.ShapeDtypeStruct((PEER_HEADS, N_KEYS, t), BF16),
                   jax.ShapeDtypeStruct((PEER_HEADS, t), F32)),
        grid=(t // tm,),
        in_specs=[blk(2 * PEER_HEADS)],
        out_specs=(pl.BlockSpec((N_PAIRS, 2 * PEER_HEADS, tm), lambda i: (0, 0, i)),
                   blk(PEER_HEADS),
                   pl.BlockSpec((PEER_HEADS, tm), lambda i: (0, i))),
        compiler_params=_cparams(("arbitrary",)),
        name="peer_topk",
        interpret=interpret,
    )(st)


SB_PEER = 2 * N_KEYS
KB_PEER = 32


def _peer_kernel(hnt_ref, u_ref, vt_ref, e1_ref, e2_ref, th_ref, h_ref, y_ref,
                 at0_ref, at1_ref, ht0_ref, ht1_ref, acc_ref, join_ref, *, nj):
    s = pl.program_id(0)
    tm = hnt_ref.shape[1]
    j_out = lax.rem(jnp.maximum(s - 2, 0), nj)

    @pl.when(s == 0)
    def _():
        for ref in (at0_ref, at1_ref, ht0_ref, ht1_ref):
            ref[...] = jnp.zeros_like(ref)

    @pl.when(j_out == 0)
    def _():
        acc_ref[...] = jnp.zeros_like(acc_ref)

    def stages(at_w, at_r, ht_w, ht_r):
        def slice_body(q, carry):
            r0 = pl.multiple_of(q * SB_PEER, SB_PEER)
            at_new = jnp.dot(u_ref[pl.ds(r0, SB_PEER), :], hnt_ref[...],
                             preferred_element_type=F32)
            at_w[pl.ds(r0, SB_PEER), :] = at_new
            join = at_new[SB_PEER - 8:, tm - LANES:]
            for par in range(2):
                for tc in range(tm // LANES):
                    lanes = slice(tc * LANES, (tc + 1) * LANES)
                    e1b = [jnp.broadcast_to(
                        e1_ref[q, 2 * h + par:2 * h + par + 1, lanes].astype(BF16),
                        (KB_PEER, LANES)) for h in range(PEER_HEADS)]
                    thb = [jnp.broadcast_to(th_ref[h:h + 1, lanes].astype(BF16),
                                            (KB_PEER, LANES)) for h in range(PEER_HEADS)]
                    for kb in range(N_KEYS // KB_PEER):
                        keys = slice(kb * KB_PEER, (kb + 1) * KB_PEER)
                        ra = pl.multiple_of(r0 + par * N_KEYS + kb * KB_PEER, KB_PEER)
                        gate = jnp.zeros((KB_PEER, LANES), BF16)
                        for h in range(PEER_HEADS):
                            p = e1b[h] * e2_ref[h, keys, lanes]
                            gate = gate + jnp.where(p >= thb[h], p, jnp.zeros_like(p))
                        a = at_r[pl.ds(ra, KB_PEER), lanes].astype(BF16)
                        act = jax.nn.gelu(a) * gate
                        ht_w[pl.ds(ra, KB_PEER), lanes] = act
                        join = join + act[:8].astype(F32)
            out_new = jnp.dot(vt_ref[pl.ds(r0, SB_PEER), :], ht_r[...],
                              preferred_element_type=F32)
            acc_ref[pl.ds(r0, SB_PEER), :] += out_new
            return carry + join + out_new[SB_PEER - 8:, tm - LANES:]

        jv = lax.fori_loop(0, u_ref.shape[0] // SB_PEER, slice_body, jnp.zeros((8, LANES), F32))
        join_ref[...] = jv

    parity = lax.rem(s, 2)

    @pl.when(parity == 0)
    def _():
        stages(at0_ref, at1_ref, ht1_ref, ht0_ref)

    @pl.when(parity == 1)
    def _():
        stages(at1_ref, at0_ref, ht0_ref, ht1_ref)

    @pl.when((j_out == nj - 1) & (s >= 2))
    def _():
        y_ref[...] = h_ref[...] + acc_ref[...].T


def _peer(hnt, h, e1, e2, th, expert_u, expert_v, interpret):
    t = hnt.shape[1]
    tm, eb = TM_PEER, EB_PEER
    assert eb == D_MODEL
    nj = N_EXPERTS // eb
    n_pairs = (t // tm) * nj
    u = expert_u.astype(BF16)
    vt = expert_v.astype(BF16).T
    tile = lambda s: s // nj
    blk = lambda s: lax.rem(s, nj)
    s1 = lambda s: jnp.minimum(s, n_pairs - 1)
    s2 = lambda s: jnp.clip(s - 1, 0, n_pairs - 1)
    s3 = lambda s: jnp.maximum(s - 2, 0)
    return pl.pallas_call(
        functools.partial(_peer_kernel, nj=nj),
        out_shape=jax.ShapeDtypeStruct((t, D_MODEL), F32),
        grid=(n_pairs + 2,),
        in_specs=[pl.BlockSpec((D_MODEL, tm), lambda s: (0, tile(s1(s)))),
                  pl.BlockSpec((eb, D_MODEL), lambda s: (blk(s1(s)), 0)),
                  pl.BlockSpec((D_MODEL, eb), lambda s: (0, blk(s3(s)))),
                  pl.BlockSpec((eb // SB_PEER, 2 * PEER_HEADS, tm),
                               lambda s: (blk(s2(s)), 0, tile(s2(s)))),
                  pl.BlockSpec((PEER_HEADS, N_KEYS, tm), lambda s: (0, 0, tile(s2(s)))),
                  pl.BlockSpec((PEER_HEADS, tm), lambda s: (0, tile(s2(s)))),
                  pl.BlockSpec((tm, D_MODEL), lambda s: (tile(s3(s)), 0))],
        out_specs=pl.BlockSpec((tm, D_MODEL), lambda s: (tile(s3(s)), 0)),
        scratch_shapes=[pltpu.VMEM((eb, tm), F32), pltpu.VMEM((eb, tm), F32),
                        pltpu.VMEM((eb, tm), BF16), pltpu.VMEM((eb, tm), BF16),
                        pltpu.VMEM((D_MODEL, tm), F32), pltpu.VMEM((8, LANES), F32)],
        compiler_params=_cparams(("arbitrary",)),
        name="peer_experts",
        interpret=interpret,
    )(hnt, u, vt, e1, e2, th, h)


def _hybrid_block(x, g_mix, w_in, q_gain, k_gain, w_fourier, w_attn, w_out, g_ffn, w_query,
                  sub_keys, expert_u, expert_v, interpret=False):
    b, s, _ = x.shape
    x2 = x.reshape(b * s, D_MODEL)
    fa, fb, q, k, v, sga, sgb = _proj(x2, s, g_mix, w_in, q_gain, k_gain, interpret)
    o_list, ld_list = [], []
    for gi, (_, dilation) in enumerate(ATTN_GROUPS):
        o, ld = _attn_group(q[gi], k[gi], v[gi], dilation, interpret)
        o_list.append(o)
        ld_list.append(ld)
    yf = _seq_dft(fa, fb, b, s, interpret)
    h, hnt, st = _mix(x2, o_list, ld_list, yf, sga, sgb, w_attn, w_fourier, w_out, g_ffn,
                      w_query, sub_keys, interpret)
    e1, e2, th = _topk(st, interpret)
    y = _peer(hnt, h, e1, e2, th, expert_u, expert_v, interpret)
    return y.reshape(b, s, D_MODEL)


def kernel(x_prompt, x_sample, g_mix, w_in, q_gain, k_gain, w_fourier, w_attn, w_out, g_ffn,
           w_query, sub_keys, expert_u, expert_v):
    y_prompt, y_sample = x_prompt, x_sample
    for layer in range(g_mix.shape[0]):
        params = (g_mix[layer], w_in[layer], q_gain[layer], k_gain[layer], w_fourier[layer],
                  w_attn[layer], w_out[layer], g_ffn[layer], w_query[layer], sub_keys[layer],
                  expert_u[layer], expert_v[layer])
        y_prompt = _hybrid_block(y_prompt, *params)
        y_sample = _hybrid_block(y_sample, *params)
    return (y_prompt, y_sample)
```

```python
import functools
import math

import jax
import jax.numpy as jnp
from jax import lax
from jax.experimental import pallas as pl
from jax.experimental.pallas import tpu as pltpu

F32 = jnp.float32
BF16 = jnp.bfloat16

D_MODEL = 1024
FOURIER_GROUPS = 4
FOURIER_GROUP_DIM = 128
FOURIER_WIDTH = FOURIER_GROUPS * FOURIER_GROUP_DIM
ATTN_GROUPS = ((128, 1), (512, 4), (2048, 16))
N_ATTN_GROUPS = 3
HEADS_PER_GROUP = 4
HEAD_DIM = 128
ATTN_GROUP_WIDTH = HEADS_PER_GROUP * HEAD_DIM
ATTN_WIDTH = N_ATTN_GROUPS * ATTN_GROUP_WIDTH
ROPE_DIM = HEAD_DIM // 4
ROPE_THETA = 500000.0
N_KEYS = 128
N_EXPERTS = N_KEYS * N_KEYS
PEER_HEADS = 8
PEER_TOPK = 16
PEER_KEY_DIM = 256
NORM_EPS = 1e-6
NEG_INF = -1e30

LANES = 128
VMEM_LIMIT = 56 * 1024 * 1024

TM_PROJ = 512
TM_MIX = 512
TM_TOPK = 512
TM_PEER = 512
EB_PEER = 1024
QB_ATTN = 128
DFT_TS = 1024
DFT_TK = 2048


def _cparams(sem):
    return pltpu.CompilerParams(dimension_semantics=sem, vmem_limit_bytes=VMEM_LIMIT)


def _const_spec(shape):
    nd = len(shape)
    return pl.BlockSpec(shape, lambda *_: (0,) * nd)


def _proj_kernel(x_ref, g_ref, wf_ref, wq_ref, wk_ref, wv_ref, wga_ref, wgb_ref,
                 qg_ref, kg_ref, cc_ref, sc_ref, rc_ref, rs1_ref, rs2_ref,
                 fa_ref, fb_ref, q0_ref, q1_ref, q2_ref, k0_ref, k1_ref, k2_ref,
                 v0_ref, v1_ref, v2_ref, sga_ref, sgb_ref, stage_ref):
    tm = x_ref.shape[0]
    hpg = HEADS_PER_GROUP
    stage_slot = [0]

    def store_group(out_ref, gi, cols):
        dilation = ATTN_GROUPS[gi][1]
        if dilation == 1:
            for hl, c in enumerate(cols):
                out_ref[0, :, hl * HEAD_DIM:(hl + 1) * HEAD_DIM] = c.astype(BF16)
            return
        slot = stage_slot[0]
        stage_slot[0] += 1
        for hl, c in enumerate(cols):
            stage_ref[slot, hl] = c
            for r in range(dilation):
                out_ref[r, :, hl * HEAD_DIM:(hl + 1) * HEAD_DIM] = stage_ref[
                    slot, hl, pl.ds(r, tm // dilation, stride=dilation), :].astype(BF16)

    x = x_ref[...]
    ms = jnp.mean(x * x, axis=-1, keepdims=True)
    xn = (x * lax.rsqrt(ms + NORM_EPS) * g_ref[...]).astype(BF16)

    f = jnp.dot(xn, wf_ref[...], preferred_element_type=F32).astype(BF16)
    fa_ref[...] = jnp.dot(f, cc_ref[...], preferred_element_type=F32).astype(BF16)
    fb_ref[...] = jnp.dot(f, sc_ref[...], preferred_element_type=F32).astype(BF16)

    rc = rc_ref[...]
    rs1 = rs1_ref[...]
    rs2 = rs2_ref[...]

    def norm_rope(w_ref, gain_ref, out_refs, scale):
        y = jnp.dot(xn, w_ref[...], preferred_element_type=F32)
        for gi in range(N_ATTN_GROUPS):
            cols = []
            for hh in range(gi * hpg, (gi + 1) * hpg):
                yh = y[:, hh * HEAD_DIM:(hh + 1) * HEAD_DIM]
                hms = jnp.mean(yh * yh, axis=-1, keepdims=True)
                yn = yh * lax.rsqrt(hms + NORM_EPS) * gain_ref[gi:gi + 1, :]
                r = (yn * rc + pltpu.roll(yn, LANES - ROPE_DIM // 2, 1) * rs1
                     + pltpu.roll(yn, ROPE_DIM // 2, 1) * rs2)
                cols.append(r * scale)
            store_group(out_refs[gi], gi, cols)

    norm_rope(wq_ref, qg_ref, (q0_ref, q1_ref, q2_ref), HEAD_DIM ** -0.5)
    norm_rope(wk_ref, kg_ref, (k0_ref, k1_ref, k2_ref), 1.0)
    yv = jnp.dot(xn, wv_ref[...], preferred_element_type=F32)
    for gi, out_ref in enumerate((v0_ref, v1_ref, v2_ref)):
        store_group(out_ref, gi, [yv[:, hh * HEAD_DIM:(hh + 1) * HEAD_DIM]
                                  for hh in range(gi * hpg, (gi + 1) * hpg)])
    sga_ref[...] = jax.nn.sigmoid(
        jnp.dot(xn, wga_ref[...], preferred_element_type=F32)).astype(BF16)
    sgb_ref[...] = jax.nn.sigmoid(
        jnp.dot(xn, wgb_ref[...], preferred_element_type=F32)).astype(BF16)


def _rope_tables(s):
    half = ROPE_DIM // 2
    inv_freq = ROPE_THETA ** (-jnp.arange(half, dtype=F32) * 2.0 / ROPE_DIM)
    ang = jnp.arange(s, dtype=F32)[:, None] * inv_freq[None, :]
    cos, sin = jnp.cos(ang), jnp.sin(ang)
    zeros = jnp.zeros((s, HEAD_DIM - ROPE_DIM), F32)
    zh = jnp.zeros((s, half), F32)
    rc = jnp.concatenate([cos, cos, jnp.ones((s, HEAD_DIM - ROPE_DIM), F32)], axis=1)
    rs1 = jnp.concatenate([-sin, zh, zeros], axis=1)
    rs2 = jnp.concatenate([zh, sin, zeros], axis=1)
    return rc, rs1, rs2


def _channel_dft_mats():
    n = FOURIER_GROUP_DIM
    idx = (jnp.arange(n)[:, None] * jnp.arange(n)[None, :]) % n
    ang = idx.astype(F32) * (2.0 * math.pi / n)
    eye = jnp.eye(FOURIER_GROUPS, dtype=F32)
    cc = jnp.kron(eye, jnp.cos(ang) * n ** -0.5)
    sc = jnp.kron(eye, jnp.sin(ang) * n ** -0.5)
    return cc.astype(BF16), sc.astype(BF16)


def _proj(x2, s, g_mix, w_in, q_gain, k_gain, interpret):
    t = x2.shape[0]
    tm = TM_PROJ
    c0 = FOURIER_WIDTH
    c1 = c0 + ATTN_WIDTH
    c2 = c1 + ATTN_WIDTH
    c3 = c2 + ATTN_WIDTH
    c4 = c3 + D_MODEL
    wb = w_in.astype(BF16)
    wf, wq, wk, wv, wga, wgb = (wb[:, :c0], wb[:, c0:c1], wb[:, c1:c2], wb[:, c2:c3],
                                wb[:, c3:c4], wb[:, c4:])
    cc, sc = _channel_dft_mats()
    rc, rs1, rs2 = _rope_tables(s)
    spt = s // tm
    b = t // s
    tok = lambda w: pl.BlockSpec((tm, w), lambda i: (i, 0))
    pos = pl.BlockSpec((tm, HEAD_DIM), lambda i: (i % spt, 0))
    grp_shapes = [jax.ShapeDtypeStruct((b, d, s // d, ATTN_GROUP_WIDTH), BF16)
                  for _, d in ATTN_GROUPS]
    grp_specs = [pl.BlockSpec((None, d, tm // d, ATTN_GROUP_WIDTH),
                              lambda i: (i // spt, 0, i % spt, 0)) for _, d in ATTN_GROUPS]
    out_shape = ([jax.ShapeDtypeStruct((t, FOURIER_WIDTH), BF16)] * 2 + grp_shapes * 3
                 + [jax.ShapeDtypeStruct((t, D_MODEL), BF16)] * 2)
    n_staged = 3 * sum(1 for _, d in ATTN_GROUPS if d > 1)
    outs = pl.pallas_call(
        _proj_kernel,
        out_shape=out_shape,
        grid=(t // tm,),
        in_specs=[tok(D_MODEL), _const_spec((1, D_MODEL)),
                  _const_spec(wf.shape), _const_spec(wq.shape), _const_spec(wk.shape),
                  _const_spec(wv.shape), _const_spec(wga.shape), _const_spec(wgb.shape),
                  _const_spec(q_gain.shape), _const_spec(k_gain.shape),
                  _const_spec(cc.shape), _const_spec(sc.shape), pos, pos, pos],
        out_specs=[tok(FOURIER_WIDTH)] * 2 + grp_specs * 3 + [tok(D_MODEL)] * 2,
        scratch_shapes=[pltpu.VMEM((n_staged, HEADS_PER_GROUP, tm, HEAD_DIM), F32)],
        compiler_params=_cparams(("arbitrary",)),
        name="proj",
        interpret=interpret,
    )(x2, g_mix.reshape(1, D_MODEL), wf, wq, wk, wv, wga, wgb, q_gain, k_gain, cc, sc,
      rc, rs1, rs2)
    fa, fb = outs[0], outs[1]
    q, k, v = outs[2:5], outs[5:8], outs[8:11]
    return fa, fb, q, k, v, outs[11], outs[12]


LD_LANES = LANES // HEADS_PER_GROUP


def _attn_kernel(q_ref, k_ref, v_ref, o_ref, ld_ref, *, sub_len, heads):
    kb = min(2 * QB_ATTN, sub_len)
    span = QB_ATTN // 2
    first_head = pl.program_id(2) * heads

    if heads < HEADS_PER_GROUP:
        @pl.when(first_head == 0)
        def _():
            ld_ref[...] = jnp.zeros_like(ld_ref)

    def body(i, carry):
        start = pl.multiple_of(i * QB_ATTN, QB_ATTN)
        ks = pl.multiple_of(jnp.clip(start - span, 0, sub_len - kb), span)
        qpos = start + lax.broadcasted_iota(jnp.int32, (QB_ATTN, kb), 0)
        kpos = ks + lax.broadcasted_iota(jnp.int32, (QB_ATTN, kb), 1)
        in_band = jnp.abs(kpos - qpos) <= span
        lane_head = lax.broadcasted_iota(jnp.int32, (QB_ATTN, LANES), 1) // LD_LANES
        if heads < HEADS_PER_GROUP:
            ld_all = ld_ref[pl.ds(start, QB_ATTN), :]
        else:
            ld_all = jnp.zeros((QB_ATTN, LANES), F32)
        for hl in range(heads):
            cols = slice(hl * HEAD_DIM, (hl + 1) * HEAD_DIM)
            q = q_ref[pl.ds(start, QB_ATTN), cols]
            k = k_ref[pl.ds(ks, kb), cols]
            v = v_ref[pl.ds(ks, kb), cols]
            s = lax.dot_general(q, k, (((1,), (1,)), ((), ())), preferred_element_type=F32)
            s = jnp.where(in_band, s, NEG_INF)
            m = jnp.max(s, axis=1, keepdims=True)
            p = jnp.exp(s - m)
            l = jnp.sum(p, axis=1, keepdims=True)
            o = jnp.dot(p.astype(BF16), v, preferred_element_type=F32) / l
            o_ref[pl.ds(start, QB_ATTN), cols] = o.astype(BF16)
            ld_all = jnp.where(lane_head == first_head + hl, m + jnp.log(l), ld_all)
        ld_ref[pl.ds(start, QB_ATTN), :] = ld_all
        return carry

    nq = sub_len // QB_ATTN
    lax.fori_loop(0, nq, body, 0, unroll=min(nq, HEADS_PER_GROUP // heads))


ATTN_MAX_WIDE_LEN = 2048


def _attn_group(q, k, v, dilation, interpret):
    b, _, sub_len, _ = q.shape
    assert sub_len % QB_ATTN == 0
    heads = HEADS_PER_GROUP if sub_len <= ATTN_MAX_WIDE_LEN else 1
    width = heads * HEAD_DIM
    spec = pl.BlockSpec((None, None, sub_len, width), lambda bi, r, hb: (bi, r, 0, hb))
    return pl.pallas_call(
        functools.partial(_attn_kernel, sub_len=sub_len, heads=heads),
        out_shape=(jax.ShapeDtypeStruct(q.shape, BF16),
                   jax.ShapeDtypeStruct((b, dilation, sub_len, LANES), F32)),
        grid=(b, dilation, HEADS_PER_GROUP // heads),
        in_specs=[spec, spec, spec],
        out_specs=(spec, pl.BlockSpec((None, None, sub_len, LANES),
                                      lambda bi, r, hb: (bi, r, 0, 0))),
        compiler_params=_cparams(("arbitrary", "arbitrary", "arbitrary")),
        name=f"attn_d{dilation}",
        interpret=interpret,
    )(q, k, v)


def _dft_kernel(cs_ref, ss_ref, a_ref, b_ref, y_ref, acc_ref):
    kk = pl.program_id(2)

    @pl.when(kk == 0)
    def _():
        acc_ref[...] = jnp.zeros_like(acc_ref)

    acc_ref[...] += (jnp.dot(cs_ref[...], a_ref[...], preferred_element_type=F32)
                     - jnp.dot(ss_ref[...], b_ref[...], preferred_element_type=F32))

    @pl.when(kk == pl.num_programs(2) - 1)
    def _():
        y_ref[...] = acc_ref[...].astype(BF16)


def _seq_dft_mats(s):
    lo = LANES
    hi = s // lo
    rows = jnp.arange(s, dtype=jnp.int32)[:, None]
    scale = 2.0 * math.pi / s
    ang_hi = ((rows * (jnp.arange(hi, dtype=jnp.int32)[None, :] * lo)) % s).astype(F32) * scale
    ang_lo = ((rows * jnp.arange(lo, dtype=jnp.int32)[None, :]) % s).astype(F32) * scale
    ch, sh = jnp.cos(ang_hi)[:, :, None], jnp.sin(ang_hi)[:, :, None]
    cl, sl = jnp.cos(ang_lo)[:, None, :], jnp.sin(ang_lo)[:, None, :]
    norm = s ** -0.5
    cs = ((ch * cl - sh * sl) * norm).astype(BF16).reshape(s, s)
    ss = ((sh * cl + ch * sl) * norm).astype(BF16).reshape(s, s)
    return cs, ss


def _seq_dft(fa, fb, b, s, interpret):
    ts = min(DFT_TS, s)
    tk = min(DFT_TK, s)
    cs, ss = _seq_dft_mats(s)
    mat = pl.BlockSpec((ts, tk), lambda bi, i, kk: (i, kk))
    vec = pl.BlockSpec((None, tk, FOURIER_WIDTH), lambda bi, i, kk: (bi, kk, 0))
    y = pl.pallas_call(
        _dft_kernel,
        out_shape=jax.ShapeDtypeStruct((b, s, FOURIER_WIDTH), BF16),
        grid=(b, s // ts, s // tk),
        in_specs=[mat, mat, vec, vec],
        out_specs=pl.BlockSpec((None, ts, FOURIER_WIDTH), lambda bi, i, kk: (bi, i, 0)),
        scratch_shapes=[pltpu.VMEM((ts, FOURIER_WIDTH), F32)],
        compiler_params=_cparams(("arbitrary", "arbitrary", "arbitrary")),
        name="seq_dft",
        interpret=interpret,
    )(cs, ss, fa.reshape(b, s, FOURIER_WIDTH), fb.reshape(b, s, FOURIER_WIDTH))
    return y.reshape(b * s, FOURIER_WIDTH)


def _mix_kernel(x_ref, o0_ref, o1_ref, o2_ref, l0_ref, l1_ref, l2_ref, y_ref, sga_ref, sgb_ref,
                wat_ref, wfo_ref, wout_ref, gf_ref, wqy_ref, sk_ref, hx_ref,
                h_ref, hnt_ref, st_ref, ostage_ref, lstage_ref):
    tm = x_ref.shape[0]

    def token_order(ref, stage, gi):
        dilation = ATTN_GROUPS[gi][1]
        if dilation == 1:
            return ref[0].astype(F32)
        cols = []
        for c in range(ref.shape[-1] // LANES):
            for r in range(dilation):
                stage[gi - 1, c, pl.ds(r, tm // dilation, stride=dilation), :] = (
                    ref[r, :, c * LANES:(c + 1) * LANES].astype(F32))
            cols.append(stage[gi - 1, c])
        return cols[0] if len(cols) == 1 else jnp.concatenate(cols, axis=1)

    lds = [token_order(ref, lstage_ref, gi) for gi, ref in enumerate((l0_ref, l1_ref, l2_ref))]
    mx = jnp.maximum(jnp.maximum(lds[0], lds[1]), lds[2])
    ws = [jnp.exp(ld - mx) for ld in lds]
    inv = 1.0 / (ws[0] + ws[1] + ws[2])
    attn = jnp.zeros((tm, ATTN_GROUP_WIDTH), F32)
    for gi, ref in enumerate((o0_ref, o1_ref, o2_ref)):
        wexp = jnp.dot((ws[gi] * inv).astype(BF16), hx_ref[...], preferred_element_type=F32)
        attn = attn + wexp * token_order(ref, ostage_ref, gi)
    attn = jnp.dot(attn.astype(BF16), wat_ref[...], preferred_element_type=F32)
    four = jnp.dot(y_ref[...], wfo_ref[...], preferred_element_type=F32)
    mix = sga_ref[...].astype(F32) * four + sgb_ref[...].astype(F32) * attn
    h = x_ref[...] + jnp.dot(mix.astype(BF16), wout_ref[...], preferred_element_type=F32)
    h_ref[...] = h
    ms = jnp.mean(h * h, axis=-1, keepdims=True)
    hn32 = h * lax.rsqrt(ms + NORM_EPS) * gf_ref[...]
    hn = hn32.astype(BF16)
    hnt_ref[...] = hn32.T.astype(BF16)
    qy = jnp.dot(hn, wqy_ref[...], preferred_element_type=F32).astype(BF16)
    half = PEER_KEY_DIM // 2
    for hc in range(2 * PEER_HEADS):
        st_ref[hc] = lax.dot_general(sk_ref[hc], qy[:, hc * half:(hc + 1) * half],
                                     (((1,), (1,)), ((), ())), preferred_element_type=F32)


def _mix(x2, o_list, ld_list, yf, sga, sgb, w_attn, w_fourier, w_out, g_ffn, w_query, sub_keys,
         interpret):
    t = x2.shape[0]
    tm = TM_MIX
    spt = o_list[0].shape[2] // tm
    tok = lambda w: pl.BlockSpec((tm, w), lambda i: (i, 0))
    grp = lambda w: [pl.BlockSpec((None, d, tm // d, w), lambda i: (i // spt, 0, i % spt, 0))
                     for _, d in ATTN_GROUPS]
    head_expand = (jnp.arange(LANES)[:, None]
                   == LD_LANES * (jnp.arange(ATTN_GROUP_WIDTH)[None, :] // HEAD_DIM)).astype(BF16)
    n_staged = sum(1 for _, d in ATTN_GROUPS if d > 1)
    sk = jnp.stack([sub_keys[:, 0][:, jnp.array(_PAIR_ORDER)], sub_keys[:, 1]], axis=1)
    sk = sk.reshape(2 * PEER_HEADS, N_KEYS, PEER_KEY_DIM // 2).astype(BF16)
    wat, wfo, wout, wqy = (w_attn.astype(BF16), w_fourier.astype(BF16), w_out.astype(BF16),
                           w_query.astype(BF16))
    return pl.pallas_call(
        _mix_kernel,
        out_shape=(jax.ShapeDtypeStruct((t, D_MODEL), F32),
                   jax.ShapeDtypeStruct((D_MODEL, t), BF16),
                   jax.ShapeDtypeStruct((2 * PEER_HEADS, N_KEYS, t), F32)),
        grid=(t // tm,),
        in_specs=[tok(D_MODEL)] + grp(ATTN_GROUP_WIDTH) + grp(LANES) + [tok(FOURIER_WIDTH),
                  tok(D_MODEL), tok(D_MODEL),
                  _const_spec(wat.shape), _const_spec(wfo.shape), _const_spec(wout.shape),
                  _const_spec((1, D_MODEL)), _const_spec(wqy.shape), _const_spec(sk.shape),
                  _const_spec(head_expand.shape)],
        out_specs=(tok(D_MODEL), pl.BlockSpec((D_MODEL, tm), lambda i: (0, i)),
                   pl.BlockSpec((2 * PEER_HEADS, N_KEYS, tm), lambda i: (0, 0, i))),
        scratch_shapes=[pltpu.VMEM((n_staged, HEADS_PER_GROUP, tm, HEAD_DIM), F32),
                        pltpu.VMEM((n_staged, 1, tm, LANES), F32)],
        compiler_params=_cparams(("arbitrary",)),
        name="mix",
        interpret=interpret,
    )(x2, *o_list, *ld_list, yf, sga, sgb, wat, wfo, wout, g_ffn.reshape(1, D_MODEL), wqy, sk,
      head_expand)


N_PAIRS = N_KEYS // 2
_PAIR_ORDER = tuple(range(0, N_KEYS, 2)) + tuple(range(1, N_KEYS, 2))
SUBLANES = 8


def _batcher_network(n):
    def merge(lo, hi, r):
        step = r * 2
        if step < hi - lo:
            yield from merge(lo, hi, step)
            yield from merge(lo + r, hi, step)
            for i in range(lo + r, hi - r, step):
                yield (i, i + r)
        else:
            yield (lo, lo + r)

    def sort(lo, hi):
        if hi - lo >= 1:
            mid = lo + (hi - lo) // 2
            yield from sort(lo, mid)
            yield from sort(mid + 1, hi)
            yield from merge(lo, hi, 1)

    return tuple(sort(0, n - 1))


_SORT16 = _batcher_network(PEER_TOPK)


def _compare_exchange(v, i, j):
    v[i], v[j] = jnp.maximum(v[i], v[j]), jnp.minimum(v[i], v[j])


def _top16_sorted(keys):
    n = PEER_TOPK
    v = [keys[SUBLANES * k:SUBLANES * (k + 1)] for k in range(n)]
    for i, j in _SORT16:
        _compare_exchange(v, i, j)
    shift = SUBLANES // 2
    while shift >= 1:
        other = [pltpu.roll(x, shift, 0) for x in v]
        v = [jnp.maximum(v[k], other[n - 1 - k]) for k in range(n)]
        d = n // 2
        while d >= 1:
            for i in range(n):
                if not i & d:
                    _compare_exchange(v, i, i + d)
            d //= 2
        shift //= 2
    return [x[0:1] for x in v]


_CAND_PAIRS = tuple((i, j) for i in range(PEER_TOPK) for j in range(PEER_TOPK)
                    if (i + 1) * (j + 1) <= PEER_TOPK)
_CAND_ROWS = -(-len(_CAND_PAIRS) // SUBLANES) * SUBLANES
TIE_STEP = 2.0 ** -17


def _kth_slot_value(vals, k):
    slot = lax.broadcasted_iota(jnp.int32, vals.shape, 0).astype(F32)
    bumped = vals * (1.0 + slot * TIE_STEP)
    cur = bumped
    m = None
    for _ in range(k):
        m = jnp.max(cur, axis=0, keepdims=True)
        cur = jnp.where(cur == m, -1.0, cur)
    return bumped, m


def _candidates(fa, fb):
    rows = [fa[i] * fb[j] for i, j in _CAND_PAIRS]
    rows.append(jnp.zeros((_CAND_ROWS - len(rows),) + rows[0].shape[1:], rows[0].dtype))
    return jnp.concatenate(rows, axis=0)


def _topk_kernel(st_ref, e1_ref, e2_ref, th_ref):
    tm = st_ref.shape[2]
    for h in range(PEER_HEADS):
        for tc in range(tm // LANES):
            lanes = slice(tc * LANES, (tc + 1) * LANES)
            s1 = st_ref[2 * h, :, lanes]
            s2 = st_ref[2 * h + 1, :, lanes]
            a = _top16_sorted(s1)
            b = _top16_sorted(s2)
            ea = [jnp.exp(ai - a[0]) for ai in a]
            eb = [jnp.exp(bi - b[0]) for bi in b]
            cand = _candidates(ea, eb)
            bumped, kth = _kth_slot_value(cand, PEER_TOPK)
            z = jnp.sum(jnp.where(bumped >= kth, cand, 0.0), axis=0, keepdims=True)
            inv_z = 1.0 / z
            cand_b = _candidates([(x * inv_z).astype(BF16) for x in ea],
                                 [x.astype(BF16) for x in eb]).astype(F32)
            _, kth_b = _kth_slot_value(cand_b, PEER_TOPK)
            th_ref[h:h + 1, lanes] = kth_b.astype(BF16).astype(F32)
            e1 = jnp.exp(s1 - a[0]) * inv_z
            e1_ref[:, 2 * h, lanes] = e1[:N_PAIRS]
            e1_ref[:, 2 * h + 1, lanes] = e1[N_PAIRS:]
            e2_ref[h, :, lanes] = jnp.exp(s2 - b[0]).astype(BF16)


def _topk(st, interpret):
    t = st.shape[2]
    tm = TM_TOPK
    blk = lambda n: pl.BlockSpec((n, N_KEYS, tm), lambda i: (0, 0, i))
    return pl.pallas_call(
        _topk_kernel,
        out_shape=(jax.ShapeDtypeStruct((N_PAIRS, 2 * PEER_HEADS, t), F32),
                   jax.ShapeDtypeStruct((PEER_HEADS, N_KEYS, t), BF16),
                   jax.ShapeDtypeStruct((PEER_HEADS, t), F32)),
        grid=(t // tm,),
        in_specs=[blk(2 * PEER_HEADS)],
        out_specs=(pl.BlockSpec((N_PAIRS, 2 * PEER_HEADS, tm), lambda i: (0, 0, i)),
                   blk(PEER_HEADS),
                   pl.BlockSpec((PEER_HEADS, tm), lambda i: (0, i))),
        compiler_params=_cparams(("arbitrary",)),
        name="peer_topk",
        interpret=interpret,
    )(st)


SB_PEER = 2 * N_KEYS
KB_PEER = 32


def _peer_kernel(hnt_ref, u_ref, vt_ref, e1_ref, e2_ref, th_ref, h_ref, y_ref,
                 at0_ref, at1_ref, ht0_ref, ht1_ref, acc_ref, join_ref, *, nj):
    s = pl.program_id(0)
    tm = hnt_ref.shape[1]
    j_out = lax.rem(jnp.maximum(s - 2, 0), nj)

    @pl.when(s == 0)
    def _():
        for ref in (at0_ref, at1_ref, ht0_ref, ht1_ref):
            ref[...] = jnp.zeros_like(ref)

    @pl.when(j_out == 0)
    def _():
        acc_ref[...] = jnp.zeros_like(acc_ref)

    def stages(at_w, at_r, ht_w, ht_r):
        def slice_body(q, carry):
            r0 = pl.multiple_of(q * SB_PEER, SB_PEER)
            at_new = jnp.dot(u_ref[pl.ds(r0, SB_PEER), :], hnt_ref[...],
                             preferred_element_type=F32)
            at_w[pl.ds(r0, SB_PEER), :] = at_new
            join = at_new[SB_PEER - 8:, tm - LANES:]
            for par in range(2):
                for tc in range(tm // LANES):
                    lanes = slice(tc * LANES, (tc + 1) * LANES)
                    e1b = [jnp.broadcast_to(
                        e1_ref[q, 2 * h + par:2 * h + par + 1, lanes].astype(BF16),
                        (KB_PEER, LANES)) for h in range(PEER_HEADS)]
                    thb = [jnp.broadcast_to(th_ref[h:h + 1, lanes].astype(BF16),
                                            (KB_PEER, LANES)) for h in range(PEER_HEADS)]
                    for kb in range(N_KEYS // KB_PEER):
                        keys = slice(kb * KB_PEER, (kb + 1) * KB_PEER)
                        ra = pl.multiple_of(r0 + par * N_KEYS + kb * KB_PEER, KB_PEER)
                        gate = jnp.zeros((KB_PEER, LANES), BF16)
                        for h in range(PEER_HEADS):
                            p = e1b[h] * e2_ref[h, keys, lanes]
                            gate = gate + jnp.where(p >= thb[h], p, jnp.zeros_like(p))
                        a = at_r[pl.ds(ra, KB_PEER), lanes].astype(BF16)
                        act = jax.nn.gelu(a) * gate
                        ht_w[pl.ds(ra, KB_PEER), lanes] = act
                        join = join + act[:8].astype(F32)
            out_new = jnp.dot(vt_ref[pl.ds(r0, SB_PEER), :], ht_r[...],
                              preferred_element_type=F32)
            acc_ref[pl.ds(r0, SB_PEER), :] += out_new
            return carry + join + out_new[SB_PEER - 8:, tm - LANES:]

        jv = lax.fori_loop(0, u_ref.shape[0] // SB_PEER, slice_body, jnp.zeros((8, LANES), F32))
        join_ref[...] = jv

    parity = lax.rem(s, 2)

    @pl.when(parity == 0)
    def _():
        stages(at0_ref, at1_ref, ht1_ref, ht0_ref)

    @pl.when(parity == 1)
    def _():
        stages(at1_ref, at0_ref, ht0_ref, ht1_ref)

    @pl.when((j_out == nj - 1) & (s >= 2))
    def _():
        y_ref[...] = h_ref[...] + acc_ref[...].T


def _peer(hnt, h, e1, e2, th, expert_u, expert_v, interpret):
    t = hnt.shape[1]
    tm, eb = TM_PEER, EB_PEER
    assert eb == D_MODEL
    nj = N_EXPERTS // eb
    n_pairs = (t // tm) * nj
    u = expert_u.astype(BF16)
    vt = jnp.transpose(expert_v.astype(BF16).reshape(nj, eb, D_MODEL), (0, 2, 1))
    tile = lambda s: s // nj
    blk = lambda s: lax.rem(s, nj)
    s1 = lambda s: jnp.minimum(s, n_pairs - 1)
    s2 = lambda s: jnp.clip(s - 1, 0, n_pairs - 1)
    s3 = lambda s: jnp.maximum(s - 2, 0)
    return pl.pallas_call(
        functools.partial(_peer_kernel, nj=nj),
        out_shape=jax.ShapeDtypeStruct((t, D_MODEL), F32),
        grid=(n_pairs + 2,),
        in_specs=[pl.BlockSpec((D_MODEL, tm), lambda s: (0, tile(s1(s)))),
                  pl.BlockSpec((eb, D_MODEL), lambda s: (blk(s1(s)), 0)),
                  pl.BlockSpec((None, D_MODEL, eb), lambda s: (blk(s3(s)), 0, 0)),
                  pl.BlockSpec((eb // SB_PEER, 2 * PEER_HEADS, tm),
                               lambda s: (blk(s2(s)), 0, tile(s2(s)))),
                  pl.BlockSpec((PEER_HEADS, N_KEYS, tm), lambda s: (0, 0, tile(s2(s)))),
                  pl.BlockSpec((PEER_HEADS, tm), lambda s: (0, tile(s2(s)))),
                  pl.BlockSpec((tm, D_MODEL), lambda s: (tile(s3(s)), 0))],
        out_specs=pl.BlockSpec((tm, D_MODEL), lambda s: (tile(s3(s)), 0)),
        scratch_shapes=[pltpu.VMEM((eb, tm), F32), pltpu.VMEM((eb, tm), F32),
                        pltpu.VMEM((eb, tm), BF16), pltpu.VMEM((eb, tm), BF16),
                        pltpu.VMEM((D_MODEL, tm), F32), pltpu.VMEM((8, LANES), F32)],
        compiler_params=_cparams(("arbitrary",)),
        name="peer_experts",
        interpret=interpret,
    )(hnt, u, vt, e1, e2, th, h)


def _hybrid_block(x, g_mix, w_in, q_gain, k_gain, w_fourier, w_attn, w_out, g_ffn, w_query,
                  sub_keys, expert_u, expert_v, interpret=False):
    b, s, _ = x.shape
    x2 = x.reshape(b * s, D_MODEL)
    fa, fb, q, k, v, sga, sgb = _proj(x2, s, g_mix, w_in, q_gain, k_gain, interpret)
    o_list, ld_list = [], []
    for gi, (_, dilation) in enumerate(ATTN_GROUPS):
        o, ld = _attn_group(q[gi], k[gi], v[gi], dilation, interpret)
        o_list.append(o)
        ld_list.append(ld)
    yf = _seq_dft(fa, fb, b, s, interpret)
    h, hnt, st = _mix(x2, o_list, ld_list, yf, sga, sgb, w_attn, w_fourier, w_out, g_ffn,
                      w_query, sub_keys, interpret)
    e1, e2, th = _topk(st, interpret)
    y = _peer(hnt, h, e1, e2, th, expert_u, expert_v, interpret)
    return y.reshape(b, s, D_MODEL)


def kernel(x_prompt, x_sample, g_mix, w_in, q_gain, k_gain, w_fourier, w_attn, w_out, g_ffn,
           w_query, sub_keys, expert_u, expert_v):
    y_prompt, y_sample = x_prompt, x_sample
    for layer in range(g_mix.shape[0]):
        params = (g_mix[layer], w_in[layer], q_gain[layer], k_gain[layer], w_fourier[layer],
                  w_attn[layer], w_out[layer], g_ffn[layer], w_query[layer], sub_keys[layer],
                  expert_u[layer], expert_v[layer])
        y_prompt = _hybrid_block(y_prompt, *params)
        y_sample = _hybrid_block(y_sample, *params)
    return (y_prompt, y_sample)
```

```python
import functools
import math

import jax
import jax.numpy as jnp
from jax import lax
from jax.experimental import pallas as pl
from jax.experimental.pallas import tpu as pltpu

F32 = jnp.float32
BF16 = jnp.bfloat16

D_MODEL = 1024
FOURIER_GROUPS = 4
FOURIER_GROUP_DIM = 128
FOURIER_WIDTH = FOURIER_GROUPS * FOURIER_GROUP_DIM
ATTN_GROUPS = ((128, 1), (512, 4), (2048, 16))
N_ATTN_GROUPS = 3
HEADS_PER_GROUP = 4
HEAD_DIM = 128
ATTN_GROUP_WIDTH = HEADS_PER_GROUP * HEAD_DIM
ATTN_WIDTH = N_ATTN_GROUPS * ATTN_GROUP_WIDTH
ROPE_DIM = HEAD_DIM // 4
ROPE_THETA = 500000.0
N_KEYS = 128
N_EXPERTS = N_KEYS * N_KEYS
PEER_HEADS = 8
PEER_TOPK = 16
PEER_KEY_DIM = 256
NORM_EPS = 1e-6
NEG_INF = -1e30

LANES = 128
VMEM_LIMIT = 56 * 1024 * 1024

TM_PROJ = 512
TM_MIX = 512
TM_TOPK = 512
TM_PEER = 1024
EB_PEER = 1024
QB_ATTN = 128
DFT_TS = 1024
DFT_TK = 2048


def _cparams(sem):
    return pltpu.CompilerParams(dimension_semantics=sem, vmem_limit_bytes=VMEM_LIMIT)


def _const_spec(shape):
    nd = len(shape)
    return pl.BlockSpec(shape, lambda *_: (0,) * nd)


def _proj_kernel(x_ref, g_ref, wf_ref, wq_ref, wk_ref, wv_ref, wga_ref, wgb_ref,
                 qg_ref, kg_ref, cc_ref, sc_ref, rc_ref, rs1_ref, rs2_ref,
                 fa_ref, fb_ref, q0_ref, q1_ref, q2_ref, k0_ref, k1_ref, k2_ref,
                 v0_ref, v1_ref, v2_ref, sga_ref, sgb_ref, stage_ref):
    tm = x_ref.shape[0]
    hpg = HEADS_PER_GROUP
    stage_slot = [0]

    def store_group(out_ref, gi, cols):
        dilation = ATTN_GROUPS[gi][1]
        if dilation == 1:
            for hl, c in enumerate(cols):
                out_ref[0, :, hl * HEAD_DIM:(hl + 1) * HEAD_DIM] = c.astype(BF16)
            return
        slot = stage_slot[0]
        stage_slot[0] += 1
        for hl, c in enumerate(cols):
            stage_ref[slot, hl] = c
            for r in range(dilation):
                out_ref[r, :, hl * HEAD_DIM:(hl + 1) * HEAD_DIM] = stage_ref[
                    slot, hl, pl.ds(r, tm // dilation, stride=dilation), :].astype(BF16)

    x = x_ref[...]
    ms = jnp.mean(x * x, axis=-1, keepdims=True)
    xn = (x * lax.rsqrt(ms + NORM_EPS) * g_ref[...]).astype(BF16)

    f = jnp.dot(xn, wf_ref[...], preferred_element_type=F32).astype(BF16)
    fa_ref[...] = jnp.dot(f, cc_ref[...], preferred_element_type=F32).astype(BF16)
    fb_ref[...] = jnp.dot(f, sc_ref[...], preferred_element_type=F32).astype(BF16)

    rc = rc_ref[...]
    rs1 = rs1_ref[...]
    rs2 = rs2_ref[...]

    def norm_rope(w_ref, gain_ref, out_refs, scale):
        y = jnp.dot(xn, w_ref[...], preferred_element_type=F32)
        for gi in range(N_ATTN_GROUPS):
            cols = []
            for hh in range(gi * hpg, (gi + 1) * hpg):
                yh = y[:, hh * HEAD_DIM:(hh + 1) * HEAD_DIM]
                hms = jnp.mean(yh * yh, axis=-1, keepdims=True)
                yn = yh * lax.rsqrt(hms + NORM_EPS) * gain_ref[gi:gi + 1, :]
                r = (yn * rc + pltpu.roll(yn, LANES - ROPE_DIM // 2, 1) * rs1
                     + pltpu.roll(yn, ROPE_DIM // 2, 1) * rs2)
                cols.append(r * scale)
            store_group(out_refs[gi], gi, cols)

    norm_rope(wq_ref, qg_ref, (q0_ref, q1_ref, q2_ref), HEAD_DIM ** -0.5)
    norm_rope(wk_ref, kg_ref, (k0_ref, k1_ref, k2_ref), 1.0)
    yv = jnp.dot(xn, wv_ref[...], preferred_element_type=F32)
    for gi, out_ref in enumerate((v0_ref, v1_ref, v2_ref)):
        store_group(out_ref, gi, [yv[:, hh * HEAD_DIM:(hh + 1) * HEAD_DIM]
                                  for hh in range(gi * hpg, (gi + 1) * hpg)])
    sga_ref[...] = jax.nn.sigmoid(
        jnp.dot(xn, wga_ref[...], preferred_element_type=F32)).astype(BF16)
    sgb_ref[...] = jax.nn.sigmoid(
        jnp.dot(xn, wgb_ref[...], preferred_element_type=F32)).astype(BF16)


def _rope_tables(s):
    half = ROPE_DIM // 2
    inv_freq = ROPE_THETA ** (-jnp.arange(half, dtype=F32) * 2.0 / ROPE_DIM)
    ang = jnp.arange(s, dtype=F32)[:, None] * inv_freq[None, :]
    cos, sin = jnp.cos(ang), jnp.sin(ang)
    zeros = jnp.zeros((s, HEAD_DIM - ROPE_DIM), F32)
    zh = jnp.zeros((s, half), F32)
    rc = jnp.concatenate([cos, cos, jnp.ones((s, HEAD_DIM - ROPE_DIM), F32)], axis=1)
    rs1 = jnp.concatenate([-sin, zh, zeros], axis=1)
    rs2 = jnp.concatenate([zh, sin, zeros], axis=1)
    return rc, rs1, rs2


def _channel_dft_mats():
    n = FOURIER_GROUP_DIM
    idx = (jnp.arange(n)[:, None] * jnp.arange(n)[None, :]) % n
    ang = idx.astype(F32) * (2.0 * math.pi / n)
    eye = jnp.eye(FOURIER_GROUPS, dtype=F32)
    cc = jnp.kron(eye, jnp.cos(ang) * n ** -0.5)
    sc = jnp.kron(eye, jnp.sin(ang) * n ** -0.5)
    return cc.astype(BF16), sc.astype(BF16)


def _proj(x2, s, g_mix, w_in, q_gain, k_gain, interpret):
    t = x2.shape[0]
    tm = TM_PROJ
    c0 = FOURIER_WIDTH
    c1 = c0 + ATTN_WIDTH
    c2 = c1 + ATTN_WIDTH
    c3 = c2 + ATTN_WIDTH
    c4 = c3 + D_MODEL
    wb = w_in.astype(BF16)
    wf, wq, wk, wv, wga, wgb = (wb[:, :c0], wb[:, c0:c1], wb[:, c1:c2], wb[:, c2:c3],
                                wb[:, c3:c4], wb[:, c4:])
    cc, sc = _channel_dft_mats()
    rc, rs1, rs2 = _rope_tables(s)
    spt = s // tm
    b = t // s
    tok = lambda w: pl.BlockSpec((tm, w), lambda i: (i, 0))
    pos = pl.BlockSpec((tm, HEAD_DIM), lambda i: (i % spt, 0))
    grp_shapes = [jax.ShapeDtypeStruct((b, d, s // d, ATTN_GROUP_WIDTH), BF16)
                  for _, d in ATTN_GROUPS]
    grp_specs = [pl.BlockSpec((None, d, tm // d, ATTN_GROUP_WIDTH),
                              lambda i: (i // spt, 0, i % spt, 0)) for _, d in ATTN_GROUPS]
    out_shape = ([jax.ShapeDtypeStruct((t, FOURIER_WIDTH), BF16)] * 2 + grp_shapes * 3
                 + [jax.ShapeDtypeStruct((t, D_MODEL), BF16)] * 2)
    n_staged = 3 * sum(1 for _, d in ATTN_GROUPS if d > 1)
    outs = pl.pallas_call(
        _proj_kernel,
        out_shape=out_shape,
        grid=(t // tm,),
        in_specs=[tok(D_MODEL), _const_spec((1, D_MODEL)),
                  _const_spec(wf.shape), _const_spec(wq.shape), _const_spec(wk.shape),
                  _const_spec(wv.shape), _const_spec(wga.shape), _const_spec(wgb.shape),
                  _const_spec(q_gain.shape), _const_spec(k_gain.shape),
                  _const_spec(cc.shape), _const_spec(sc.shape), pos, pos, pos],
        out_specs=[tok(FOURIER_WIDTH)] * 2 + grp_specs * 3 + [tok(D_MODEL)] * 2,
        scratch_shapes=[pltpu.VMEM((n_staged, HEADS_PER_GROUP, tm, HEAD_DIM), F32)],
        compiler_params=_cparams(("arbitrary",)),
        name="proj",
        interpret=interpret,
    )(x2, g_mix.reshape(1, D_MODEL), wf, wq, wk, wv, wga, wgb, q_gain, k_gain, cc, sc,
      rc, rs1, rs2)
    fa, fb = outs[0], outs[1]
    q, k, v = outs[2:5], outs[5:8], outs[8:11]
    return fa, fb, q, k, v, outs[11], outs[12]


LD_LANES = LANES // HEADS_PER_GROUP


def _attn_kernel(q_ref, k_ref, v_ref, o_ref, ld_ref, *, sub_len, heads):
    kb = min(2 * QB_ATTN, sub_len)
    span = QB_ATTN // 2
    first_head = pl.program_id(2) * heads

    if heads < HEADS_PER_GROUP:
        @pl.when(first_head == 0)
        def _():
            ld_ref[...] = jnp.zeros_like(ld_ref)

    def body(i, carry):
        start = pl.multiple_of(i * QB_ATTN, QB_ATTN)
        ks = pl.multiple_of(jnp.clip(start - span, 0, sub_len - kb), span)
        qpos = start + lax.broadcasted_iota(jnp.int32, (QB_ATTN, kb), 0)
        kpos = ks + lax.broadcasted_iota(jnp.int32, (QB_ATTN, kb), 1)
        in_band = jnp.abs(kpos - qpos) <= span
        lane_head = lax.broadcasted_iota(jnp.int32, (QB_ATTN, LANES), 1) // LD_LANES
        if heads < HEADS_PER_GROUP:
            ld_all = ld_ref[pl.ds(start, QB_ATTN), :]
        else:
            ld_all = jnp.zeros((QB_ATTN, LANES), F32)
        for hl in range(heads):
            cols = slice(hl * HEAD_DIM, (hl + 1) * HEAD_DIM)
            q = q_ref[pl.ds(start, QB_ATTN), cols]
            k = k_ref[pl.ds(ks, kb), cols]
            v = v_ref[pl.ds(ks, kb), cols]
            s = lax.dot_general(q, k, (((1,), (1,)), ((), ())), preferred_element_type=F32)
            s = jnp.where(in_band, s, NEG_INF)
            m = jnp.max(s, axis=1, keepdims=True)
            p = jnp.exp(s - m)
            l = jnp.sum(p, axis=1, keepdims=True)
            o = jnp.dot(p.astype(BF16), v, preferred_element_type=F32) / l
            o_ref[pl.ds(start, QB_ATTN), cols] = o.astype(BF16)
            ld_all = jnp.where(lane_head == first_head + hl, m + jnp.log(l), ld_all)
        ld_ref[pl.ds(start, QB_ATTN), :] = ld_all
        return carry

    nq = sub_len // QB_ATTN
    lax.fori_loop(0, nq, body, 0, unroll=min(nq, HEADS_PER_GROUP // heads))


ATTN_MAX_WIDE_LEN = 2048


def _attn_group(q, k, v, dilation, interpret):
    b, _, sub_len, _ = q.shape
    assert sub_len % QB_ATTN == 0
    heads = HEADS_PER_GROUP if sub_len <= ATTN_MAX_WIDE_LEN else 1
    width = heads * HEAD_DIM
    spec = pl.BlockSpec((None, None, sub_len, width), lambda bi, r, hb: (bi, r, 0, hb))
    return pl.pallas_call(
        functools.partial(_attn_kernel, sub_len=sub_len, heads=heads),
        out_shape=(jax.ShapeDtypeStruct(q.shape, BF16),
                   jax.ShapeDtypeStruct((b, dilation, sub_len, LANES), F32)),
        grid=(b, dilation, HEADS_PER_GROUP // heads),
        in_specs=[spec, spec, spec],
        out_specs=(spec, pl.BlockSpec((None, None, sub_len, LANES),
                                      lambda bi, r, hb: (bi, r, 0, 0))),
        compiler_params=_cparams(("arbitrary", "arbitrary", "arbitrary")),
        name=f"attn_d{dilation}",
        interpret=interpret,
    )(q, k, v)


def _dft_kernel(cs_ref, ss_ref, a_ref, b_ref, y_ref, acc_ref):
    kk = pl.program_id(2)

    @pl.when(kk == 0)
    def _():
        acc_ref[...] = jnp.zeros_like(acc_ref)

    acc_ref[...] += (jnp.dot(cs_ref[...], a_ref[...], preferred_element_type=F32)
                     - jnp.dot(ss_ref[...], b_ref[...], preferred_element_type=F32))

    @pl.when(kk == pl.num_programs(2) - 1)
    def _():
        y_ref[...] = acc_ref[...].astype(BF16)


def _seq_dft_mats(s):
    lo = LANES
    hi = s // lo
    rows = jnp.arange(s, dtype=jnp.int32)[:, None]
    scale = 2.0 * math.pi / s
    ang_hi = ((rows * (jnp.arange(hi, dtype=jnp.int32)[None, :] * lo)) % s).astype(F32) * scale
    ang_lo = ((rows * jnp.arange(lo, dtype=jnp.int32)[None, :]) % s).astype(F32) * scale
    ch, sh = jnp.cos(ang_hi)[:, :, None], jnp.sin(ang_hi)[:, :, None]
    cl, sl = jnp.cos(ang_lo)[:, None, :], jnp.sin(ang_lo)[:, None, :]
    norm = s ** -0.5
    cs = ((ch * cl - sh * sl) * norm).astype(BF16).reshape(s, s)
    ss = ((sh * cl + ch * sl) * norm).astype(BF16).reshape(s, s)
    return cs, ss


def _seq_dft(fa, fb, b, s, interpret):
    ts = min(DFT_TS, s)
    tk = min(DFT_TK, s)
    cs, ss = _seq_dft_mats(s)
    mat = pl.BlockSpec((ts, tk), lambda bi, i, kk: (i, kk))
    vec = pl.BlockSpec((None, tk, FOURIER_WIDTH), lambda bi, i, kk: (bi, kk, 0))
    y = pl.pallas_call(
        _dft_kernel,
        out_shape=jax.ShapeDtypeStruct((b, s, FOURIER_WIDTH), BF16),
        grid=(b, s // ts, s // tk),
        in_specs=[mat, mat, vec, vec],
        out_specs=pl.BlockSpec((None, ts, FOURIER_WIDTH), lambda bi, i, kk: (bi, i, 0)),
        scratch_shapes=[pltpu.VMEM((ts, FOURIER_WIDTH), F32)],
        compiler_params=_cparams(("arbitrary", "arbitrary", "arbitrary")),
        name="seq_dft",
        interpret=interpret,
    )(cs, ss, fa.reshape(b, s, FOURIER_WIDTH), fb.reshape(b, s, FOURIER_WIDTH))
    return y.reshape(b * s, FOURIER_WIDTH)


def _mix_kernel(x_ref, o0_ref, o1_ref, o2_ref, l0_ref, l1_ref, l2_ref, y_ref, sga_ref, sgb_ref,
                wat_ref, wfo_ref, wout_ref, gf_ref, wqy_ref, sk_ref, hx_ref,
                h_ref, hnt_ref, st_ref, ostage_ref, lstage_ref):
    tm = x_ref.shape[0]

    def token_order(ref, stage, gi):
        dilation = ATTN_GROUPS[gi][1]
        if dilation == 1:
            return ref[0].astype(F32)
        cols = []
        for c in range(ref.shape[-1] // LANES):
            for r in range(dilation):
                stage[gi - 1, c, pl.ds(r, tm // dilation, stride=dilation), :] = (
                    ref[r, :, c * LANES:(c + 1) * LANES].astype(F32))
            cols.append(stage[gi - 1, c])
        return cols[0] if len(cols) == 1 else jnp.concatenate(cols, axis=1)

    lds = [token_order(ref, lstage_ref, gi) for gi, ref in enumerate((l0_ref, l1_ref, l2_ref))]
    mx = jnp.maximum(jnp.maximum(lds[0], lds[1]), lds[2])
    ws = [jnp.exp(ld - mx) for ld in lds]
    inv = 1.0 / (ws[0] + ws[1] + ws[2])
    attn = jnp.zeros((tm, ATTN_GROUP_WIDTH), F32)
    for gi, ref in enumerate((o0_ref, o1_ref, o2_ref)):
        wexp = jnp.dot((ws[gi] * inv).astype(BF16), hx_ref[...], preferred_element_type=F32)
        attn = attn + wexp * token_order(ref, ostage_ref, gi)
    attn = jnp.dot(attn.astype(BF16), wat_ref[...], preferred_element_type=F32)
    four = jnp.dot(y_ref[...], wfo_ref[...], preferred_element_type=F32)
    mix = sga_ref[...].astype(F32) * four + sgb_ref[...].astype(F32) * attn
    h = x_ref[...] + jnp.dot(mix.astype(BF16), wout_ref[...], preferred_element_type=F32)
    h_ref[...] = h
    ms = jnp.mean(h * h, axis=-1, keepdims=True)
    hn32 = h * lax.rsqrt(ms + NORM_EPS) * gf_ref[...]
    hn = hn32.astype(BF16)
    hnt_ref[...] = hn32.T.astype(BF16)
    qy = jnp.dot(hn, wqy_ref[...], preferred_element_type=F32).astype(BF16)
    half = PEER_KEY_DIM // 2
    for hc in range(2 * PEER_HEADS):
        st_ref[hc] = lax.dot_general(sk_ref[hc], qy[:, hc * half:(hc + 1) * half],
                                     (((1,), (1,)), ((), ())), preferred_element_type=F32)


def _mix(x2, o_list, ld_list, yf, sga, sgb, w_attn, w_fourier, w_out, g_ffn, w_query, sub_keys,
         interpret):
    t = x2.shape[0]
    tm = TM_MIX
    spt = o_list[0].shape[2] // tm
    tok = lambda w: pl.BlockSpec((tm, w), lambda i: (i, 0))
    grp = lambda w: [pl.BlockSpec((None, d, tm // d, w), lambda i: (i // spt, 0, i % spt, 0))
                     for _, d in ATTN_GROUPS]
    head_expand = (jnp.arange(LANES)[:, None]
                   == LD_LANES * (jnp.arange(ATTN_GROUP_WIDTH)[None, :] // HEAD_DIM)).astype(BF16)
    n_staged = sum(1 for _, d in ATTN_GROUPS if d > 1)
    sk = jnp.stack([sub_keys[:, 0][:, jnp.array(_PAIR_ORDER)], sub_keys[:, 1]], axis=1)
    sk = sk.reshape(2 * PEER_HEADS, N_KEYS, PEER_KEY_DIM // 2).astype(BF16)
    wat, wfo, wout, wqy = (w_attn.astype(BF16), w_fourier.astype(BF16), w_out.astype(BF16),
                           w_query.astype(BF16))
    return pl.pallas_call(
        _mix_kernel,
        out_shape=(jax.ShapeDtypeStruct((t, D_MODEL), F32),
                   jax.ShapeDtypeStruct((D_MODEL, t), BF16),
                   jax.ShapeDtypeStruct((2 * PEER_HEADS, N_KEYS, t), F32)),
        grid=(t // tm,),
        in_specs=[tok(D_MODEL)] + grp(ATTN_GROUP_WIDTH) + grp(LANES) + [tok(FOURIER_WIDTH),
                  tok(D_MODEL), tok(D_MODEL),
                  _const_spec(wat.shape), _const_spec(wfo.shape), _const_spec(wout.shape),
                  _const_spec((1, D_MODEL)), _const_spec(wqy.shape), _const_spec(sk.shape),
                  _const_spec(head_expand.shape)],
        out_specs=(tok(D_MODEL), pl.BlockSpec((D_MODEL, tm), lambda i: (0, i)),
                   pl.BlockSpec((2 * PEER_HEADS, N_KEYS, tm), lambda i: (0, 0, i))),
        scratch_shapes=[pltpu.VMEM((n_staged, HEADS_PER_GROUP, tm, HEAD_DIM), F32),
                        pltpu.VMEM((n_staged, 1, tm, LANES), F32)],
        compiler_params=_cparams(("arbitrary",)),
        name="mix",
        interpret=interpret,
    )(x2, *o_list, *ld_list, yf, sga, sgb, wat, wfo, wout, g_ffn.reshape(1, D_MODEL), wqy, sk,
      head_expand)


N_PAIRS = N_KEYS // 2
_PAIR_ORDER = tuple(range(0, N_KEYS, 2)) + tuple(range(1, N_KEYS, 2))
SUBLANES = 8


def _batcher_network(n):
    def merge(lo, hi, r):
        step = r * 2
        if step < hi - lo:
            yield from merge(lo, hi, step)
            yield from merge(lo + r, hi, step)
            for i in range(lo + r, hi - r, step):
                yield (i, i + r)
        else:
            yield (lo, lo + r)

    def sort(lo, hi):
        if hi - lo >= 1:
            mid = lo + (hi - lo) // 2
            yield from sort(lo, mid)
            yield from sort(mid + 1, hi)
            yield from merge(lo, hi, 1)

    return tuple(sort(0, n - 1))


_SORT16 = _batcher_network(PEER_TOPK)


def _compare_exchange(v, i, j):
    v[i], v[j] = jnp.maximum(v[i], v[j]), jnp.minimum(v[i], v[j])


def _top16_sorted(keys):
    n = PEER_TOPK
    v = [keys[SUBLANES * k:SUBLANES * (k + 1)] for k in range(n)]
    for i, j in _SORT16:
        _compare_exchange(v, i, j)
    shift = SUBLANES // 2
    while shift >= 1:
        other = [pltpu.roll(x, shift, 0) for x in v]
        v = [jnp.maximum(v[k], other[n - 1 - k]) for k in range(n)]
        d = n // 2
        while d >= 1:
            for i in range(n):
                if not i & d:
                    _compare_exchange(v, i, i + d)
            d //= 2
        shift //= 2
    return [x[0:1] for x in v]


_CAND_PAIRS = tuple((i, j) for i in range(PEER_TOPK) for j in range(PEER_TOPK)
                    if (i + 1) * (j + 1) <= PEER_TOPK)
_CAND_ROWS = -(-len(_CAND_PAIRS) // SUBLANES) * SUBLANES
TIE_STEP = 2.0 ** -17


def _kth_slot_value(vals, k):
    slot = lax.broadcasted_iota(jnp.int32, vals.shape, 0).astype(F32)
    bumped = vals * (1.0 + slot * TIE_STEP)
    cur = bumped
    m = None
    for _ in range(k):
        m = jnp.max(cur, axis=0, keepdims=True)
        cur = jnp.where(cur == m, -1.0, cur)
    return bumped, m


def _candidates(fa, fb):
    rows = [fa[i] * fb[j] for i, j in _CAND_PAIRS]
    rows.append(jnp.zeros((_CAND_ROWS - len(rows),) + rows[0].shape[1:], rows[0].dtype))
    return jnp.concatenate(rows, axis=0)


def _topk_kernel(st_ref, e1_ref, e2_ref, th_ref):
    tm = st_ref.shape[2]
    for h in range(PEER_HEADS):
        for tc in range(tm // LANES):
            lanes = slice(tc * LANES, (tc + 1) * LANES)
            s1 = st_ref[2 * h, :, lanes]
            s2 = st_ref[2 * h + 1, :, lanes]
            a = _top16_sorted(s1)
            b = _top16_sorted(s2)
            ea = [jnp.exp(ai - a[0]) for ai in a]
            eb = [jnp.exp(bi - b[0]) for bi in b]
            cand = _candidates(ea, eb)
            bumped, kth = _kth_slot_value(cand, PEER_TOPK)
            z = jnp.sum(jnp.where(bumped >= kth, cand, 0.0), axis=0, keepdims=True)
            inv_z = 1.0 / z
            cand_b = _candidates([(x * inv_z).astype(BF16) for x in ea],
                                 [x.astype(BF16) for x in eb]).astype(F32)
            _, kth_b = _kth_slot_value(cand_b, PEER_TOPK)
            th_ref[h:h + 1, lanes] = kth_b.astype(BF16).astype(F32)
            e1 = jnp.exp(s1 - a[0]) * inv_z
            e1_ref[:, 2 * h, lanes] = e1[:N_PAIRS]
            e1_ref[:, 2 * h + 1, lanes] = e1[N_PAIRS:]
            e2_ref[h, :, lanes] = jnp.exp(s2 - b[0]).astype(BF16)


def _topk(st, interpret):
    t = st.shape[2]
    tm = TM_TOPK
    blk = lambda n: pl.BlockSpec((n, N_KEYS, tm), lambda i: (0, 0, i))
    return pl.pallas_call(
        _topk_kernel,
        out_shape=(jax.ShapeDtypeStruct((N_PAIRS, 2 * PEER_HEADS, t), F32),
                   jax.ShapeDtypeStruct((PEER_HEADS, N_KEYS, t), BF16),
                   jax.ShapeDtypeStruct((PEER_HEADS, t), F32)),
        grid=(t // tm,),
        in_specs=[blk(2 * PEER_HEADS)],
        out_specs=(pl.BlockSpec((N_PAIRS, 2 * PEER_HEADS, tm), lambda i: (0, 0, i)),
                   blk(PEER_HEADS),
                   pl.BlockSpec((PEER_HEADS, tm), lambda i: (0, i))),
        compiler_params=_cparams(("arbitrary",)),
        name="peer_topk",
        interpret=interpret,
    )(st)


SB_PEER = 2 * N_KEYS
KB_PEER = 32


def _peer_kernel(hnt_ref, u_ref, vt_ref, e1_ref, e2_ref, th_ref, h_ref, y_ref,
                 at0_ref, at1_ref, ht0_ref, ht1_ref, acc_ref, join_ref, *, nj):
    s = pl.program_id(0)
    tm = hnt_ref.shape[1]
    j_out = lax.rem(jnp.maximum(s - 2, 0), nj)

    @pl.when(s == 0)
    def _():
        for ref in (at0_ref, at1_ref, ht0_ref, ht1_ref):
            ref[...] = jnp.zeros_like(ref)

    @pl.when(j_out == 0)
    def _():
        acc_ref[...] = jnp.zeros_like(acc_ref)

    def stages(at_w, at_r, ht_w, ht_r):
        def slice_body(q, carry):
            r0 = pl.multiple_of(q * SB_PEER, SB_PEER)
            at_new = jnp.dot(u_ref[pl.ds(r0, SB_PEER), :], hnt_ref[...],
                             preferred_element_type=F32)
            at_w[pl.ds(r0, SB_PEER), :] = at_new
            join = at_new[SB_PEER - 8:, tm - LANES:]
            for par in range(2):
                for tc in range(tm // LANES):
                    lanes = slice(tc * LANES, (tc + 1) * LANES)
                    e1b = [jnp.broadcast_to(
                        e1_ref[q, 2 * h + par:2 * h + par + 1, lanes].astype(BF16),
                        (KB_PEER, LANES)) for h in range(PEER_HEADS)]
                    thb = [jnp.broadcast_to(th_ref[h:h + 1, lanes].astype(BF16),
                                            (KB_PEER, LANES)) for h in range(PEER_HEADS)]
                    for kb in range(N_KEYS // KB_PEER):
                        keys = slice(kb * KB_PEER, (kb + 1) * KB_PEER)
                        ra = pl.multiple_of(r0 + par * N_KEYS + kb * KB_PEER, KB_PEER)
                        gate = jnp.zeros((KB_PEER, LANES), BF16)
                        for h in range(PEER_HEADS):
                            p = e1b[h] * e2_ref[h, keys, lanes]
                            gate = gate + jnp.where(p >= thb[h], p, jnp.zeros_like(p))
                        a = at_r[pl.ds(ra, KB_PEER), lanes].astype(BF16)
                        act = jax.nn.gelu(a) * gate
                        ht_w[pl.ds(ra, KB_PEER), lanes] = act
                        join = join + act[:8].astype(F32)
            out_new = jnp.dot(vt_ref[pl.ds(r0, SB_PEER), :], ht_r[...],
                              preferred_element_type=F32)
            acc_ref[pl.ds(r0, SB_PEER), :] += out_new
            return carry + join + out_new[SB_PEER - 8:, tm - LANES:]

        jv = lax.fori_loop(0, u_ref.shape[0] // SB_PEER, slice_body, jnp.zeros((8, LANES), F32))
        join_ref[...] = jv

    parity = lax.rem(s, 2)

    @pl.when(parity == 0)
    def _():
        stages(at0_ref, at1_ref, ht1_ref, ht0_ref)

    @pl.when(parity == 1)
    def _():
        stages(at1_ref, at0_ref, ht0_ref, ht1_ref)

    @pl.when((j_out == nj - 1) & (s >= 2))
    def _():
        y_ref[...] = h_ref[...] + acc_ref[...].T


def _expert_table_kernel(u_ref, v_ref, ub_ref, vt_ref):
    ub_ref[...] = u_ref[...].astype(BF16)
    vt_ref[...] = v_ref[...].T.astype(BF16)


def _expert_tables(expert_u, expert_v, eb, interpret):
    nj = N_EXPERTS // eb
    blk = pl.BlockSpec((eb, D_MODEL), lambda j: (j, 0))
    return pl.pallas_call(
        _expert_table_kernel,
        out_shape=(jax.ShapeDtypeStruct((N_EXPERTS, D_MODEL), BF16),
                   jax.ShapeDtypeStruct((nj, D_MODEL, eb), BF16)),
        grid=(nj,),
        in_specs=[blk, blk],
        out_specs=(blk, pl.BlockSpec((None, D_MODEL, eb), lambda j: (j, 0, 0))),
        compiler_params=_cparams(("arbitrary",)),
        name="expert_tables",
        interpret=interpret,
    )(expert_u, expert_v)


def _peer(hnt, h, e1, e2, th, u, vt, interpret):
    t = hnt.shape[1]
    tm, eb = TM_PEER, EB_PEER
    assert eb == D_MODEL
    nj = N_EXPERTS // eb
    n_pairs = (t // tm) * nj
    tile = lambda s: s // nj
    blk = lambda s: lax.rem(s, nj)
    s1 = lambda s: jnp.minimum(s, n_pairs - 1)
    s2 = lambda s: jnp.clip(s - 1, 0, n_pairs - 1)
    s3 = lambda s: jnp.maximum(s - 2, 0)
    return pl.pallas_call(
        functools.partial(_peer_kernel, nj=nj),
        out_shape=jax.ShapeDtypeStruct((t, D_MODEL), F32),
        grid=(n_pairs + 2,),
        in_specs=[pl.BlockSpec((D_MODEL, tm), lambda s: (0, tile(s1(s)))),
                  pl.BlockSpec((eb, D_MODEL), lambda s: (blk(s1(s)), 0)),
                  pl.BlockSpec((None, D_MODEL, eb), lambda s: (blk(s3(s)), 0, 0)),
                  pl.BlockSpec((eb // SB_PEER, 2 * PEER_HEADS, tm),
                               lambda s: (blk(s2(s)), 0, tile(s2(s)))),
                  pl.BlockSpec((PEER_HEADS, N_KEYS, tm), lambda s: (0, 0, tile(s2(s)))),
                  pl.BlockSpec((PEER_HEADS, tm), lambda s: (0, tile(s2(s)))),
                  pl.BlockSpec((tm, D_MODEL), lambda s: (tile(s3(s)), 0))],
        out_specs=pl.BlockSpec((tm, D_MODEL), lambda s: (tile(s3(s)), 0)),
        scratch_shapes=[pltpu.VMEM((eb, tm), F32), pltpu.VMEM((eb, tm), F32),
                        pltpu.VMEM((eb, tm), BF16), pltpu.VMEM((eb, tm), BF16),
                        pltpu.VMEM((D_MODEL, tm), F32), pltpu.VMEM((8, LANES), F32)],
        compiler_params=_cparams(("arbitrary",)),
        name="peer_experts",
        interpret=interpret,
    )(hnt, u, vt, e1, e2, th, h)


def _hybrid_block(x, g_mix, w_in, q_gain, k_gain, w_fourier, w_attn, w_out, g_ffn, w_query,
                  sub_keys, expert_u_b, expert_vt_b, interpret=False):
    b, s, _ = x.shape
    x2 = x.reshape(b * s, D_MODEL)
    fa, fb, q, k, v, sga, sgb = _proj(x2, s, g_mix, w_in, q_gain, k_gain, interpret)
    o_list, ld_list = [], []
    for gi, (_, dilation) in enumerate(ATTN_GROUPS):
        o, ld = _attn_group(q[gi], k[gi], v[gi], dilation, interpret)
        o_list.append(o)
        ld_list.append(ld)
    yf = _seq_dft(fa, fb, b, s, interpret)
    h, hnt, st = _mix(x2, o_list, ld_list, yf, sga, sgb, w_attn, w_fourier, w_out, g_ffn,
                      w_query, sub_keys, interpret)
    e1, e2, th = _topk(st, interpret)
    y = _peer(hnt, h, e1, e2, th, expert_u_b, expert_vt_b, interpret)
    return y.reshape(b, s, D_MODEL)


def kernel(x_prompt, x_sample, g_mix, w_in, q_gain, k_gain, w_fourier, w_attn, w_out, g_ffn,
           w_query, sub_keys, expert_u, expert_v):
    y_prompt, y_sample = x_prompt, x_sample
    for layer in range(g_mix.shape[0]):
        params = (g_mix[layer], w_in[layer], q_gain[layer], k_gain[layer], w_fourier[layer],
                  w_attn[layer], w_out[layer], g_ffn[layer], w_query[layer], sub_keys[layer],
                  *_expert_tables(expert_u[layer], expert_v[layer], EB_PEER, False))
        y_prompt = _hybrid_block(y_prompt, *params)
        y_sample = _hybrid_block(y_sample, *params)
    return (y_prompt, y_sample)
```

```python
import functools
import math

import jax
import jax.numpy as jnp
from jax import lax
from jax.experimental import pallas as pl
from jax.experimental.pallas import tpu as pltpu

F32 = jnp.float32
BF16 = jnp.bfloat16

D_MODEL = 1024
FOURIER_GROUPS = 4
FOURIER_GROUP_DIM = 128
FOURIER_WIDTH = FOURIER_GROUPS * FOURIER_GROUP_DIM
ATTN_GROUPS = ((128, 1), (512, 4), (2048, 16))
N_ATTN_GROUPS = 3
HEADS_PER_GROUP = 4
HEAD_DIM = 128
ATTN_GROUP_WIDTH = HEADS_PER_GROUP * HEAD_DIM
ATTN_WIDTH = N_ATTN_GROUPS * ATTN_GROUP_WIDTH
ROPE_DIM = HEAD_DIM // 4
ROPE_THETA = 500000.0
N_KEYS = 128
N_EXPERTS = N_KEYS * N_KEYS
PEER_HEADS = 8
PEER_TOPK = 16
PEER_KEY_DIM = 256
NORM_EPS = 1e-6
NEG_INF = -1e30

LANES = 128
VMEM_LIMIT = 56 * 1024 * 1024

TM_PROJ = 512
TM_MIX = 512
TM_TOPK = 512
TM_PEER = 512
EB_PEER = 1024
QB_ATTN = 128
DFT_TS = 1024
DFT_TK = 2048


def _cparams(sem):
    return pltpu.CompilerParams(dimension_semantics=sem, vmem_limit_bytes=VMEM_LIMIT)


def _const_spec(shape):
    nd = len(shape)
    return pl.BlockSpec(shape, lambda *_: (0,) * nd)


def _proj_kernel(x_ref, g_ref, wf_ref, wq_ref, wk_ref, wv_ref, wga_ref, wgb_ref,
                 qg_ref, kg_ref, cc_ref, sc_ref, rc_ref, rs1_ref, rs2_ref,
                 fa_ref, fb_ref, q0_ref, q1_ref, q2_ref, k0_ref, k1_ref, k2_ref,
                 v0_ref, v1_ref, v2_ref, sga_ref, sgb_ref, stage_ref):
    tm = x_ref.shape[0]
    hpg = HEADS_PER_GROUP
    stage_slot = [0]

    def store_group(out_ref, gi, cols):
        dilation = ATTN_GROUPS[gi][1]
        if dilation == 1:
            for hl, c in enumerate(cols):
                out_ref[0, :, hl * HEAD_DIM:(hl + 1) * HEAD_DIM] = c.astype(BF16)
            return
        slot = stage_slot[0]
        stage_slot[0] += 1
        for hl, c in enumerate(cols):
            stage_ref[slot, hl] = c
            for r in range(dilation):
                out_ref[r, :, hl * HEAD_DIM:(hl + 1) * HEAD_DIM] = stage_ref[
                    slot, hl, pl.ds(r, tm // dilation, stride=dilation), :].astype(BF16)

    x = x_ref[...]
    ms = jnp.mean(x * x, axis=-1, keepdims=True)
    xn = (x * lax.rsqrt(ms + NORM_EPS) * g_ref[...]).astype(BF16)

    f = jnp.dot(xn, wf_ref[...], preferred_element_type=F32).astype(BF16)
    fa_ref[...] = jnp.dot(f, cc_ref[...], preferred_element_type=F32).astype(BF16)
    fb_ref[...] = jnp.dot(f, sc_ref[...], preferred_element_type=F32).astype(BF16)

    rc = rc_ref[...]
    rs1 = rs1_ref[...]
    rs2 = rs2_ref[...]

    def norm_rope(w_ref, gain_ref, out_refs, scale):
        y = jnp.dot(xn, w_ref[...], preferred_element_type=F32)
        for gi in range(N_ATTN_GROUPS):
            cols = []
            for hh in range(gi * hpg, (gi + 1) * hpg):
                yh = y[:, hh * HEAD_DIM:(hh + 1) * HEAD_DIM]
                hms = jnp.mean(yh * yh, axis=-1, keepdims=True)
                yn = yh * lax.rsqrt(hms + NORM_EPS) * gain_ref[gi:gi + 1, :]
                r = (yn * rc + pltpu.roll(yn, LANES - ROPE_DIM // 2, 1) * rs1
                     + pltpu.roll(yn, ROPE_DIM // 2, 1) * rs2)
                cols.append(r * scale)
            store_group(out_refs[gi], gi, cols)

    norm_rope(wq_ref, qg_ref, (q0_ref, q1_ref, q2_ref), HEAD_DIM ** -0.5)
    norm_rope(wk_ref, kg_ref, (k0_ref, k1_ref, k2_ref), 1.0)
    yv = jnp.dot(xn, wv_ref[...], preferred_element_type=F32)
    for gi, out_ref in enumerate((v0_ref, v1_ref, v2_ref)):
        store_group(out_ref, gi, [yv[:, hh * HEAD_DIM:(hh + 1) * HEAD_DIM]
                                  for hh in range(gi * hpg, (gi + 1) * hpg)])
    sga_ref[...] = jax.nn.sigmoid(
        jnp.dot(xn, wga_ref[...], preferred_element_type=F32)).astype(BF16)
    sgb_ref[...] = jax.nn.sigmoid(
        jnp.dot(xn, wgb_ref[...], preferred_element_type=F32)).astype(BF16)


def _rope_tables(s):
    half = ROPE_DIM // 2
    inv_freq = ROPE_THETA ** (-jnp.arange(half, dtype=F32) * 2.0 / ROPE_DIM)
    ang = jnp.arange(s, dtype=F32)[:, None] * inv_freq[None, :]
    cos, sin = jnp.cos(ang), jnp.sin(ang)
    zeros = jnp.zeros((s, HEAD_DIM - ROPE_DIM), F32)
    zh = jnp.zeros((s, half), F32)
    rc = jnp.concatenate([cos, cos, jnp.ones((s, HEAD_DIM - ROPE_DIM), F32)], axis=1)
    rs1 = jnp.concatenate([-sin, zh, zeros], axis=1)
    rs2 = jnp.concatenate([zh, sin, zeros], axis=1)
    return rc, rs1, rs2


def _channel_dft_mats():
    n = FOURIER_GROUP_DIM
    idx = (jnp.arange(n)[:, None] * jnp.arange(n)[None, :]) % n
    ang = idx.astype(F32) * (2.0 * math.pi / n)
    eye = jnp.eye(FOURIER_GROUPS, dtype=F32)
    cc = jnp.kron(eye, jnp.cos(ang) * n ** -0.5)
    sc = jnp.kron(eye, jnp.sin(ang) * n ** -0.5)
    return cc.astype(BF16), sc.astype(BF16)


def _proj(x2, s, g_mix, w_in, q_gain, k_gain, interpret):
    t = x2.shape[0]
    tm = TM_PROJ
    c0 = FOURIER_WIDTH
    c1 = c0 + ATTN_WIDTH
    c2 = c1 + ATTN_WIDTH
    c3 = c2 + ATTN_WIDTH
    c4 = c3 + D_MODEL
    wb = w_in.astype(BF16)
    wf, wq, wk, wv, wga, wgb = (wb[:, :c0], wb[:, c0:c1], wb[:, c1:c2], wb[:, c2:c3],
                                wb[:, c3:c4], wb[:, c4:])
    cc, sc = _channel_dft_mats()
    rc, rs1, rs2 = _rope_tables(s)
    spt = s // tm
    b = t // s
    tok = lambda w: pl.BlockSpec((tm, w), lambda i: (i, 0))
    pos = pl.BlockSpec((tm, HEAD_DIM), lambda i: (i % spt, 0))
    grp_shapes = [jax.ShapeDtypeStruct((b, d, s // d, ATTN_GROUP_WIDTH), BF16)
                  for _, d in ATTN_GROUPS]
    grp_specs = [pl.BlockSpec((None, d, tm // d, ATTN_GROUP_WIDTH),
                              lambda i: (i // spt, 0, i % spt, 0)) for _, d in ATTN_GROUPS]
    out_shape = ([jax.ShapeDtypeStruct((t, FOURIER_WIDTH), BF16)] * 2 + grp_shapes * 3
                 + [jax.ShapeDtypeStruct((t, D_MODEL), BF16)] * 2)
    n_staged = 3 * sum(1 for _, d in ATTN_GROUPS if d > 1)
    outs = pl.pallas_call(
        _proj_kernel,
        out_shape=out_shape,
        grid=(t // tm,),
        in_specs=[tok(D_MODEL), _const_spec((1, D_MODEL)),
                  _const_spec(wf.shape), _const_spec(wq.shape), _const_spec(wk.shape),
                  _const_spec(wv.shape), _const_spec(wga.shape), _const_spec(wgb.shape),
                  _const_spec(q_gain.shape), _const_spec(k_gain.shape),
                  _const_spec(cc.shape), _const_spec(sc.shape), pos, pos, pos],
        out_specs=[tok(FOURIER_WIDTH)] * 2 + grp_specs * 3 + [tok(D_MODEL)] * 2,
        scratch_shapes=[pltpu.VMEM((n_staged, HEADS_PER_GROUP, tm, HEAD_DIM), F32)],
        compiler_params=_cparams(("arbitrary",)),
        name="proj",
        interpret=interpret,
    )(x2, g_mix.reshape(1, D_MODEL), wf, wq, wk, wv, wga, wgb, q_gain, k_gain, cc, sc,
      rc, rs1, rs2)
    fa, fb = outs[0], outs[1]
    q, k, v = outs[2:5], outs[5:8], outs[8:11]
    return fa, fb, q, k, v, outs[11], outs[12]


LD_LANES = LANES // HEADS_PER_GROUP


def _attn_kernel(q_ref, k_ref, v_ref, o_ref, ld_ref, *, sub_len, heads):
    kb = min(2 * QB_ATTN, sub_len)
    span = QB_ATTN // 2
    first_head = pl.program_id(2) * heads

    if heads < HEADS_PER_GROUP:
        @pl.when(first_head == 0)
        def _():
            ld_ref[...] = jnp.zeros_like(ld_ref)

    def body(i, carry):
        start = pl.multiple_of(i * QB_ATTN, QB_ATTN)
        ks = pl.multiple_of(jnp.clip(start - span, 0, sub_len - kb), span)
        qpos = start + lax.broadcasted_iota(jnp.int32, (QB_ATTN, kb), 0)
        kpos = ks + lax.broadcasted_iota(jnp.int32, (QB_ATTN, kb), 1)
        in_band = jnp.abs(kpos - qpos) <= span
        lane_head = lax.broadcasted_iota(jnp.int32, (QB_ATTN, LANES), 1) // LD_LANES
        if heads < HEADS_PER_GROUP:
            ld_all = ld_ref[pl.ds(start, QB_ATTN), :]
        else:
            ld_all = jnp.zeros((QB_ATTN, LANES), F32)
        for hl in range(heads):
            cols = slice(hl * HEAD_DIM, (hl + 1) * HEAD_DIM)
            q = q_ref[pl.ds(start, QB_ATTN), cols]
            k = k_ref[pl.ds(ks, kb), cols]
            v = v_ref[pl.ds(ks, kb), cols]
            s = lax.dot_general(q, k, (((1,), (1,)), ((), ())), preferred_element_type=F32)
            s = jnp.where(in_band, s, NEG_INF)
            m = jnp.max(s, axis=1, keepdims=True)
            p = jnp.exp(s - m)
            l = jnp.sum(p, axis=1, keepdims=True)
            o = jnp.dot(p.astype(BF16), v, preferred_element_type=F32) / l
            o_ref[pl.ds(start, QB_ATTN), cols] = o.astype(BF16)
            ld_all = jnp.where(lane_head == first_head + hl, m + jnp.log(l), ld_all)
        ld_ref[pl.ds(start, QB_ATTN), :] = ld_all
        return carry

    nq = sub_len // QB_ATTN
    lax.fori_loop(0, nq, body, 0, unroll=min(nq, 2 * HEADS_PER_GROUP // heads))


ATTN_MAX_WIDE_LEN = 2048


def _attn_group(q, k, v, dilation, interpret):
    b, _, sub_len, _ = q.shape
    assert sub_len % QB_ATTN == 0
    heads = HEADS_PER_GROUP if sub_len <= ATTN_MAX_WIDE_LEN else 1
    width = heads * HEAD_DIM
    spec = pl.BlockSpec((None, None, sub_len, width), lambda bi, r, hb: (bi, r, 0, hb))
    return pl.pallas_call(
        functools.partial(_attn_kernel, sub_len=sub_len, heads=heads),
        out_shape=(jax.ShapeDtypeStruct(q.shape, BF16),
                   jax.ShapeDtypeStruct((b, dilation, sub_len, LANES), F32)),
        grid=(b, dilation, HEADS_PER_GROUP // heads),
        in_specs=[spec, spec, spec],
        out_specs=(spec, pl.BlockSpec((None, None, sub_len, LANES),
                                      lambda bi, r, hb: (bi, r, 0, 0))),
        compiler_params=_cparams(("arbitrary", "arbitrary", "arbitrary")),
        name=f"attn_d{dilation}",
        interpret=interpret,
    )(q, k, v)


def _dft_kernel(cs_ref, ss_ref, a_ref, b_ref, y_ref, acc_ref):
    kk = pl.program_id(2)

    @pl.when(kk == 0)
    def _():
        acc_ref[...] = jnp.zeros_like(acc_ref)

    acc_ref[...] += (jnp.dot(cs_ref[...], a_ref[...], preferred_element_type=F32)
                     - jnp.dot(ss_ref[...], b_ref[...], preferred_element_type=F32))

    @pl.when(kk == pl.num_programs(2) - 1)
    def _():
        y_ref[...] = acc_ref[...].astype(BF16)


def _seq_dft_mats(s):
    lo = LANES
    hi = s // lo
    rows = jnp.arange(s, dtype=jnp.int32)[:, None]
    scale = 2.0 * math.pi / s
    ang_hi = ((rows * (jnp.arange(hi, dtype=jnp.int32)[None, :] * lo)) % s).astype(F32) * scale
    ang_lo = ((rows * jnp.arange(lo, dtype=jnp.int32)[None, :]) % s).astype(F32) * scale
    ch, sh = jnp.cos(ang_hi)[:, :, None], jnp.sin(ang_hi)[:, :, None]
    cl, sl = jnp.cos(ang_lo)[:, None, :], jnp.sin(ang_lo)[:, None, :]
    norm = s ** -0.5
    cs = ((ch * cl - sh * sl) * norm).astype(BF16).reshape(s, s)
    ss = ((sh * cl + ch * sl) * norm).astype(BF16).reshape(s, s)
    return cs, ss


def _seq_dft(fa, fb, b, s, interpret):
    ts = min(DFT_TS, s)
    tk = min(DFT_TK, s)
    cs, ss = _seq_dft_mats(s)
    mat = pl.BlockSpec((ts, tk), lambda bi, i, kk: (i, kk))
    vec = pl.BlockSpec((None, tk, FOURIER_WIDTH), lambda bi, i, kk: (bi, kk, 0))
    y = pl.pallas_call(
        _dft_kernel,
        out_shape=jax.ShapeDtypeStruct((b, s, FOURIER_WIDTH), BF16),
        grid=(b, s // ts, s // tk),
        in_specs=[mat, mat, vec, vec],
        out_specs=pl.BlockSpec((None, ts, FOURIER_WIDTH), lambda bi, i, kk: (bi, i, 0)),
        scratch_shapes=[pltpu.VMEM((ts, FOURIER_WIDTH), F32)],
        compiler_params=_cparams(("arbitrary", "arbitrary", "arbitrary")),
        name="seq_dft",
        interpret=interpret,
    )(cs, ss, fa.reshape(b, s, FOURIER_WIDTH), fb.reshape(b, s, FOURIER_WIDTH))
    return y.reshape(b * s, FOURIER_WIDTH)


def _mix_kernel(x_ref, o0_ref, o1_ref, o2_ref, l0_ref, l1_ref, l2_ref, y_ref, sga_ref, sgb_ref,
                wat_ref, wfo_ref, wout_ref, gf_ref, wqy_ref, sk_ref, hx_ref,
                h_ref, hnt_ref, st_ref, ostage_ref, lstage_ref):
    tm = x_ref.shape[0]

    def token_order(ref, stage, gi):
        dilation = ATTN_GROUPS[gi][1]
        if dilation == 1:
            return ref[0].astype(F32)
        cols = []
        for c in range(ref.shape[-1] // LANES):
            for r in range(dilation):
                stage[gi - 1, c, pl.ds(r, tm // dilation, stride=dilation), :] = (
                    ref[r, :, c * LANES:(c + 1) * LANES].astype(F32))
            cols.append(stage[gi - 1, c])
        return cols[0] if len(cols) == 1 else jnp.concatenate(cols, axis=1)

    lds = [token_order(ref, lstage_ref, gi) for gi, ref in enumerate((l0_ref, l1_ref, l2_ref))]
    mx = jnp.maximum(jnp.maximum(lds[0], lds[1]), lds[2])
    ws = [jnp.exp(ld - mx) for ld in lds]
    inv = 1.0 / (ws[0] + ws[1] + ws[2])
    attn = jnp.zeros((tm, ATTN_GROUP_WIDTH), F32)
    for gi, ref in enumerate((o0_ref, o1_ref, o2_ref)):
        wexp = jnp.dot((ws[gi] * inv).astype(BF16), hx_ref[...], preferred_element_type=F32)
        attn = attn + wexp * token_order(ref, ostage_ref, gi)
    attn = jnp.dot(attn.astype(BF16), wat_ref[...], preferred_element_type=F32)
    four = jnp.dot(y_ref[...], wfo_ref[...], preferred_element_type=F32)
    mix = sga_ref[...].astype(F32) * four + sgb_ref[...].astype(F32) * attn
    h = x_ref[...] + jnp.dot(mix.astype(BF16), wout_ref[...], preferred_element_type=F32)
    h_ref[...] = h
    ms = jnp.mean(h * h, axis=-1, keepdims=True)
    hn32 = h * lax.rsqrt(ms + NORM_EPS) * gf_ref[...]
    hn = hn32.astype(BF16)
    hnt_ref[...] = hn32.T.astype(BF16)
    qy = jnp.dot(hn, wqy_ref[...], preferred_element_type=F32).astype(BF16)
    half = PEER_KEY_DIM // 2
    for hc in range(2 * PEER_HEADS):
        st_ref[hc] = lax.dot_general(sk_ref[hc], qy[:, hc * half:(hc + 1) * half],
                                     (((1,), (1,)), ((), ())), preferred_element_type=F32)


def _mix(x2, o_list, ld_list, yf, sga, sgb, w_attn, w_fourier, w_out, g_ffn, w_query, sub_keys,
         interpret):
    t = x2.shape[0]
    tm = TM_MIX
    spt = o_list[0].shape[2] // tm
    tok = lambda w: pl.BlockSpec((tm, w), lambda i: (i, 0))
    grp = lambda w: [pl.BlockSpec((None, d, tm // d, w), lambda i: (i // spt, 0, i % spt, 0))
                     for _, d in ATTN_GROUPS]
    head_expand = (jnp.arange(LANES)[:, None]
                   == LD_LANES * (jnp.arange(ATTN_GROUP_WIDTH)[None, :] // HEAD_DIM)).astype(BF16)
    n_staged = sum(1 for _, d in ATTN_GROUPS if d > 1)
    sk = jnp.stack([sub_keys[:, 0][:, jnp.array(_PAIR_ORDER)], sub_keys[:, 1]], axis=1)
    sk = sk.reshape(2 * PEER_HEADS, N_KEYS, PEER_KEY_DIM // 2).astype(BF16)
    wat, wfo, wout, wqy = (w_attn.astype(BF16), w_fourier.astype(BF16), w_out.astype(BF16),
                           w_query.astype(BF16))
    return pl.pallas_call(
        _mix_kernel,
        out_shape=(jax.ShapeDtypeStruct((t, D_MODEL), F32),
                   jax.ShapeDtypeStruct((D_MODEL, t), BF16),
                   jax.ShapeDtypeStruct((2 * PEER_HEADS, N_KEYS, t), F32)),
        grid=(t // tm,),
        in_specs=[tok(D_MODEL)] + grp(ATTN_GROUP_WIDTH) + grp(LANES) + [tok(FOURIER_WIDTH),
                  tok(D_MODEL), tok(D_MODEL),
                  _const_spec(wat.shape), _const_spec(wfo.shape), _const_spec(wout.shape),
                  _const_spec((1, D_MODEL)), _const_spec(wqy.shape), _const_spec(sk.shape),
                  _const_spec(head_expand.shape)],
        out_specs=(tok(D_MODEL), pl.BlockSpec((D_MODEL, tm), lambda i: (0, i)),
                   pl.BlockSpec((2 * PEER_HEADS, N_KEYS, tm), lambda i: (0, 0, i))),
        scratch_shapes=[pltpu.VMEM((n_staged, HEADS_PER_GROUP, tm, HEAD_DIM), F32),
                        pltpu.VMEM((n_staged, 1, tm, LANES), F32)],
        compiler_params=_cparams(("arbitrary",)),
        name="mix",
        interpret=interpret,
    )(x2, *o_list, *ld_list, yf, sga, sgb, wat, wfo, wout, g_ffn.reshape(1, D_MODEL), wqy, sk,
      head_expand)


N_PAIRS = N_KEYS // 2
_PAIR_ORDER = tuple(range(0, N_KEYS, 2)) + tuple(range(1, N_KEYS, 2))
SUBLANES = 8


def _batcher_network(n):
    def merge(lo, hi, r):
        step = r * 2
        if step < hi - lo:
            yield from merge(lo, hi, step)
            yield from merge(lo + r, hi, step)
            for i in range(lo + r, hi - r, step):
                yield (i, i + r)
        else:
            yield (lo, lo + r)

    def sort(lo, hi):
        if hi - lo >= 1:
            mid = lo + (hi - lo) // 2
            yield from sort(lo, mid)
            yield from sort(mid + 1, hi)
            yield from merge(lo, hi, 1)

    return tuple(sort(0, n - 1))


_SORT16 = _batcher_network(PEER_TOPK)


def _compare_exchange(v, i, j):
    v[i], v[j] = jnp.maximum(v[i], v[j]), jnp.minimum(v[i], v[j])


def _top16_sorted(keys):
    n = PEER_TOPK
    v = [keys[SUBLANES * k:SUBLANES * (k + 1)] for k in range(n)]
    for i, j in _SORT16:
        _compare_exchange(v, i, j)
    shift = SUBLANES // 2
    while shift >= 1:
        other = [pltpu.roll(x, shift, 0) for x in v]
        v = [jnp.maximum(v[k], other[n - 1 - k]) for k in range(n)]
        d = n // 2
        while d >= 1:
            for i in range(n):
                if not i & d:
                    _compare_exchange(v, i, i + d)
            d //= 2
        shift //= 2
    return [x[0:1] for x in v]


_CAND_PAIRS = tuple((i, j) for i in range(PEER_TOPK) for j in range(PEER_TOPK)
                    if (i + 1) * (j + 1) <= PEER_TOPK)
_CAND_ROWS = -(-len(_CAND_PAIRS) // SUBLANES) * SUBLANES
TIE_STEP = 2.0 ** -17


def _kth_slot_value(vals, k):
    slot = lax.broadcasted_iota(jnp.int32, vals.shape, 0).astype(F32)
    bumped = vals * (1.0 + slot * TIE_STEP)
    cur = bumped
    m = None
    for _ in range(k):
        m = jnp.max(cur, axis=0, keepdims=True)
        cur = jnp.where(cur == m, -1.0, cur)
    return bumped, m


def _candidates(fa, fb):
    rows = [fa[i] * fb[j] for i, j in _CAND_PAIRS]
    rows.append(jnp.zeros((_CAND_ROWS - len(rows),) + rows[0].shape[1:], rows[0].dtype))
    return jnp.concatenate(rows, axis=0)


def _topk_kernel(st_ref, e1_ref, e2_ref, th_ref):
    tm = st_ref.shape[2]
    for h in range(PEER_HEADS):
        for tc in range(tm // LANES):
            lanes = slice(tc * LANES, (tc + 1) * LANES)
            s1 = st_ref[2 * h, :, lanes]
            s2 = st_ref[2 * h + 1, :, lanes]
            a = _top16_sorted(s1)
            b = _top16_sorted(s2)
            ea = [jnp.exp(ai - a[0]) for ai in a]
            eb = [jnp.exp(bi - b[0]) for bi in b]
            cand = _candidates(ea, eb)
            bumped, kth = _kth_slot_value(cand, PEER_TOPK)
            z = jnp.sum(jnp.where(bumped >= kth, cand, 0.0), axis=0, keepdims=True)
            inv_z = 1.0 / z
            cand_b = _candidates([(x * inv_z).astype(BF16) for x in ea],
                                 [x.astype(BF16) for x in eb]).astype(F32)
            _, kth_b = _kth_slot_value(cand_b, PEER_TOPK)
            th_ref[h:h + 1, lanes] = kth_b.astype(BF16).astype(F32)
            e1 = jnp.exp(s1 - a[0]) * inv_z
            e1_ref[:, 2 * h, lanes] = e1[:N_PAIRS]
            e1_ref[:, 2 * h + 1, lanes] = e1[N_PAIRS:]
            e2_ref[h, :, lanes] = jnp.exp(s2 - b[0]).astype(BF16)


def _topk(st, interpret):
    t = st.shape[2]
    tm = TM_TOPK
    blk = lambda n: pl.BlockSpec((n, N_KEYS, tm), lambda i: (0, 0, i))
    return pl.pallas_call(
        _topk_kernel,
        out_shape=(jax.ShapeDtypeStruct((N_PAIRS, 2 * PEER_HEADS, t), F32),
                   jax.ShapeDtypeStruct((PEER_HEADS, N_KEYS, t), BF16),
                   jax.ShapeDtypeStruct((PEER_HEADS, t), F32)),
        grid=(t // tm,),
        in_specs=[blk(2 * PEER_HEADS)],
        out_specs=(pl.BlockSpec((N_PAIRS, 2 * PEER_HEADS, tm), lambda i: (0, 0, i)),
                   blk(PEER_HEADS),
                   pl.BlockSpec((PEER_HEADS, tm), lambda i: (0, i))),
        compiler_params=_cparams(("arbitrary",)),
        name="peer_topk",
        interpret=interpret,
    )(st)


SB_PEER = 2 * N_KEYS
KB_PEER = 32


def _peer_kernel(hnt_ref, u_ref, vt_ref, e1_ref, e2_ref, th_ref, h_ref, y_ref,
                 at0_ref, at1_ref, ht0_ref, ht1_ref, acc_ref, join_ref, *, nj, n_pairs):
    s = pl.program_id(0)
    tm = hnt_ref.shape[1]
    j_out = lax.rem(jnp.maximum(s - 2, 0), nj)
    pair0 = lax.rem(jnp.clip(s - 1, 0, n_pairs - 1), nj) * (u_ref.shape[0] // SB_PEER)

    @pl.when(s == 0)
    def _():
        for ref in (at0_ref, at1_ref, ht0_ref, ht1_ref):
            ref[...] = jnp.zeros_like(ref)

    @pl.when(j_out == 0)
    def _():
        acc_ref[...] = jnp.zeros_like(acc_ref)

    def stages(at_w, at_r, ht_w, ht_r):
        def slice_body(q, carry):
            r0 = pl.multiple_of(q * SB_PEER, SB_PEER)
            at_new = jnp.dot(u_ref[pl.ds(r0, SB_PEER), :], hnt_ref[...],
                             preferred_element_type=F32)
            at_w[pl.ds(r0, SB_PEER), :] = at_new
            join = at_new[SB_PEER - 8:, tm - LANES:]
            for par in range(2):
                for tc in range(tm // LANES):
                    lanes = slice(tc * LANES, (tc + 1) * LANES)
                    e1b = [jnp.broadcast_to(
                        e1_ref[pair0 + q, 2 * h + par:2 * h + par + 1, lanes].astype(BF16),
                        (KB_PEER, LANES)) for h in range(PEER_HEADS)]
                    thb = [jnp.broadcast_to(th_ref[h:h + 1, lanes].astype(BF16),
                                            (KB_PEER, LANES)) for h in range(PEER_HEADS)]
                    for kb in range(N_KEYS // KB_PEER):
                        keys = slice(kb * KB_PEER, (kb + 1) * KB_PEER)
                        ra = pl.multiple_of(r0 + par * N_KEYS + kb * KB_PEER, KB_PEER)
                        gate = jnp.zeros((KB_PEER, LANES), BF16)
                        for h in range(PEER_HEADS):
                            p = e1b[h] * e2_ref[h, keys, lanes]
                            gate = gate + jnp.where(p >= thb[h], p, jnp.zeros_like(p))
                        a = at_r[pl.ds(ra, KB_PEER), lanes].astype(BF16)
                        act = jax.nn.gelu(a) * gate
                        ht_w[pl.ds(ra, KB_PEER), lanes] = act
                        join = join + act[:8].astype(F32)
            out_new = jnp.dot(vt_ref[pl.ds(r0, SB_PEER), :], ht_r[...],
                              preferred_element_type=F32)
            acc_ref[pl.ds(r0, SB_PEER), :] += out_new
            return carry + join + out_new[SB_PEER - 8:, tm - LANES:]

        jv = lax.fori_loop(0, u_ref.shape[0] // SB_PEER, slice_body, jnp.zeros((8, LANES), F32))
        join_ref[...] = jv

    parity = lax.rem(s, 2)

    @pl.when(parity == 0)
    def _():
        stages(at0_ref, at1_ref, ht1_ref, ht0_ref)

    @pl.when(parity == 1)
    def _():
        stages(at1_ref, at0_ref, ht0_ref, ht1_ref)

    @pl.when((j_out == nj - 1) & (s >= 2))
    def _():
        y_ref[...] = h_ref[...] + acc_ref[...].T


def _expert_table_kernel(u_ref, v_ref, ub_ref, vt_ref):
    ub_ref[...] = u_ref[...].astype(BF16)
    vt_ref[...] = v_ref[...].T.astype(BF16)


def _expert_tables(expert_u, expert_v, layer, eb, interpret):
    nj = N_EXPERTS // eb
    blk = pl.BlockSpec((eb, D_MODEL), lambda j: (j, 0))
    layer_blk = pl.BlockSpec((None, eb, D_MODEL), lambda j: (layer, j, 0))
    return pl.pallas_call(
        _expert_table_kernel,
        out_shape=(jax.ShapeDtypeStruct((N_EXPERTS, D_MODEL), BF16),
                   jax.ShapeDtypeStruct((nj, D_MODEL, eb), BF16)),
        grid=(nj,),
        in_specs=[layer_blk, layer_blk],
        out_specs=(blk, pl.BlockSpec((None, D_MODEL, eb), lambda j: (j, 0, 0))),
        compiler_params=_cparams(("arbitrary",)),
        name="expert_tables",
        interpret=interpret,
    )(expert_u, expert_v)


def _peer(hnt, h, e1, e2, th, u, vt, interpret):
    t = hnt.shape[1]
    tm, eb = TM_PEER, EB_PEER
    assert eb == D_MODEL
    nj = N_EXPERTS // eb
    n_pairs = (t // tm) * nj
    tile = lambda s: s // nj
    blk = lambda s: lax.rem(s, nj)
    s1 = lambda s: jnp.minimum(s, n_pairs - 1)
    s2 = lambda s: jnp.clip(s - 1, 0, n_pairs - 1)
    s3 = lambda s: jnp.maximum(s - 2, 0)
    return pl.pallas_call(
        functools.partial(_peer_kernel, nj=nj, n_pairs=n_pairs),
        out_shape=jax.ShapeDtypeStruct((t, D_MODEL), F32),
        grid=(n_pairs + 2,),
        in_specs=[pl.BlockSpec((D_MODEL, tm), lambda s: (0, tile(s1(s)))),
                  pl.BlockSpec((eb, D_MODEL), lambda s: (blk(s1(s)), 0)),
                  pl.BlockSpec((None, D_MODEL, eb), lambda s: (blk(s3(s)), 0, 0)),
                  pl.BlockSpec((N_PAIRS, 2 * PEER_HEADS, tm), lambda s: (0, 0, tile(s2(s)))),
                  pl.BlockSpec((PEER_HEADS, N_KEYS, tm), lambda s: (0, 0, tile(s2(s)))),
                  pl.BlockSpec((PEER_HEADS, tm), lambda s: (0, tile(s2(s)))),
                  pl.BlockSpec((tm, D_MODEL), lambda s: (tile(s3(s)), 0))],
        out_specs=pl.BlockSpec((tm, D_MODEL), lambda s: (tile(s3(s)), 0)),
        scratch_shapes=[pltpu.VMEM((eb, tm), F32), pltpu.VMEM((eb, tm), F32),
                        pltpu.VMEM((eb, tm), BF16), pltpu.VMEM((eb, tm), BF16),
                        pltpu.VMEM((D_MODEL, tm), F32), pltpu.VMEM((8, LANES), F32)],
        compiler_params=_cparams(("arbitrary",)),
        name="peer_experts",
        interpret=interpret,
    )(hnt, u, vt, e1, e2, th, h)


def _hybrid_block(x, g_mix, w_in, q_gain, k_gain, w_fourier, w_attn, w_out, g_ffn, w_query,
                  sub_keys, expert_u_b, expert_vt_b, interpret=False):
    b, s, _ = x.shape
    x2 = x.reshape(b * s, D_MODEL)
    fa, fb, q, k, v, sga, sgb = _proj(x2, s, g_mix, w_in, q_gain, k_gain, interpret)
    o_list, ld_list = [], []
    for gi, (_, dilation) in enumerate(ATTN_GROUPS):
        o, ld = _attn_group(q[gi], k[gi], v[gi], dilation, interpret)
        o_list.append(o)
        ld_list.append(ld)
    yf = _seq_dft(fa, fb, b, s, interpret)
    h, hnt, st = _mix(x2, o_list, ld_list, yf, sga, sgb, w_attn, w_fourier, w_out, g_ffn,
                      w_query, sub_keys, interpret)
    e1, e2, th = _topk(st, interpret)
    y = _peer(hnt, h, e1, e2, th, expert_u_b, expert_vt_b, interpret)
    return y.reshape(b, s, D_MODEL)


def kernel(x_prompt, x_sample, g_mix, w_in, q_gain, k_gain, w_fourier, w_attn, w_out, g_ffn,
           w_query, sub_keys, expert_u, expert_v):
    y_prompt, y_sample = x_prompt, x_sample
    for layer in range(g_mix.shape[0]):
        params = (g_mix[layer], w_in[layer], q_gain[layer], k_gain[layer], w_fourier[layer],
                  w_attn[layer], w_out[layer], g_ffn[layer], w_query[layer], sub_keys[layer],
                  *_expert_tables(expert_u, expert_v, layer, EB_PEER, False))
        y_prompt = _hybrid_block(y_prompt, *params)
        y_sample = _hybrid_block(y_sample, *params)
    return (y_prompt, y_sample)
```

```python
import functools
import math

import jax
import jax.numpy as jnp
from jax import lax
from jax.experimental import pallas as pl
from jax.experimental.pallas import tpu as pltpu

F32 = jnp.float32
BF16 = jnp.bfloat16

D_MODEL = 1024
FOURIER_GROUPS = 4
FOURIER_GROUP_DIM = 128
FOURIER_WIDTH = FOURIER_GROUPS * FOURIER_GROUP_DIM
ATTN_GROUPS = ((128, 1), (512, 4), (2048, 16))
N_ATTN_GROUPS = 3
HEADS_PER_GROUP = 4
HEAD_DIM = 128
ATTN_GROUP_WIDTH = HEADS_PER_GROUP * HEAD_DIM
ATTN_WIDTH = N_ATTN_GROUPS * ATTN_GROUP_WIDTH
ROPE_DIM = HEAD_DIM // 4
ROPE_THETA = 500000.0
N_KEYS = 128
N_EXPERTS = N_KEYS * N_KEYS
PEER_HEADS = 8
PEER_TOPK = 16
PEER_KEY_DIM = 256
NORM_EPS = 1e-6
NEG_INF = -1e30

LANES = 128
VMEM_LIMIT = 56 * 1024 * 1024

TM_PROJ = 512
TM_MIX = 512
TM_TOPK = 512
TM_PEER = 512
EB_PEER = 1024
QB_ATTN = 128
DFT_TS = 1024
DFT_TK = 2048


def _cparams(sem):
    return pltpu.CompilerParams(dimension_semantics=sem, vmem_limit_bytes=VMEM_LIMIT)


def _const_spec(shape):
    nd = len(shape)
    return pl.BlockSpec(shape, lambda *_: (0,) * nd)


def _proj_kernel(x_ref, g_ref, wf_ref, wq_ref, wk_ref, wv_ref, wga_ref, wgb_ref,
                 qg_ref, kg_ref, cc_ref, sc_ref, rc_ref, rs1_ref, rs2_ref,
                 fa_ref, fb_ref, q0_ref, q1_ref, q2_ref, k0_ref, k1_ref, k2_ref,
                 v0_ref, v1_ref, v2_ref, sga_ref, sgb_ref, stage_ref):
    tm = x_ref.shape[0]
    hpg = HEADS_PER_GROUP
    stage_slot = [0]

    def store_group(out_ref, gi, cols):
        dilation = ATTN_GROUPS[gi][1]
        if dilation == 1:
            for hl, c in enumerate(cols):
                out_ref[0, :, hl * HEAD_DIM:(hl + 1) * HEAD_DIM] = c.astype(BF16)
            return
        slot = stage_slot[0]
        stage_slot[0] += 1
        for hl, c in enumerate(cols):
            stage_ref[slot, hl] = c
            for r in range(dilation):
                out_ref[r, :, hl * HEAD_DIM:(hl + 1) * HEAD_DIM] = stage_ref[
                    slot, hl, pl.ds(r, tm // dilation, stride=dilation), :].astype(BF16)

    x = x_ref[...]
    ms = jnp.mean(x * x, axis=-1, keepdims=True)
    xn = (x * lax.rsqrt(ms + NORM_EPS) * g_ref[...]).astype(BF16)

    f = jnp.dot(xn, wf_ref[...], preferred_element_type=F32).astype(BF16)
    fa_ref[...] = jnp.dot(f, cc_ref[...], preferred_element_type=F32).astype(BF16)
    fb_ref[...] = jnp.dot(f, sc_ref[...], preferred_element_type=F32).astype(BF16)

    rc = rc_ref[...]
    rs1 = rs1_ref[...]
    rs2 = rs2_ref[...]

    def norm_rope(w_ref, gain_ref, out_refs, scale):
        y = jnp.dot(xn, w_ref[...], preferred_element_type=F32)
        for gi in range(N_ATTN_GROUPS):
            cols = []
            for hh in range(gi * hpg, (gi + 1) * hpg):
                yh = y[:, hh * HEAD_DIM:(hh + 1) * HEAD_DIM]
                hms = jnp.mean(yh * yh, axis=-1, keepdims=True)
                yn = yh * lax.rsqrt(hms + NORM_EPS) * gain_ref[gi:gi + 1, :]
                r = (yn * rc + pltpu.roll(yn, LANES - ROPE_DIM // 2, 1) * rs1
                     + pltpu.roll(yn, ROPE_DIM // 2, 1) * rs2)
                cols.append(r * scale)
            store_group(out_refs[gi], gi, cols)

    norm_rope(wq_ref, qg_ref, (q0_ref, q1_ref, q2_ref), HEAD_DIM ** -0.5)
    norm_rope(wk_ref, kg_ref, (k0_ref, k1_ref, k2_ref), 1.0)
    yv = jnp.dot(xn, wv_ref[...], preferred_element_type=F32)
    for gi, out_ref in enumerate((v0_ref, v1_ref, v2_ref)):
        store_group(out_ref, gi, [yv[:, hh * HEAD_DIM:(hh + 1) * HEAD_DIM]
                                  for hh in range(gi * hpg, (gi + 1) * hpg)])
    sga_ref[...] = jax.nn.sigmoid(
        jnp.dot(xn, wga_ref[...], preferred_element_type=F32)).astype(BF16)
    sgb_ref[...] = jax.nn.sigmoid(
        jnp.dot(xn, wgb_ref[...], preferred_element_type=F32)).astype(BF16)


def _rope_tables(s):
    half = ROPE_DIM // 2
    inv_freq = ROPE_THETA ** (-jnp.arange(half, dtype=F32) * 2.0 / ROPE_DIM)
    ang = jnp.arange(s, dtype=F32)[:, None] * inv_freq[None, :]
    cos, sin = jnp.cos(ang), jnp.sin(ang)
    zeros = jnp.zeros((s, HEAD_DIM - ROPE_DIM), F32)
    zh = jnp.zeros((s, half), F32)
    rc = jnp.concatenate([cos, cos, jnp.ones((s, HEAD_DIM - ROPE_DIM), F32)], axis=1)
    rs1 = jnp.concatenate([-sin, zh, zeros], axis=1)
    rs2 = jnp.concatenate([zh, sin, zeros], axis=1)
    return rc, rs1, rs2


def _channel_dft_mats():
    n = FOURIER_GROUP_DIM
    idx = (jnp.arange(n)[:, None] * jnp.arange(n)[None, :]) % n
    ang = idx.astype(F32) * (2.0 * math.pi / n)
    eye = jnp.eye(FOURIER_GROUPS, dtype=F32)
    cc = jnp.kron(eye, jnp.cos(ang) * n ** -0.5)
    sc = jnp.kron(eye, jnp.sin(ang) * n ** -0.5)
    return cc.astype(BF16), sc.astype(BF16)


def _proj(x2, s, g_mix, w_in, q_gain, k_gain, interpret):
    t = x2.shape[0]
    tm = TM_PROJ
    c0 = FOURIER_WIDTH
    c1 = c0 + ATTN_WIDTH
    c2 = c1 + ATTN_WIDTH
    c3 = c2 + ATTN_WIDTH
    c4 = c3 + D_MODEL
    wb = w_in.astype(BF16)
    wf, wq, wk, wv, wga, wgb = (wb[:, :c0], wb[:, c0:c1], wb[:, c1:c2], wb[:, c2:c3],
                                wb[:, c3:c4], wb[:, c4:])
    cc, sc = _channel_dft_mats()
    rc, rs1, rs2 = _rope_tables(s)
    spt = s // tm
    b = t // s
    tok = lambda w: pl.BlockSpec((tm, w), lambda i: (i, 0))
    pos = pl.BlockSpec((tm, HEAD_DIM), lambda i: (i % spt, 0))
    grp_shapes = [jax.ShapeDtypeStruct((b, d, s // d, ATTN_GROUP_WIDTH), BF16)
                  for _, d in ATTN_GROUPS]
    grp_specs = [pl.BlockSpec((None, d, tm // d, ATTN_GROUP_WIDTH),
                              lambda i: (i // spt, 0, i % spt, 0)) for _, d in ATTN_GROUPS]
    out_shape = ([jax.ShapeDtypeStruct((t, FOURIER_WIDTH), BF16)] * 2 + grp_shapes * 3
                 + [jax.ShapeDtypeStruct((t, D_MODEL), BF16)] * 2)
    n_staged = 3 * sum(1 for _, d in ATTN_GROUPS if d > 1)
    outs = pl.pallas_call(
        _proj_kernel,
        out_shape=out_shape,
        grid=(t // tm,),
        in_specs=[tok(D_MODEL), _const_spec((1, D_MODEL)),
                  _const_spec(wf.shape), _const_spec(wq.shape), _const_spec(wk.shape),
                  _const_spec(wv.shape), _const_spec(wga.shape), _const_spec(wgb.shape),
                  _const_spec(q_gain.shape), _const_spec(k_gain.shape),
                  _const_spec(cc.shape), _const_spec(sc.shape), pos, pos, pos],
        out_specs=[tok(FOURIER_WIDTH)] * 2 + grp_specs * 3 + [tok(D_MODEL)] * 2,
        scratch_shapes=[pltpu.VMEM((n_staged, HEADS_PER_GROUP, tm, HEAD_DIM), F32)],
        compiler_params=_cparams(("arbitrary",)),
        name="proj",
        interpret=interpret,
    )(x2, g_mix.reshape(1, D_MODEL), wf, wq, wk, wv, wga, wgb, q_gain, k_gain, cc, sc,
      rc, rs1, rs2)
    fa, fb = outs[0], outs[1]
    q, k, v = outs[2:5], outs[5:8], outs[8:11]
    return fa, fb, q, k, v, outs[11], outs[12]


LD_LANES = LANES // HEADS_PER_GROUP


def _attn_kernel(q_ref, k_ref, v_ref, o_ref, ld_ref, *, sub_len, heads):
    kb = min(2 * QB_ATTN, sub_len)
    span = QB_ATTN // 2
    first_head = pl.program_id(2) * heads

    if heads < HEADS_PER_GROUP:
        @pl.when(first_head == 0)
        def _():
            ld_ref[...] = jnp.zeros_like(ld_ref)

    def body(i, carry):
        start = pl.multiple_of(i * QB_ATTN, QB_ATTN)
        ks = pl.multiple_of(jnp.clip(start - span, 0, sub_len - kb), span)
        qpos = start + lax.broadcasted_iota(jnp.int32, (QB_ATTN, kb), 0)
        kpos = ks + lax.broadcasted_iota(jnp.int32, (QB_ATTN, kb), 1)
        in_band = jnp.abs(kpos - qpos) <= span
        lane_head = lax.broadcasted_iota(jnp.int32, (QB_ATTN, LANES), 1) // LD_LANES
        if heads < HEADS_PER_GROUP:
            ld_all = ld_ref[pl.ds(start, QB_ATTN), :]
        else:
            ld_all = jnp.zeros((QB_ATTN, LANES), F32)
        for hl in range(heads):
            cols = slice(hl * HEAD_DIM, (hl + 1) * HEAD_DIM)
            q = q_ref[pl.ds(start, QB_ATTN), cols]
            k = k_ref[pl.ds(ks, kb), cols]
            v = v_ref[pl.ds(ks, kb), cols]
            s = lax.dot_general(q, k, (((1,), (1,)), ((), ())), preferred_element_type=F32)
            s = jnp.where(in_band, s, NEG_INF)
            m = jnp.max(s, axis=1, keepdims=True)
            p = jnp.exp(s - m)
            l = jnp.sum(p, axis=1, keepdims=True)
            o = jnp.dot(p.astype(BF16), v, preferred_element_type=F32) / l
            o_ref[pl.ds(start, QB_ATTN), cols] = o.astype(BF16)
            ld_all = jnp.where(lane_head == first_head + hl, m + jnp.log(l), ld_all)
        ld_ref[pl.ds(start, QB_ATTN), :] = ld_all
        return carry

    nq = sub_len // QB_ATTN
    lax.fori_loop(0, nq, body, 0, unroll=min(nq, 2 * HEADS_PER_GROUP // heads))


ATTN_MAX_WIDE_LEN = 2048


def _attn_group(q, k, v, dilation, interpret):
    b, _, sub_len, _ = q.shape
    assert sub_len % QB_ATTN == 0
    heads = HEADS_PER_GROUP if sub_len <= ATTN_MAX_WIDE_LEN else 1
    width = heads * HEAD_DIM
    spec = pl.BlockSpec((None, None, sub_len, width), lambda bi, r, hb: (bi, r, 0, hb))
    return pl.pallas_call(
        functools.partial(_attn_kernel, sub_len=sub_len, heads=heads),
        out_shape=(jax.ShapeDtypeStruct(q.shape, BF16),
                   jax.ShapeDtypeStruct((b, dilation, sub_len, LANES), F32)),
        grid=(b, dilation, HEADS_PER_GROUP // heads),
        in_specs=[spec, spec, spec],
        out_specs=(spec, pl.BlockSpec((None, None, sub_len, LANES),
                                      lambda bi, r, hb: (bi, r, 0, 0))),
        compiler_params=_cparams(("arbitrary", "arbitrary", "arbitrary")),
        name=f"attn_d{dilation}",
        interpret=interpret,
    )(q, k, v)


def _dft_kernel(cs_ref, ss_ref, a_ref, b_ref, y_ref, acc_ref):
    kk = pl.program_id(2)

    @pl.when(kk == 0)
    def _():
        acc_ref[...] = jnp.zeros_like(acc_ref)

    acc_ref[...] += (jnp.dot(cs_ref[...], a_ref[...], preferred_element_type=F32)
                     - jnp.dot(ss_ref[...], b_ref[...], preferred_element_type=F32))

    @pl.when(kk == pl.num_programs(2) - 1)
    def _():
        y_ref[...] = acc_ref[...].astype(BF16)


def _dft_mats_kernel(ch_ref, sh_ref, cl_ref, sl_ref, cs_ref, ss_ref):
    cl, sl = cl_ref[...], sl_ref[...]
    for k in range(ch_ref.shape[1]):
        c, s_ = ch_ref[:, k:k + 1], sh_ref[:, k:k + 1]
        cols = slice(k * LANES, (k + 1) * LANES)
        cs_ref[:, cols] = (c * cl - s_ * sl).astype(BF16)
        ss_ref[:, cols] = (s_ * cl + c * sl).astype(BF16)


DFT_GEN_ROWS = 256


def _seq_dft_mats(s, interpret):
    lo = LANES
    hi = s // lo
    rows = jnp.arange(s, dtype=jnp.int32)[:, None]
    scale = 2.0 * math.pi / s
    ang_hi = ((rows * (jnp.arange(hi, dtype=jnp.int32)[None, :] * lo)) % s).astype(F32) * scale
    ang_lo = ((rows * jnp.arange(lo, dtype=jnp.int32)[None, :]) % s).astype(F32) * scale
    norm = s ** -0.5
    tr = DFT_GEN_ROWS
    coarse = pl.BlockSpec((tr, hi), lambda i: (i, 0))
    fine = pl.BlockSpec((tr, lo), lambda i: (i, 0))
    out = pl.BlockSpec((tr, s), lambda i: (i, 0))
    return pl.pallas_call(
        _dft_mats_kernel,
        out_shape=(jax.ShapeDtypeStruct((s, s), BF16),) * 2,
        grid=(s // tr,),
        in_specs=[coarse, coarse, fine, fine],
        out_specs=(out, out),
        compiler_params=_cparams(("arbitrary",)),
        name="dft_mats",
        interpret=interpret,
    )(jnp.cos(ang_hi), jnp.sin(ang_hi), jnp.cos(ang_lo) * norm, jnp.sin(ang_lo) * norm)


def _seq_dft(fa, fb, b, s, interpret):
    ts = min(DFT_TS, s)
    tk = min(DFT_TK, s)
    cs, ss = _seq_dft_mats(s, interpret)
    mat = pl.BlockSpec((ts, tk), lambda bi, i, kk: (i, kk))
    vec = pl.BlockSpec((None, tk, FOURIER_WIDTH), lambda bi, i, kk: (bi, kk, 0))
    y = pl.pallas_call(
        _dft_kernel,
        out_shape=jax.ShapeDtypeStruct((b, s, FOURIER_WIDTH), BF16),
        grid=(b, s // ts, s // tk),
        in_specs=[mat, mat, vec, vec],
        out_specs=pl.BlockSpec((None, ts, FOURIER_WIDTH), lambda bi, i, kk: (bi, i, 0)),
        scratch_shapes=[pltpu.VMEM((ts, FOURIER_WIDTH), F32)],
        compiler_params=_cparams(("arbitrary", "arbitrary", "arbitrary")),
        name="seq_dft",
        interpret=interpret,
    )(cs, ss, fa.reshape(b, s, FOURIER_WIDTH), fb.reshape(b, s, FOURIER_WIDTH))
    return y.reshape(b * s, FOURIER_WIDTH)


def _mix_kernel(x_ref, o0_ref, o1_ref, o2_ref, l0_ref, l1_ref, l2_ref, y_ref, sga_ref, sgb_ref,
                wat_ref, wfo_ref, wout_ref, gf_ref, wqy_ref, sk_ref, hx_ref,
                h_ref, hnt_ref, st_ref, ostage_ref, lstage_ref):
    tm = x_ref.shape[0]

    def token_order(ref, stage, gi):
        dilation = ATTN_GROUPS[gi][1]
        if dilation == 1:
            return ref[0].astype(F32)
        cols = []
        for c in range(ref.shape[-1] // LANES):
            for r in range(dilation):
                stage[gi - 1, c, pl.ds(r, tm // dilation, stride=dilation), :] = (
                    ref[r, :, c * LANES:(c + 1) * LANES].astype(F32))
            cols.append(stage[gi - 1, c])
        return cols[0] if len(cols) == 1 else jnp.concatenate(cols, axis=1)

    lds = [token_order(ref, lstage_ref, gi) for gi, ref in enumerate((l0_ref, l1_ref, l2_ref))]
    mx = jnp.maximum(jnp.maximum(lds[0], lds[1]), lds[2])
    ws = [jnp.exp(ld - mx) for ld in lds]
    inv = 1.0 / (ws[0] + ws[1] + ws[2])
    attn = jnp.zeros((tm, ATTN_GROUP_WIDTH), F32)
    for gi, ref in enumerate((o0_ref, o1_ref, o2_ref)):
        wexp = jnp.dot((ws[gi] * inv).astype(BF16), hx_ref[...], preferred_element_type=F32)
        attn = attn + wexp * token_order(ref, ostage_ref, gi)
    attn = jnp.dot(attn.astype(BF16), wat_ref[...], preferred_element_type=F32)
    four = jnp.dot(y_ref[...], wfo_ref[...], preferred_element_type=F32)
    mix = sga_ref[...].astype(F32) * four + sgb_ref[...].astype(F32) * attn
    h = x_ref[...] + jnp.dot(mix.astype(BF16), wout_ref[...], preferred_element_type=F32)
    h_ref[...] = h
    ms = jnp.mean(h * h, axis=-1, keepdims=True)
    hn32 = h * lax.rsqrt(ms + NORM_EPS) * gf_ref[...]
    hn = hn32.astype(BF16)
    hnt_ref[...] = hn32.T.astype(BF16)
    qy = jnp.dot(hn, wqy_ref[...], preferred_element_type=F32).astype(BF16)
    half = PEER_KEY_DIM // 2
    for hc in range(2 * PEER_HEADS):
        st_ref[hc] = lax.dot_general(sk_ref[hc], qy[:, hc * half:(hc + 1) * half],
                                     (((1,), (1,)), ((), ())), preferred_element_type=F32)


def _mix(x2, o_list, ld_list, yf, sga, sgb, w_attn, w_fourier, w_out, g_ffn, w_query, sub_keys,
         interpret):
    t = x2.shape[0]
    tm = TM_MIX
    spt = o_list[0].shape[2] // tm
    tok = lambda w: pl.BlockSpec((tm, w), lambda i: (i, 0))
    grp = lambda w: [pl.BlockSpec((None, d, tm // d, w), lambda i: (i // spt, 0, i % spt, 0))
                     for _, d in ATTN_GROUPS]
    head_expand = (jnp.arange(LANES)[:, None]
                   == LD_LANES * (jnp.arange(ATTN_GROUP_WIDTH)[None, :] // HEAD_DIM)).astype(BF16)
    n_staged = sum(1 for _, d in ATTN_GROUPS if d > 1)
    sk = jnp.stack([sub_keys[:, 0][:, jnp.array(_PAIR_ORDER)], sub_keys[:, 1]], axis=1)
    sk = sk.reshape(2 * PEER_HEADS, N_KEYS, PEER_KEY_DIM // 2).astype(BF16)
    wat, wfo, wout, wqy = (w_attn.astype(BF16), w_fourier.astype(BF16), w_out.astype(BF16),
                           w_query.astype(BF16))
    return pl.pallas_call(
        _mix_kernel,
        out_shape=(jax.ShapeDtypeStruct((t, D_MODEL), F32),
                   jax.ShapeDtypeStruct((D_MODEL, t), BF16),
                   jax.ShapeDtypeStruct((2 * PEER_HEADS, N_KEYS, t), F32)),
        grid=(t // tm,),
        in_specs=[tok(D_MODEL)] + grp(ATTN_GROUP_WIDTH) + grp(LANES) + [tok(FOURIER_WIDTH),
                  tok(D_MODEL), tok(D_MODEL),
                  _const_spec(wat.shape), _const_spec(wfo.shape), _const_spec(wout.shape),
                  _const_spec((1, D_MODEL)), _const_spec(wqy.shape), _const_spec(sk.shape),
                  _const_spec(head_expand.shape)],
        out_specs=(tok(D_MODEL), pl.BlockSpec((D_MODEL, tm), lambda i: (0, i)),
                   pl.BlockSpec((2 * PEER_HEADS, N_KEYS, tm), lambda i: (0, 0, i))),
        scratch_shapes=[pltpu.VMEM((n_staged, HEADS_PER_GROUP, tm, HEAD_DIM), F32),
                        pltpu.VMEM((n_staged, 1, tm, LANES), F32)],
        compiler_params=_cparams(("arbitrary",)),
        name="mix",
        interpret=interpret,
    )(x2, *o_list, *ld_list, yf, sga, sgb, wat, wfo, wout, g_ffn.reshape(1, D_MODEL), wqy, sk,
      head_expand)


N_PAIRS = N_KEYS // 2
_PAIR_ORDER = tuple(range(0, N_KEYS, 2)) + tuple(range(1, N_KEYS, 2))
SUBLANES = 8


def _batcher_network(n):
    def merge(lo, hi, r):
        step = r * 2
        if step < hi - lo:
            yield from merge(lo, hi, step)
            yield from merge(lo + r, hi, step)
            for i in range(lo + r, hi - r, step):
                yield (i, i + r)
        else:
            yield (lo, lo + r)

    def sort(lo, hi):
        if hi - lo >= 1:
            mid = lo + (hi - lo) // 2
            yield from sort(lo, mid)
            yield from sort(mid + 1, hi)
            yield from merge(lo, hi, 1)

    return tuple(sort(0, n - 1))


_SORT16 = _batcher_network(PEER_TOPK)


def _compare_exchange(v, i, j):
    v[i], v[j] = jnp.maximum(v[i], v[j]), jnp.minimum(v[i], v[j])


def _top16_sorted(keys):
    n = PEER_TOPK
    v = [keys[SUBLANES * k:SUBLANES * (k + 1)] for k in range(n)]
    for i, j in _SORT16:
        _compare_exchange(v, i, j)
    shift = SUBLANES // 2
    while shift >= 1:
        other = [pltpu.roll(x, shift, 0) for x in v]
        v = [jnp.maximum(v[k], other[n - 1 - k]) for k in range(n)]
        d = n // 2
        while d >= 1:
            for i in range(n):
                if not i & d:
                    _compare_exchange(v, i, i + d)
            d //= 2
        shift //= 2
    return [x[0:1] for x in v]


_CAND_PAIRS = tuple((i, j) for i in range(PEER_TOPK) for j in range(PEER_TOPK)
                    if (i + 1) * (j + 1) <= PEER_TOPK)
_CAND_ROWS = -(-len(_CAND_PAIRS) // SUBLANES) * SUBLANES
TIE_STEP = 2.0 ** -17


def _kth_slot_value(vals, k):
    slot = lax.broadcasted_iota(jnp.int32, vals.shape, 0).astype(F32)
    bumped = vals * (1.0 + slot * TIE_STEP)
    cur = bumped
    m = None
    for _ in range(k):
        m = jnp.max(cur, axis=0, keepdims=True)
        cur = jnp.where(cur == m, -1.0, cur)
    return bumped, m


def _candidates(fa, fb):
    rows = [fa[i] * fb[j] for i, j in _CAND_PAIRS]
    rows.append(jnp.zeros((_CAND_ROWS - len(rows),) + rows[0].shape[1:], rows[0].dtype))
    return jnp.concatenate(rows, axis=0)


def _topk_kernel(st_ref, e1_ref, e2_ref, th_ref):
    tm = st_ref.shape[2]
    for h in range(PEER_HEADS):
        for tc in range(tm // LANES):
            lanes = slice(tc * LANES, (tc + 1) * LANES)
            s1 = st_ref[2 * h, :, lanes]
            s2 = st_ref[2 * h + 1, :, lanes]
            a = _top16_sorted(s1)
            b = _top16_sorted(s2)
            ea = [jnp.exp(ai - a[0]) for ai in a]
            eb = [jnp.exp(bi - b[0]) for bi in b]
            cand = _candidates(ea, eb)
            bumped, kth = _kth_slot_value(cand, PEER_TOPK)
            z = jnp.sum(jnp.where(bumped >= kth, cand, 0.0), axis=0, keepdims=True)
            inv_z = 1.0 / z
            cand_b = _candidates([(x * inv_z).astype(BF16) for x in ea],
                                 [x.astype(BF16) for x in eb]).astype(F32)
            _, kth_b = _kth_slot_value(cand_b, PEER_TOPK)
            th_ref[h:h + 1, lanes] = kth_b.astype(BF16).astype(F32)
            e1 = jnp.exp(s1 - a[0]) * inv_z
            e1_ref[:, 2 * h, lanes] = e1[:N_PAIRS]
            e1_ref[:, 2 * h + 1, lanes] = e1[N_PAIRS:]
            e2_ref[h, :, lanes] = jnp.exp(s2 - b[0]).astype(BF16)


def _topk(st, interpret):
    t = st.shape[2]
    tm = TM_TOPK
    blk = lambda n: pl.BlockSpec((n, N_KEYS, tm), lambda i: (0, 0, i))
    return pl.pallas_call(
        _topk_kernel,
        out_shape=(jax.ShapeDtypeStruct((N_PAIRS, 2 * PEER_HEADS, t), F32),
                   jax.ShapeDtypeStruct((PEER_HEADS, N_KEYS, t), BF16),
                   jax.ShapeDtypeStruct((PEER_HEADS, t), F32)),
        grid=(t // tm,),
        in_specs=[blk(2 * PEER_HEADS)],
        out_specs=(pl.BlockSpec((N_PAIRS, 2 * PEER_HEADS, tm), lambda i: (0, 0, i)),
                   blk(PEER_HEADS),
                   pl.BlockSpec((PEER_HEADS, tm), lambda i: (0, i))),
        compiler_params=_cparams(("arbitrary",)),
        name="peer_topk",
        interpret=interpret,
    )(st)


SB_PEER = 2 * N_KEYS
KB_PEER = 32
SL_PEER = 2 * SB_PEER


def _peer_kernel(hnt_ref, u_ref, vt_ref, e1_ref, e2_ref, th_ref, h_ref, y_ref,
                 at0_ref, at1_ref, ht0_ref, ht1_ref, acc_ref, join_ref, *, nj):
    s = pl.program_id(0)
    tm = hnt_ref.shape[1]
    j_out = lax.rem(jnp.maximum(s - 2, 0), nj)

    @pl.when(s == 0)
    def _():
        for ref in (at0_ref, at1_ref, ht0_ref, ht1_ref):
            ref[...] = jnp.zeros_like(ref)

    @pl.when(j_out == 0)
    def _():
        acc_ref[...] = jnp.zeros_like(acc_ref)

    def stages(at_w, at_r, ht_w, ht_r):
        def slice_body(q, carry):
            r0 = pl.multiple_of(q * SL_PEER, SL_PEER)
            at_new = jnp.dot(u_ref[pl.ds(r0, SL_PEER), :], hnt_ref[...],
                             preferred_element_type=F32)
            at_w[pl.ds(r0, SL_PEER), :] = at_new
            join = at_new[SL_PEER - 8:, tm - LANES:]
            for kp in range(SL_PEER // SB_PEER):
                key_pair = q * (SL_PEER // SB_PEER) + kp
                for par in range(2):
                    for tc in range(tm // LANES):
                        lanes = slice(tc * LANES, (tc + 1) * LANES)
                        e1b = [jnp.broadcast_to(
                            e1_ref[key_pair, 2 * h + par:2 * h + par + 1, lanes].astype(BF16),
                            (KB_PEER, LANES)) for h in range(PEER_HEADS)]
                        thb = [jnp.broadcast_to(th_ref[h:h + 1, lanes].astype(BF16),
                                                (KB_PEER, LANES)) for h in range(PEER_HEADS)]
                        for kb in range(N_KEYS // KB_PEER):
                            keys = slice(kb * KB_PEER, (kb + 1) * KB_PEER)
                            ra = pl.multiple_of(
                                r0 + kp * SB_PEER + par * N_KEYS + kb * KB_PEER, KB_PEER)
                            gate = jnp.zeros((KB_PEER, LANES), BF16)
                            for h in range(PEER_HEADS):
                                p = e1b[h] * e2_ref[h, keys, lanes]
                                gate = gate + jnp.where(p >= thb[h], p, jnp.zeros_like(p))
                            a = at_r[pl.ds(ra, KB_PEER), lanes].astype(BF16)
                            act = jax.nn.gelu(a) * gate
                            ht_w[pl.ds(ra, KB_PEER), lanes] = act
                            join = join + act[:8].astype(F32)
            out_new = jnp.dot(vt_ref[pl.ds(r0, SL_PEER), :], ht_r[...],
                              preferred_element_type=F32)
            acc_ref[pl.ds(r0, SL_PEER), :] += out_new
            return carry + join + out_new[SL_PEER - 8:, tm - LANES:]

        jv = lax.fori_loop(0, u_ref.shape[0] // SL_PEER, slice_body, jnp.zeros((8, LANES), F32))
        join_ref[...] = jv

    parity = lax.rem(s, 2)

    @pl.when(parity == 0)
    def _():
        stages(at0_ref, at1_ref, ht1_ref, ht0_ref)

    @pl.when(parity == 1)
    def _():
        stages(at1_ref, at0_ref, ht0_ref, ht1_ref)

    @pl.when((j_out == nj - 1) & (s >= 2))
    def _():
        y_ref[...] = h_ref[...] + acc_ref[...].T


def _expert_table_kernel(u_ref, v_ref, ub_ref, vt_ref):
    ub_ref[...] = u_ref[...].astype(BF16)
    vt_ref[...] = v_ref[...].T.astype(BF16)


def _expert_tables(expert_u, expert_v, layer, eb, interpret):
    nj = N_EXPERTS // eb
    blk = pl.BlockSpec((eb, D_MODEL), lambda j: (j, 0))
    layer_blk = pl.BlockSpec((None, eb, D_MODEL), lambda j: (layer, j, 0))
    return pl.pallas_call(
        _expert_table_kernel,
        out_shape=(jax.ShapeDtypeStruct((N_EXPERTS, D_MODEL), BF16),
                   jax.ShapeDtypeStruct((nj, D_MODEL, eb), BF16)),
        grid=(nj,),
        in_specs=[layer_blk, layer_blk],
        out_specs=(blk, pl.BlockSpec((None, D_MODEL, eb), lambda j: (j, 0, 0))),
        compiler_params=_cparams(("arbitrary",)),
        name="expert_tables",
        interpret=interpret,
    )(expert_u, expert_v)


def _peer(hnt, h, e1, e2, th, u, vt, interpret):
    t = hnt.shape[1]
    tm, eb = TM_PEER, EB_PEER
    assert eb == D_MODEL
    nj = N_EXPERTS // eb
    n_pairs = (t // tm) * nj
    tile = lambda s: s // nj
    blk = lambda s: lax.rem(s, nj)
    s1 = lambda s: jnp.minimum(s, n_pairs - 1)
    s2 = lambda s: jnp.clip(s - 1, 0, n_pairs - 1)
    s3 = lambda s: jnp.maximum(s - 2, 0)
    return pl.pallas_call(
        functools.partial(_peer_kernel, nj=nj),
        out_shape=jax.ShapeDtypeStruct((t, D_MODEL), F32),
        grid=(n_pairs + 2,),
        in_specs=[pl.BlockSpec((D_MODEL, tm), lambda s: (0, tile(s1(s)))),
                  pl.BlockSpec((eb, D_MODEL), lambda s: (blk(s1(s)), 0)),
                  pl.BlockSpec((None, D_MODEL, eb), lambda s: (blk(s3(s)), 0, 0)),
                  pl.BlockSpec((eb // SB_PEER, 2 * PEER_HEADS, tm),
                               lambda s: (blk(s2(s)), 0, tile(s2(s)))),
                  pl.BlockSpec((PEER_HEADS, N_KEYS, tm), lambda s: (0, 0, tile(s2(s)))),
                  pl.BlockSpec((PEER_HEADS, tm), lambda s: (0, tile(s2(s)))),
                  pl.BlockSpec((tm, D_MODEL), lambda s: (tile(s3(s)), 0))],
        out_specs=pl.BlockSpec((tm, D_MODEL), lambda s: (tile(s3(s)), 0)),
        scratch_shapes=[pltpu.VMEM((eb, tm), F32), pltpu.VMEM((eb, tm), F32),
                        pltpu.VMEM((eb, tm), BF16), pltpu.VMEM((eb, tm), BF16),
                        pltpu.VMEM((D_MODEL, tm), F32), pltpu.VMEM((8, LANES), F32)],
        compiler_params=_cparams(("arbitrary",)),
        name="peer_experts",
        interpret=interpret,
    )(hnt, u, vt, e1, e2, th, h)


def _hybrid_block(x, g_mix, w_in, q_gain, k_gain, w_fourier, w_attn, w_out, g_ffn, w_query,
                  sub_keys, expert_u_b, expert_vt_b, interpret=False):
    b, s, _ = x.shape
    x2 = x.reshape(b * s, D_MODEL)
    fa, fb, q, k, v, sga, sgb = _proj(x2, s, g_mix, w_in, q_gain, k_gain, interpret)
    o_list, ld_list = [], []
    for gi, (_, dilation) in enumerate(ATTN_GROUPS):
        o, ld = _attn_group(q[gi], k[gi], v[gi], dilation, interpret)
        o_list.append(o)
        ld_list.append(ld)
    yf = _seq_dft(fa, fb, b, s, interpret)
    h, hnt, st = _mix(x2, o_list, ld_list, yf, sga, sgb, w_attn, w_fourier, w_out, g_ffn,
                      w_query, sub_keys, interpret)
    e1, e2, th = _topk(st, interpret)
    y = _peer(hnt, h, e1, e2, th, expert_u_b, expert_vt_b, interpret)
    return y.reshape(b, s, D_MODEL)


def kernel(x_prompt, x_sample, g_mix, w_in, q_gain, k_gain, w_fourier, w_attn, w_out, g_ffn,
           w_query, sub_keys, expert_u, expert_v):
    y_prompt, y_sample = x_prompt, x_sample
    for layer in range(g_mix.shape[0]):
        params = (g_mix[layer], w_in[layer], q_gain[layer], k_gain[layer], w_fourier[layer],
                  w_attn[layer], w_out[layer], g_ffn[layer], w_query[layer], sub_keys[layer],
                  *_expert_tables(expert_u, expert_v, layer, EB_PEER, False))
        y_prompt = _hybrid_block(y_prompt, *params)
        y_sample = _hybrid_block(y_sample, *params)
    return (y_prompt, y_sample)
```

```python
import functools
import math

import jax
import jax.numpy as jnp
from jax import lax
from jax.experimental import pallas as pl
from jax.experimental.pallas import tpu as pltpu

F32 = jnp.float32
BF16 = jnp.bfloat16

D_MODEL = 1024
FOURIER_GROUPS = 4
FOURIER_GROUP_DIM = 128
FOURIER_WIDTH = FOURIER_GROUPS * FOURIER_GROUP_DIM
ATTN_GROUPS = ((128, 1), (512, 4), (2048, 16))
N_ATTN_GROUPS = 3
HEADS_PER_GROUP = 4
HEAD_DIM = 128
ATTN_GROUP_WIDTH = HEADS_PER_GROUP * HEAD_DIM
ATTN_WIDTH = N_ATTN_GROUPS * ATTN_GROUP_WIDTH
ROPE_DIM = HEAD_DIM // 4
ROPE_THETA = 500000.0
N_KEYS = 128
N_EXPERTS = N_KEYS * N_KEYS
PEER_HEADS = 8
PEER_TOPK = 16
PEER_KEY_DIM = 256
NORM_EPS = 1e-6
NEG_INF = -1e30

LANES = 128
VMEM_LIMIT = 56 * 1024 * 1024

TM_PROJ = 512
TM_MIX = 512
TM_TOPK = 512
TM_PEER = 512
EB_PEER = 1024
QB_ATTN = 128
DFT_TS = 1024
DFT_TK = 2048


def _cparams(sem):
    return pltpu.CompilerParams(dimension_semantics=sem, vmem_limit_bytes=VMEM_LIMIT)


def _const_spec(shape):
    nd = len(shape)
    return pl.BlockSpec(shape, lambda *_: (0,) * nd)


def _proj_kernel(x_ref, g_ref, wf_ref, wq_ref, wk_ref, wv_ref, wga_ref, wgb_ref,
                 qg_ref, kg_ref, cc_ref, sc_ref, rc_ref, rs1_ref, rs2_ref,
                 fa_ref, fb_ref, q0_ref, q1_ref, q2_ref, k0_ref, k1_ref, k2_ref,
                 v0_ref, v1_ref, v2_ref, sga_ref, sgb_ref, stage_ref):
    tm = x_ref.shape[0]
    hpg = HEADS_PER_GROUP
    stage_slot = [0]

    def store_group(out_ref, gi, cols):
        dilation = ATTN_GROUPS[gi][1]
        if dilation == 1:
            for hl, c in enumerate(cols):
                out_ref[0, :, hl * HEAD_DIM:(hl + 1) * HEAD_DIM] = c.astype(BF16)
            return
        slot = stage_slot[0]
        stage_slot[0] += 1
        for hl, c in enumerate(cols):
            stage_ref[slot, hl] = c
            for r in range(dilation):
                out_ref[r, :, hl * HEAD_DIM:(hl + 1) * HEAD_DIM] = stage_ref[
                    slot, hl, pl.ds(r, tm // dilation, stride=dilation), :].astype(BF16)

    x = x_ref[...]
    ms = jnp.mean(x * x, axis=-1, keepdims=True)
    xn = (x * lax.rsqrt(ms + NORM_EPS) * g_ref[...]).astype(BF16)

    f = jnp.dot(xn, wf_ref[...], preferred_element_type=F32).astype(BF16)
    fa_ref[...] = jnp.dot(f, cc_ref[...], preferred_element_type=F32).astype(BF16)
    fb_ref[...] = jnp.dot(f, sc_ref[...], preferred_element_type=F32).astype(BF16)

    rc = rc_ref[...]
    rs1 = rs1_ref[...]
    rs2 = rs2_ref[...]

    def norm_rope(w_ref, gain_ref, out_refs, scale):
        y = jnp.dot(xn, w_ref[...], preferred_element_type=F32)
        for gi in range(N_ATTN_GROUPS):
            cols = []
            for hh in range(gi * hpg, (gi + 1) * hpg):
                yh = y[:, hh * HEAD_DIM:(hh + 1) * HEAD_DIM]
                hms = jnp.mean(yh * yh, axis=-1, keepdims=True)
                yn = yh * lax.rsqrt(hms + NORM_EPS) * gain_ref[gi:gi + 1, :]
                r = (yn * rc + pltpu.roll(yn, LANES - ROPE_DIM // 2, 1) * rs1
                     + pltpu.roll(yn, ROPE_DIM // 2, 1) * rs2)
                cols.append(r * scale)
            store_group(out_refs[gi], gi, cols)

    norm_rope(wq_ref, qg_ref, (q0_ref, q1_ref, q2_ref), HEAD_DIM ** -0.5)
    norm_rope(wk_ref, kg_ref, (k0_ref, k1_ref, k2_ref), 1.0)
    yv = jnp.dot(xn, wv_ref[...], preferred_element_type=F32)
    for gi, out_ref in enumerate((v0_ref, v1_ref, v2_ref)):
        store_group(out_ref, gi, [yv[:, hh * HEAD_DIM:(hh + 1) * HEAD_DIM]
                                  for hh in range(gi * hpg, (gi + 1) * hpg)])
    sga_ref[...] = jax.nn.sigmoid(
        jnp.dot(xn, wga_ref[...], preferred_element_type=F32)).astype(BF16)
    sgb_ref[...] = jax.nn.sigmoid(
        jnp.dot(xn, wgb_ref[...], preferred_element_type=F32)).astype(BF16)


def _rope_tables(s):
    half = ROPE_DIM // 2
    inv_freq = ROPE_THETA ** (-jnp.arange(half, dtype=F32) * 2.0 / ROPE_DIM)
    ang = jnp.arange(s, dtype=F32)[:, None] * inv_freq[None, :]
    cos, sin = jnp.cos(ang), jnp.sin(ang)
    zeros = jnp.zeros((s, HEAD_DIM - ROPE_DIM), F32)
    zh = jnp.zeros((s, half), F32)
    rc = jnp.concatenate([cos, cos, jnp.ones((s, HEAD_DIM - ROPE_DIM), F32)], axis=1)
    rs1 = jnp.concatenate([-sin, zh, zeros], axis=1)
    rs2 = jnp.concatenate([zh, sin, zeros], axis=1)
    return rc, rs1, rs2


def _channel_dft_mats():
    n = FOURIER_GROUP_DIM
    idx = (jnp.arange(n)[:, None] * jnp.arange(n)[None, :]) % n
    ang = idx.astype(F32) * (2.0 * math.pi / n)
    eye = jnp.eye(FOURIER_GROUPS, dtype=F32)
    cc = jnp.kron(eye, jnp.cos(ang) * n ** -0.5)
    sc = jnp.kron(eye, jnp.sin(ang) * n ** -0.5)
    return cc.astype(BF16), sc.astype(BF16)


def _proj(x2, s, g_mix, w_in, q_gain, k_gain, interpret):
    t = x2.shape[0]
    tm = TM_PROJ
    c0 = FOURIER_WIDTH
    c1 = c0 + ATTN_WIDTH
    c2 = c1 + ATTN_WIDTH
    c3 = c2 + ATTN_WIDTH
    c4 = c3 + D_MODEL
    wb = w_in.astype(BF16)
    wf, wq, wk, wv, wga, wgb = (wb[:, :c0], wb[:, c0:c1], wb[:, c1:c2], wb[:, c2:c3],
                                wb[:, c3:c4], wb[:, c4:])
    cc, sc = _channel_dft_mats()
    rc, rs1, rs2 = _rope_tables(s)
    spt = s // tm
    b = t // s
    tok = lambda w: pl.BlockSpec((tm, w), lambda i: (i, 0))
    pos = pl.BlockSpec((tm, HEAD_DIM), lambda i: (i % spt, 0))
    grp_shapes = [jax.ShapeDtypeStruct((b, d, s // d, ATTN_GROUP_WIDTH), BF16)
                  for _, d in ATTN_GROUPS]
    grp_specs = [pl.BlockSpec((None, d, tm // d, ATTN_GROUP_WIDTH),
                              lambda i: (i // spt, 0, i % spt, 0)) for _, d in ATTN_GROUPS]
    out_shape = ([jax.ShapeDtypeStruct((t, FOURIER_WIDTH), BF16)] * 2 + grp_shapes * 3
                 + [jax.ShapeDtypeStruct((t, D_MODEL), BF16)] * 2)
    n_staged = 3 * sum(1 for _, d in ATTN_GROUPS if d > 1)
    outs = pl.pallas_call(
        _proj_kernel,
        out_shape=out_shape,
        grid=(t // tm,),
        in_specs=[tok(D_MODEL), _const_spec((1, D_MODEL)),
                  _const_spec(wf.shape), _const_spec(wq.shape), _const_spec(wk.shape),
                  _const_spec(wv.shape), _const_spec(wga.shape), _const_spec(wgb.shape),
                  _const_spec(q_gain.shape), _const_spec(k_gain.shape),
                  _const_spec(cc.shape), _const_spec(sc.shape), pos, pos, pos],
        out_specs=[tok(FOURIER_WIDTH)] * 2 + grp_specs * 3 + [tok(D_MODEL)] * 2,
        scratch_shapes=[pltpu.VMEM((n_staged, HEADS_PER_GROUP, tm, HEAD_DIM), F32)],
        compiler_params=_cparams(("arbitrary",)),
        name="proj",
        interpret=interpret,
    )(x2, g_mix.reshape(1, D_MODEL), wf, wq, wk, wv, wga, wgb, q_gain, k_gain, cc, sc,
      rc, rs1, rs2)
    fa, fb = outs[0], outs[1]
    q, k, v = outs[2:5], outs[5:8], outs[8:11]
    return fa, fb, q, k, v, outs[11], outs[12]


LD_LANES = LANES // HEADS_PER_GROUP


def _attn_kernel(q_ref, k_ref, v_ref, o_ref, ld_ref, *, sub_len, heads):
    kb = min(2 * QB_ATTN, sub_len)
    span = QB_ATTN // 2
    first_head = pl.program_id(2) * heads

    if heads < HEADS_PER_GROUP:
        @pl.when(first_head == 0)
        def _():
            ld_ref[...] = jnp.zeros_like(ld_ref)

    def body(i, carry):
        start = pl.multiple_of(i * QB_ATTN, QB_ATTN)
        ks = pl.multiple_of(jnp.clip(start - span, 0, sub_len - kb), span)
        qpos = start + lax.broadcasted_iota(jnp.int32, (QB_ATTN, kb), 0)
        kpos = ks + lax.broadcasted_iota(jnp.int32, (QB_ATTN, kb), 1)
        in_band = jnp.abs(kpos - qpos) <= span
        lane_head = lax.broadcasted_iota(jnp.int32, (QB_ATTN, LANES), 1) // LD_LANES
        if heads < HEADS_PER_GROUP:
            ld_all = ld_ref[pl.ds(start, QB_ATTN), :]
        else:
            ld_all = jnp.zeros((QB_ATTN, LANES), F32)
        for hl in range(heads):
            cols = slice(hl * HEAD_DIM, (hl + 1) * HEAD_DIM)
            q = q_ref[pl.ds(start, QB_ATTN), cols]
            k = k_ref[pl.ds(ks, kb), cols]
            v = v_ref[pl.ds(ks, kb), cols]
            s = lax.dot_general(q, k, (((1,), (1,)), ((), ())), preferred_element_type=F32)
            s = jnp.where(in_band, s, NEG_INF)
            m = jnp.max(s, axis=1, keepdims=True)
            p = jnp.exp(s - m)
            l = jnp.sum(p, axis=1, keepdims=True)
            o = jnp.dot(p.astype(BF16), v, preferred_element_type=F32) / l
            o_ref[pl.ds(start, QB_ATTN), cols] = o.astype(BF16)
            ld_all = jnp.where(lane_head == first_head + hl, m + jnp.log(l), ld_all)
        ld_ref[pl.ds(start, QB_ATTN), :] = ld_all
        return carry

    nq = sub_len // QB_ATTN
    lax.fori_loop(0, nq, body, 0, unroll=min(nq, 2 * HEADS_PER_GROUP // heads))


ATTN_MAX_WIDE_LEN = 2048


def _attn_group(q, k, v, dilation, interpret):
    b, _, sub_len, _ = q.shape
    assert sub_len % QB_ATTN == 0
    heads = HEADS_PER_GROUP if sub_len <= ATTN_MAX_WIDE_LEN else 1
    width = heads * HEAD_DIM
    spec = pl.BlockSpec((None, None, sub_len, width), lambda bi, r, hb: (bi, r, 0, hb))
    return pl.pallas_call(
        functools.partial(_attn_kernel, sub_len=sub_len, heads=heads),
        out_shape=(jax.ShapeDtypeStruct(q.shape, BF16),
                   jax.ShapeDtypeStruct((b, dilation, sub_len, LANES), F32)),
        grid=(b, dilation, HEADS_PER_GROUP // heads),
        in_specs=[spec, spec, spec],
        out_specs=(spec, pl.BlockSpec((None, None, sub_len, LANES),
                                      lambda bi, r, hb: (bi, r, 0, 0))),
        compiler_params=_cparams(("arbitrary", "arbitrary", "arbitrary")),
        name=f"attn_d{dilation}",
        interpret=interpret,
    )(q, k, v)


def _dft_kernel(cs_ref, ss_ref, a_ref, b_ref, y_ref, acc_ref):
    kk = pl.program_id(2)

    @pl.when(kk == 0)
    def _():
        acc_ref[...] = jnp.zeros_like(acc_ref)

    acc_ref[...] += (jnp.dot(cs_ref[...], a_ref[...], preferred_element_type=F32)
                     - jnp.dot(ss_ref[...], b_ref[...], preferred_element_type=F32))

    @pl.when(kk == pl.num_programs(2) - 1)
    def _():
        y_ref[...] = acc_ref[...].astype(BF16)


def _dft_mats_kernel(ch_ref, sh_ref, cl_ref, sl_ref, cs_ref, ss_ref):
    cl, sl = cl_ref[...], sl_ref[...]
    for k in range(ch_ref.shape[1]):
        c, s_ = ch_ref[:, k:k + 1], sh_ref[:, k:k + 1]
        cols = slice(k * LANES, (k + 1) * LANES)
        cs_ref[:, cols] = (c * cl - s_ * sl).astype(BF16)
        ss_ref[:, cols] = (s_ * cl + c * sl).astype(BF16)


DFT_GEN_ROWS = 256


def _seq_dft_mats(s, interpret):
    lo = LANES
    hi = s // lo
    rows = jnp.arange(s, dtype=jnp.int32)[:, None]
    scale = 2.0 * math.pi / s
    ang_hi = ((rows * (jnp.arange(hi, dtype=jnp.int32)[None, :] * lo)) % s).astype(F32) * scale
    ang_lo = ((rows * jnp.arange(lo, dtype=jnp.int32)[None, :]) % s).astype(F32) * scale
    norm = s ** -0.5
    tr = DFT_GEN_ROWS
    coarse = pl.BlockSpec((tr, hi), lambda i: (i, 0))
    fine = pl.BlockSpec((tr, lo), lambda i: (i, 0))
    out = pl.BlockSpec((tr, s), lambda i: (i, 0))
    return pl.pallas_call(
        _dft_mats_kernel,
        out_shape=(jax.ShapeDtypeStruct((s, s), BF16),) * 2,
        grid=(s // tr,),
        in_specs=[coarse, coarse, fine, fine],
        out_specs=(out, out),
        compiler_params=_cparams(("arbitrary",)),
        name="dft_mats",
        interpret=interpret,
    )(jnp.cos(ang_hi), jnp.sin(ang_hi), jnp.cos(ang_lo) * norm, jnp.sin(ang_lo) * norm)


def _seq_dft(fa, fb, b, s, interpret):
    ts = min(DFT_TS, s)
    tk = min(DFT_TK, s)
    cs, ss = _seq_dft_mats(s, interpret)
    mat = pl.BlockSpec((ts, tk), lambda bi, i, kk: (i, kk))
    vec = pl.BlockSpec((None, tk, FOURIER_WIDTH), lambda bi, i, kk: (bi, kk, 0))
    y = pl.pallas_call(
        _dft_kernel,
        out_shape=jax.ShapeDtypeStruct((b, s, FOURIER_WIDTH), BF16),
        grid=(b, s // ts, s // tk),
        in_specs=[mat, mat, vec, vec],
        out_specs=pl.BlockSpec((None, ts, FOURIER_WIDTH), lambda bi, i, kk: (bi, i, 0)),
        scratch_shapes=[pltpu.VMEM((ts, FOURIER_WIDTH), F32)],
        compiler_params=_cparams(("arbitrary", "arbitrary", "arbitrary")),
        name="seq_dft",
        interpret=interpret,
    )(cs, ss, fa.reshape(b, s, FOURIER_WIDTH), fb.reshape(b, s, FOURIER_WIDTH))
    return y.reshape(b * s, FOURIER_WIDTH)


def _mix_kernel(x_ref, o0_ref, o1_ref, o2_ref, l0_ref, l1_ref, l2_ref, y_ref, sga_ref, sgb_ref,
                wat_ref, wfo_ref, wout_ref, gf_ref, wqy_ref, sk_ref, hx_ref,
                h_ref, hnt_ref, st_ref, ostage_ref, lstage_ref):
    tm = x_ref.shape[0]

    def token_order(ref, stage, gi):
        dilation = ATTN_GROUPS[gi][1]
        if dilation == 1:
            return ref[0].astype(F32)
        cols = []
        for c in range(ref.shape[-1] // LANES):
            for r in range(dilation):
                stage[gi - 1, c, pl.ds(r, tm // dilation, stride=dilation), :] = (
                    ref[r, :, c * LANES:(c + 1) * LANES].astype(F32))
            cols.append(stage[gi - 1, c])
        return cols[0] if len(cols) == 1 else jnp.concatenate(cols, axis=1)

    lds = [token_order(ref, lstage_ref, gi) for gi, ref in enumerate((l0_ref, l1_ref, l2_ref))]
    mx = jnp.maximum(jnp.maximum(lds[0], lds[1]), lds[2])
    ws = [jnp.exp(ld - mx) for ld in lds]
    inv = 1.0 / (ws[0] + ws[1] + ws[2])
    attn = jnp.zeros((tm, ATTN_GROUP_WIDTH), F32)
    for gi, ref in enumerate((o0_ref, o1_ref, o2_ref)):
        wexp = jnp.dot((ws[gi] * inv).astype(BF16), hx_ref[...], preferred_element_type=F32)
        attn = attn + wexp * token_order(ref, ostage_ref, gi)
    attn = jnp.dot(attn.astype(BF16), wat_ref[...], preferred_element_type=F32)
    four = jnp.dot(y_ref[...], wfo_ref[...], preferred_element_type=F32)
    mix = sga_ref[...].astype(F32) * four + sgb_ref[...].astype(F32) * attn
    h = x_ref[...] + jnp.dot(mix.astype(BF16), wout_ref[...], preferred_element_type=F32)
    h_ref[...] = h
    ms = jnp.mean(h * h, axis=-1, keepdims=True)
    hn32 = h * lax.rsqrt(ms + NORM_EPS) * gf_ref[...]
    hn = hn32.astype(BF16)
    hnt_ref[...] = hn32.T.astype(BF16)
    qy = jnp.dot(hn, wqy_ref[...], preferred_element_type=F32).astype(BF16)
    half = PEER_KEY_DIM // 2
    for hc in range(2 * PEER_HEADS):
        st_ref[hc] = lax.dot_general(sk_ref[hc], qy[:, hc * half:(hc + 1) * half],
                                     (((1,), (1,)), ((), ())), preferred_element_type=F32)


def _mix(x2, o_list, ld_list, yf, sga, sgb, w_attn, w_fourier, w_out, g_ffn, w_query, sub_keys,
         interpret):
    t = x2.shape[0]
    tm = TM_MIX
    spt = o_list[0].shape[2] // tm
    tok = lambda w: pl.BlockSpec((tm, w), lambda i: (i, 0))
    grp = lambda w: [pl.BlockSpec((None, d, tm // d, w), lambda i: (i // spt, 0, i % spt, 0))
                     for _, d in ATTN_GROUPS]
    head_expand = (jnp.arange(LANES)[:, None]
                   == LD_LANES * (jnp.arange(ATTN_GROUP_WIDTH)[None, :] // HEAD_DIM)).astype(BF16)
    n_staged = sum(1 for _, d in ATTN_GROUPS if d > 1)
    sk = jnp.stack([sub_keys[:, 0][:, jnp.array(_PAIR_ORDER)], sub_keys[:, 1]], axis=1)
    sk = sk.reshape(2 * PEER_HEADS, N_KEYS, PEER_KEY_DIM // 2).astype(BF16)
    wat, wfo, wout, wqy = (w_attn.astype(BF16), w_fourier.astype(BF16), w_out.astype(BF16),
                           w_query.astype(BF16))
    return pl.pallas_call(
        _mix_kernel,
        out_shape=(jax.ShapeDtypeStruct((t, D_MODEL), F32),
                   jax.ShapeDtypeStruct((D_MODEL, t), BF16),
                   jax.ShapeDtypeStruct((2 * PEER_HEADS, N_KEYS, t), F32)),
        grid=(t // tm,),
        in_specs=[tok(D_MODEL)] + grp(ATTN_GROUP_WIDTH) + grp(LANES) + [tok(FOURIER_WIDTH),
                  tok(D_MODEL), tok(D_MODEL),
                  _const_spec(wat.shape), _const_spec(wfo.shape), _const_spec(wout.shape),
                  _const_spec((1, D_MODEL)), _const_spec(wqy.shape), _const_spec(sk.shape),
                  _const_spec(head_expand.shape)],
        out_specs=(tok(D_MODEL), pl.BlockSpec((D_MODEL, tm), lambda i: (0, i)),
                   pl.BlockSpec((2 * PEER_HEADS, N_KEYS, tm), lambda i: (0, 0, i))),
        scratch_shapes=[pltpu.VMEM((n_staged, HEADS_PER_GROUP, tm, HEAD_DIM), F32),
                        pltpu.VMEM((n_staged, 1, tm, LANES), F32)],
        compiler_params=_cparams(("arbitrary",)),
        name="mix",
        interpret=interpret,
    )(x2, *o_list, *ld_list, yf, sga, sgb, wat, wfo, wout, g_ffn.reshape(1, D_MODEL), wqy, sk,
      head_expand)


N_PAIRS = N_KEYS // 2
_PAIR_ORDER = tuple(range(0, N_KEYS, 2)) + tuple(range(1, N_KEYS, 2))
SUBLANES = 8


def _batcher_network(n):
    def merge(lo, hi, r):
        step = r * 2
        if step < hi - lo:
            yield from merge(lo, hi, step)
            yield from merge(lo + r, hi, step)
            for i in range(lo + r, hi - r, step):
                yield (i, i + r)
        else:
            yield (lo, lo + r)

    def sort(lo, hi):
        if hi - lo >= 1:
            mid = lo + (hi - lo) // 2
            yield from sort(lo, mid)
            yield from sort(mid + 1, hi)
            yield from merge(lo, hi, 1)

    return tuple(sort(0, n - 1))


_SORT16 = _batcher_network(PEER_TOPK)


def _compare_exchange(v, i, j):
    v[i], v[j] = jnp.maximum(v[i], v[j]), jnp.minimum(v[i], v[j])


def _top16_sorted(keys):
    n = PEER_TOPK
    v = [keys[SUBLANES * k:SUBLANES * (k + 1)] for k in range(n)]
    for i, j in _SORT16:
        _compare_exchange(v, i, j)
    shift = SUBLANES // 2
    while shift >= 1:
        other = [pltpu.roll(x, shift, 0) for x in v]
        v = [jnp.maximum(v[k], other[n - 1 - k]) for k in range(n)]
        d = n // 2
        while d >= 1:
            for i in range(n):
                if not i & d:
                    _compare_exchange(v, i, i + d)
            d //= 2
        shift //= 2
    return [x[0:1] for x in v]


_CAND_PAIRS = tuple((i, j) for i in range(PEER_TOPK) for j in range(PEER_TOPK)
                    if (i + 1) * (j + 1) <= PEER_TOPK)
_CAND_ROWS = -(-len(_CAND_PAIRS) // SUBLANES) * SUBLANES
TIE_STEP = 2.0 ** -17


def _kth_slot_value(vals, k):
    slot = lax.broadcasted_iota(jnp.int32, vals.shape, 0).astype(F32)
    bumped = vals * (1.0 + slot * TIE_STEP)
    cur = bumped
    m = None
    for _ in range(k):
        m = jnp.max(cur, axis=0, keepdims=True)
        cur = jnp.where(cur == m, -1.0, cur)
    return bumped, m


def _candidates(fa, fb):
    rows = [fa[i] * fb[j] for i, j in _CAND_PAIRS]
    rows.append(jnp.zeros((_CAND_ROWS - len(rows),) + rows[0].shape[1:], rows[0].dtype))
    return jnp.concatenate(rows, axis=0)


def _topk_kernel(st_ref, e1_ref, e2_ref, th_ref):
    tm = st_ref.shape[2]
    for h in range(PEER_HEADS):
        for tc in range(tm // LANES):
            lanes = slice(tc * LANES, (tc + 1) * LANES)
            s1 = st_ref[2 * h, :, lanes]
            s2 = st_ref[2 * h + 1, :, lanes]
            a = _top16_sorted(s1)
            b = _top16_sorted(s2)
            ea = [jnp.exp(ai - a[0]) for ai in a]
            eb = [jnp.exp(bi - b[0]) for bi in b]
            cand = _candidates(ea, eb)
            bumped, kth = _kth_slot_value(cand, PEER_TOPK)
            z = jnp.sum(jnp.where(bumped >= kth, cand, 0.0), axis=0, keepdims=True)
            inv_z = 1.0 / z
            cand_b = _candidates([(x * inv_z).astype(BF16) for x in ea],
                                 [x.astype(BF16) for x in eb]).astype(F32)
            _, kth_b = _kth_slot_value(cand_b, PEER_TOPK)
            th_ref[h:h + 1, lanes] = kth_b.astype(BF16).astype(F32)
            e1 = jnp.exp(s1 - a[0]) * inv_z
            e1_ref[:, 2 * h, lanes] = e1[:N_PAIRS]
            e1_ref[:, 2 * h + 1, lanes] = e1[N_PAIRS:]
            e2_ref[h, :, lanes] = jnp.exp(s2 - b[0]).astype(BF16)


def _topk(st, interpret):
    t = st.shape[2]
    tm = TM_TOPK
    blk = lambda n: pl.BlockSpec((n, N_KEYS, tm), lambda i: (0, 0, i))
    return pl.pallas_call(
        _topk_kernel,
        out_shape=(jax.ShapeDtypeStruct((N_PAIRS, 2 * PEER_HEADS, t), F32),
                   jax.ShapeDtypeStruct((PEER_HEADS, N_KEYS, t), BF16),
                   jax.ShapeDtypeStruct((PEER_HEADS, t), F32)),
        grid=(t // tm,),
        in_specs=[blk(2 * PEER_HEADS)],
        out_specs=(pl.BlockSpec((N_PAIRS, 2 * PEER_HEADS, tm), lambda i: (0, 0, i)),
                   blk(PEER_HEADS),
                   pl.BlockSpec((PEER_HEADS, tm), lambda i: (0, i))),
        compiler_params=_cparams(("arbitrary",)),
        name="peer_topk",
        interpret=interpret,
    )(st)


SB_PEER = 2 * N_KEYS
KB_PEER = 32
SL_PEER = 4 * SB_PEER


def _peer_kernel(hnt_ref, u_ref, vt_ref, e1_ref, e2_ref, th_ref, h_ref, y_ref,
                 at0_ref, at1_ref, ht0_ref, ht1_ref, acc_ref, join_ref, *, nj):
    s = pl.program_id(0)
    tm = hnt_ref.shape[1]
    j_out = lax.rem(jnp.maximum(s - 2, 0), nj)

    @pl.when(s == 0)
    def _():
        for ref in (at0_ref, at1_ref, ht0_ref, ht1_ref):
            ref[...] = jnp.zeros_like(ref)

    @pl.when(j_out == 0)
    def _():
        acc_ref[...] = jnp.zeros_like(acc_ref)

    def stages(at_w, at_r, ht_w, ht_r):
        def slice_body(q, carry):
            r0 = pl.multiple_of(q * SL_PEER, SL_PEER)
            at_new = jnp.dot(u_ref[pl.ds(r0, SL_PEER), :], hnt_ref[...],
                             preferred_element_type=F32)
            at_w[pl.ds(r0, SL_PEER), :] = at_new
            join = at_new[SL_PEER - 8:, tm - LANES:]
            for kp in range(SL_PEER // SB_PEER):
                key_pair = q * (SL_PEER // SB_PEER) + kp
                for par in range(2):
                    for tc in range(tm // LANES):
                        lanes = slice(tc * LANES, (tc + 1) * LANES)
                        e1b = [jnp.broadcast_to(
                            e1_ref[key_pair, 2 * h + par:2 * h + par + 1, lanes].astype(BF16),
                            (KB_PEER, LANES)) for h in range(PEER_HEADS)]
                        thb = [jnp.broadcast_to(th_ref[h:h + 1, lanes].astype(BF16),
                                                (KB_PEER, LANES)) for h in range(PEER_HEADS)]
                        for kb in range(N_KEYS // KB_PEER):
                            keys = slice(kb * KB_PEER, (kb + 1) * KB_PEER)
                            ra = pl.multiple_of(
                                r0 + kp * SB_PEER + par * N_KEYS + kb * KB_PEER, KB_PEER)
                            gate = jnp.zeros((KB_PEER, LANES), BF16)
                            for h in range(PEER_HEADS):
                                p = e1b[h] * e2_ref[h, keys, lanes]
                                gate = gate + jnp.where(p >= thb[h], p, jnp.zeros_like(p))
                            a = at_r[pl.ds(ra, KB_PEER), lanes].astype(BF16)
                            act = jax.nn.gelu(a) * gate
                            ht_w[pl.ds(ra, KB_PEER), lanes] = act
                            join = join + act[:8].astype(F32)
            out_new = jnp.dot(vt_ref[pl.ds(r0, SL_PEER), :], ht_r[...],
                              preferred_element_type=F32)
            acc_ref[pl.ds(r0, SL_PEER), :] += out_new
            return carry + join + out_new[SL_PEER - 8:, tm - LANES:]

        jv = lax.fori_loop(0, u_ref.shape[0] // SL_PEER, slice_body, jnp.zeros((8, LANES), F32))
        join_ref[...] = jv

    parity = lax.rem(s, 2)

    @pl.when(parity == 0)
    def _():
        stages(at0_ref, at1_ref, ht1_ref, ht0_ref)

    @pl.when(parity == 1)
    def _():
        stages(at1_ref, at0_ref, ht0_ref, ht1_ref)

    @pl.when((j_out == nj - 1) & (s >= 2))
    def _():
        y_ref[...] = h_ref[...] + acc_ref[...].T


def _expert_table_kernel(u_ref, v_ref, ub_ref, vt_ref):
    ub_ref[...] = u_ref[...].astype(BF16)
    vt_ref[...] = v_ref[...].T.astype(BF16)


def _expert_tables(expert_u, expert_v, layer, eb, interpret):
    nj = N_EXPERTS // eb
    blk = pl.BlockSpec((eb, D_MODEL), lambda j: (j, 0))
    layer_blk = pl.BlockSpec((None, eb, D_MODEL), lambda j: (layer, j, 0))
    return pl.pallas_call(
        _expert_table_kernel,
        out_shape=(jax.ShapeDtypeStruct((N_EXPERTS, D_MODEL), BF16),
                   jax.ShapeDtypeStruct((nj, D_MODEL, eb), BF16)),
        grid=(nj,),
        in_specs=[layer_blk, layer_blk],
        out_specs=(blk, pl.BlockSpec((None, D_MODEL, eb), lambda j: (j, 0, 0))),
        compiler_params=_cparams(("arbitrary",)),
        name="expert_tables",
        interpret=interpret,
    )(expert_u, expert_v)


def _peer(hnt, h, e1, e2, th, u, vt, interpret):
    t = hnt.shape[1]
    tm, eb = TM_PEER, EB_PEER
    assert eb == D_MODEL
    nj = N_EXPERTS // eb
    n_pairs = (t // tm) * nj
    tile = lambda s: s // nj
    blk = lambda s: lax.rem(s, nj)
    s1 = lambda s: jnp.minimum(s, n_pairs - 1)
    s2 = lambda s: jnp.clip(s - 1, 0, n_pairs - 1)
    s3 = lambda s: jnp.maximum(s - 2, 0)
    return pl.pallas_call(
        functools.partial(_peer_kernel, nj=nj),
        out_shape=jax.ShapeDtypeStruct((t, D_MODEL), F32),
        grid=(n_pairs + 2,),
        in_specs=[pl.BlockSpec((D_MODEL, tm), lambda s: (0, tile(s1(s)))),
                  pl.BlockSpec((eb, D_MODEL), lambda s: (blk(s1(s)), 0)),
                  pl.BlockSpec((None, D_MODEL, eb), lambda s: (blk(s3(s)), 0, 0)),
                  pl.BlockSpec((eb // SB_PEER, 2 * PEER_HEADS, tm),
                               lambda s: (blk(s2(s)), 0, tile(s2(s)))),
                  pl.BlockSpec((PEER_HEADS, N_KEYS, tm), lambda s: (0, 0, tile(s2(s)))),
                  pl.BlockSpec((PEER_HEADS, tm), lambda s: (0, tile(s2(s)))),
                  pl.BlockSpec((tm, D_MODEL), lambda s: (tile(s3(s)), 0))],
        out_specs=pl.BlockSpec((tm, D_MODEL), lambda s: (tile(s3(s)), 0)),
        scratch_shapes=[pltpu.VMEM((eb, tm), F32), pltpu.VMEM((eb, tm), F32),
                        pltpu.VMEM((eb, tm), BF16), pltpu.VMEM((eb, tm), BF16),
                        pltpu.VMEM((D_MODEL, tm), F32), pltpu.VMEM((8, LANES), F32)],
        compiler_params=_cparams(("arbitrary",)),
        name="peer_experts",
        interpret=interpret,
    )(hnt, u, vt, e1, e2, th, h)


def _hybrid_block(x, g_mix, w_in, q_gain, k_gain, w_fourier, w_attn, w_out, g_ffn, w_query,
                  sub_keys, expert_u_b, expert_vt_b, interpret=False):
    b, s, _ = x.shape
    x2 = x.reshape(b * s, D_MODEL)
    fa, fb, q, k, v, sga, sgb = _proj(x2, s, g_mix, w_in, q_gain, k_gain, interpret)
    o_list, ld_list = [], []
    for gi, (_, dilation) in enumerate(ATTN_GROUPS):
        o, ld = _attn_group(q[gi], k[gi], v[gi], dilation, interpret)
        o_list.append(o)
        ld_list.append(ld)
    yf = _seq_dft(fa, fb, b, s, interpret)
    h, hnt, st = _mix(x2, o_list, ld_list, yf, sga, sgb, w_attn, w_fourier, w_out, g_ffn,
                      w_query, sub_keys, interpret)
    e1, e2, th = _topk(st, interpret)
    y = _peer(hnt, h, e1, e2, th, expert_u_b, expert_vt_b, interpret)
    return y.reshape(b, s, D_MODEL)


def kernel(x_prompt, x_sample, g_mix, w_in, q_gain, k_gain, w_fourier, w_attn, w_out, g_ffn,
           w_query, sub_keys, expert_u, expert_v):
    y_prompt, y_sample = x_prompt, x_sample
    for layer in range(g_mix.shape[0]):
        params = (g_mix[layer], w_in[layer], q_gain[layer], k_gain[layer], w_fourier[layer],
                  w_attn[layer], w_out[layer], g_ffn[layer], w_query[layer], sub_keys[layer],
                  *_expert_tables(expert_u, expert_v, layer, EB_PEER, False))
        y_prompt = _hybrid_block(y_prompt, *params)
        y_sample = _hybrid_block(y_sample, *params)
    return (y_prompt, y_sample)
```

```python
import functools
import math

import jax
import jax.numpy as jnp
from jax import lax
from jax.experimental import pallas as pl
from jax.experimental.pallas import tpu as pltpu

F32 = jnp.float32
BF16 = jnp.bfloat16

D_MODEL = 1024
FOURIER_GROUPS = 4
FOURIER_GROUP_DIM = 128
FOURIER_WIDTH = FOURIER_GROUPS * FOURIER_GROUP_DIM
ATTN_GROUPS = ((128, 1), (512, 4), (2048, 16))
N_ATTN_GROUPS = 3
HEADS_PER_GROUP = 4
HEAD_DIM = 128
ATTN_GROUP_WIDTH = HEADS_PER_GROUP * HEAD_DIM
ATTN_WIDTH = N_ATTN_GROUPS * ATTN_GROUP_WIDTH
ROPE_DIM = HEAD_DIM // 4
ROPE_THETA = 500000.0
N_KEYS = 128
N_EXPERTS = N_KEYS * N_KEYS
PEER_HEADS = 8
PEER_TOPK = 16
PEER_KEY_DIM = 256
NORM_EPS = 1e-6
NEG_INF = -1e30

LANES = 128
VMEM_LIMIT = 56 * 1024 * 1024

TM_PROJ = 512
TM_MIX = 512
TM_TOPK = 512
TM_PEER = 512
EB_PEER = 1024
QB_ATTN = 128
DFT_TS = 1024
DFT_TK = 2048


def _cparams(sem):
    return pltpu.CompilerParams(dimension_semantics=sem, vmem_limit_bytes=VMEM_LIMIT)


def _const_spec(shape):
    nd = len(shape)
    return pl.BlockSpec(shape, lambda *_: (0,) * nd)


def _proj_kernel(x_ref, g_ref, wf_ref, wq_ref, wk_ref, wv_ref, wga_ref, wgb_ref,
                 qg_ref, kg_ref, cc_ref, sc_ref, rc_ref, rs1_ref, rs2_ref,
                 fa_ref, fb_ref, q0_ref, q1_ref, q2_ref, k0_ref, k1_ref, k2_ref,
                 v0_ref, v1_ref, v2_ref, sga_ref, sgb_ref, stage_ref):
    tm = x_ref.shape[0]
    hpg = HEADS_PER_GROUP
    stage_slot = [0]

    def store_group(out_ref, gi, cols):
        dilation = ATTN_GROUPS[gi][1]
        if dilation == 1:
            for hl, c in enumerate(cols):
                out_ref[0, :, hl * HEAD_DIM:(hl + 1) * HEAD_DIM] = c.astype(BF16)
            return
        slot = stage_slot[0]
        stage_slot[0] += 1
        for hl, c in enumerate(cols):
            stage_ref[slot, hl] = c
            for r in range(dilation):
                out_ref[r, :, hl * HEAD_DIM:(hl + 1) * HEAD_DIM] = stage_ref[
                    slot, hl, pl.ds(r, tm // dilation, stride=dilation), :].astype(BF16)

    x = x_ref[...]
    ms = jnp.mean(x * x, axis=-1, keepdims=True)
    xn = (x * lax.rsqrt(ms + NORM_EPS) * g_ref[...]).astype(BF16)

    f = jnp.dot(xn, wf_ref[...], preferred_element_type=F32).astype(BF16)
    fa_ref[...] = jnp.dot(f, cc_ref[...], preferred_element_type=F32).astype(BF16)
    fb_ref[...] = jnp.dot(f, sc_ref[...], preferred_element_type=F32).astype(BF16)

    rc = rc_ref[...]
    rs1 = rs1_ref[...]
    rs2 = rs2_ref[...]

    def norm_rope(w_ref, gain_ref, out_refs, scale):
        y = jnp.dot(xn, w_ref[...], preferred_element_type=F32)
        for gi in range(N_ATTN_GROUPS):
            cols = []
            for hh in range(gi * hpg, (gi + 1) * hpg):
                yh = y[:, hh * HEAD_DIM:(hh + 1) * HEAD_DIM]
                hms = jnp.mean(yh * yh, axis=-1, keepdims=True)
                yn = yh * lax.rsqrt(hms + NORM_EPS) * gain_ref[gi:gi + 1, :]
                r = (yn * rc + pltpu.roll(yn, LANES - ROPE_DIM // 2, 1) * rs1
                     + pltpu.roll(yn, ROPE_DIM // 2, 1) * rs2)
                cols.append(r * scale)
            store_group(out_refs[gi], gi, cols)

    norm_rope(wq_ref, qg_ref, (q0_ref, q1_ref, q2_ref), HEAD_DIM ** -0.5)
    norm_rope(wk_ref, kg_ref, (k0_ref, k1_ref, k2_ref), 1.0)
    yv = jnp.dot(xn, wv_ref[...], preferred_element_type=F32)
    for gi, out_ref in enumerate((v0_ref, v1_ref, v2_ref)):
        store_group(out_ref, gi, [yv[:, hh * HEAD_DIM:(hh + 1) * HEAD_DIM]
                                  for hh in range(gi * hpg, (gi + 1) * hpg)])
    sga_ref[...] = jax.nn.sigmoid(
        jnp.dot(xn, wga_ref[...], preferred_element_type=F32)).astype(BF16)
    sgb_ref[...] = jax.nn.sigmoid(
        jnp.dot(xn, wgb_ref[...], preferred_element_type=F32)).astype(BF16)


def _rope_tables(s):
    half = ROPE_DIM // 2
    inv_freq = ROPE_THETA ** (-jnp.arange(half, dtype=F32) * 2.0 / ROPE_DIM)
    ang = jnp.arange(s, dtype=F32)[:, None] * inv_freq[None, :]
    cos, sin = jnp.cos(ang), jnp.sin(ang)
    zeros = jnp.zeros((s, HEAD_DIM - ROPE_DIM), F32)
    zh = jnp.zeros((s, half), F32)
    rc = jnp.concatenate([cos, cos, jnp.ones((s, HEAD_DIM - ROPE_DIM), F32)], axis=1)
    rs1 = jnp.concatenate([-sin, zh, zeros], axis=1)
    rs2 = jnp.concatenate([zh, sin, zeros], axis=1)
    return rc, rs1, rs2


def _channel_dft_mats():
    n = FOURIER_GROUP_DIM
    idx = (jnp.arange(n)[:, None] * jnp.arange(n)[None, :]) % n
    ang = idx.astype(F32) * (2.0 * math.pi / n)
    eye = jnp.eye(FOURIER_GROUPS, dtype=F32)
    cc = jnp.kron(eye, jnp.cos(ang) * n ** -0.5)
    sc = jnp.kron(eye, jnp.sin(ang) * n ** -0.5)
    return cc.astype(BF16), sc.astype(BF16)


def _proj(x2, s, g_mix, w_in, q_gain, k_gain, interpret):
    t = x2.shape[0]
    tm = TM_PROJ
    c0 = FOURIER_WIDTH
    c1 = c0 + ATTN_WIDTH
    c2 = c1 + ATTN_WIDTH
    c3 = c2 + ATTN_WIDTH
    c4 = c3 + D_MODEL
    wb = w_in.astype(BF16)
    wf, wq, wk, wv, wga, wgb = (wb[:, :c0], wb[:, c0:c1], wb[:, c1:c2], wb[:, c2:c3],
                                wb[:, c3:c4], wb[:, c4:])
    cc, sc = _channel_dft_mats()
    rc, rs1, rs2 = _rope_tables(s)
    spt = s // tm
    b = t // s
    tok = lambda w: pl.BlockSpec((tm, w), lambda i: (i, 0))
    pos = pl.BlockSpec((tm, HEAD_DIM), lambda i: (i % spt, 0))
    grp_shapes = [jax.ShapeDtypeStruct((b, d, s // d, ATTN_GROUP_WIDTH), BF16)
                  for _, d in ATTN_GROUPS]
    grp_specs = [pl.BlockSpec((None, d, tm // d, ATTN_GROUP_WIDTH),
                              lambda i: (i // spt, 0, i % spt, 0)) for _, d in ATTN_GROUPS]
    out_shape = ([jax.ShapeDtypeStruct((t, FOURIER_WIDTH), BF16)] * 2 + grp_shapes * 3
                 + [jax.ShapeDtypeStruct((t, D_MODEL), BF16)] * 2)
    n_staged = 3 * sum(1 for _, d in ATTN_GROUPS if d > 1)
    outs = pl.pallas_call(
        _proj_kernel,
        out_shape=out_shape,
        grid=(t // tm,),
        in_specs=[tok(D_MODEL), _const_spec((1, D_MODEL)),
                  _const_spec(wf.shape), _const_spec(wq.shape), _const_spec(wk.shape),
                  _const_spec(wv.shape), _const_spec(wga.shape), _const_spec(wgb.shape),
                  _const_spec(q_gain.shape), _const_spec(k_gain.shape),
                  _const_spec(cc.shape), _const_spec(sc.shape), pos, pos, pos],
        out_specs=[tok(FOURIER_WIDTH)] * 2 + grp_specs * 3 + [tok(D_MODEL)] * 2,
        scratch_shapes=[pltpu.VMEM((n_staged, HEADS_PER_GROUP, tm, HEAD_DIM), F32)],
        compiler_params=_cparams(("arbitrary",)),
        name="proj",
        interpret=interpret,
    )(x2, g_mix.reshape(1, D_MODEL), wf, wq, wk, wv, wga, wgb, q_gain, k_gain, cc, sc,
      rc, rs1, rs2)
    fa, fb = outs[0], outs[1]
    q, k, v = outs[2:5], outs[5:8], outs[8:11]
    return fa, fb, q, k, v, outs[11], outs[12]


LD_LANES = LANES // HEADS_PER_GROUP


def _attn_kernel(q_ref, k_ref, v_ref, o_ref, ld_ref, *, sub_len, heads):
    kb = min(2 * QB_ATTN, sub_len)
    span = QB_ATTN // 2
    first_head = pl.program_id(2) * heads

    if heads < HEADS_PER_GROUP:
        @pl.when(first_head == 0)
        def _():
            ld_ref[...] = jnp.zeros_like(ld_ref)

    def body(i, carry):
        start = pl.multiple_of(i * QB_ATTN, QB_ATTN)
        ks = pl.multiple_of(jnp.clip(start - span, 0, sub_len - kb), span)
        qpos = start + lax.broadcasted_iota(jnp.int32, (QB_ATTN, kb), 0)
        kpos = ks + lax.broadcasted_iota(jnp.int32, (QB_ATTN, kb), 1)
        in_band = jnp.abs(kpos - qpos) <= span
        lane_head = lax.broadcasted_iota(jnp.int32, (QB_ATTN, LANES), 1) // LD_LANES
        if heads < HEADS_PER_GROUP:
            ld_all = ld_ref[pl.ds(start, QB_ATTN), :]
        else:
            ld_all = jnp.zeros((QB_ATTN, LANES), F32)
        for hl in range(heads):
            cols = slice(hl * HEAD_DIM, (hl + 1) * HEAD_DIM)
            q = q_ref[pl.ds(start, QB_ATTN), cols]
            k = k_ref[pl.ds(ks, kb), cols]
            v = v_ref[pl.ds(ks, kb), cols]
            s = lax.dot_general(q, k, (((1,), (1,)), ((), ())), preferred_element_type=F32)
            s = jnp.where(in_band, s, NEG_INF)
            m = jnp.max(s, axis=1, keepdims=True)
            p = jnp.exp(s - m)
            l = jnp.sum(p, axis=1, keepdims=True)
            o = jnp.dot(p.astype(BF16), v, preferred_element_type=F32) / l
            o_ref[pl.ds(start, QB_ATTN), cols] = o.astype(BF16)
            ld_all = jnp.where(lane_head == first_head + hl, m + jnp.log(l), ld_all)
        ld_ref[pl.ds(start, QB_ATTN), :] = ld_all
        return carry

    nq = sub_len // QB_ATTN
    lax.fori_loop(0, nq, body, 0, unroll=min(nq, 2 * HEADS_PER_GROUP // heads))


ATTN_MAX_WIDE_LEN = 2048


def _attn_group(q, k, v, dilation, interpret):
    b, _, sub_len, _ = q.shape
    assert sub_len % QB_ATTN == 0
    heads = HEADS_PER_GROUP if sub_len <= ATTN_MAX_WIDE_LEN else 1
    width = heads * HEAD_DIM
    spec = pl.BlockSpec((None, None, sub_len, width), lambda bi, r, hb: (bi, r, 0, hb))
    return pl.pallas_call(
        functools.partial(_attn_kernel, sub_len=sub_len, heads=heads),
        out_shape=(jax.ShapeDtypeStruct(q.shape, BF16),
                   jax.ShapeDtypeStruct((b, dilation, sub_len, LANES), F32)),
        grid=(b, dilation, HEADS_PER_GROUP // heads),
        in_specs=[spec, spec, spec],
        out_specs=(spec, pl.BlockSpec((None, None, sub_len, LANES),
                                      lambda bi, r, hb: (bi, r, 0, 0))),
        compiler_params=_cparams(("arbitrary", "arbitrary", "arbitrary")),
        name=f"attn_d{dilation}",
        interpret=interpret,
    )(q, k, v)


def _dft_kernel(cs_ref, ss_ref, a_ref, b_ref, y_ref, acc_ref):
    kk = pl.program_id(2)

    @pl.when(kk == 0)
    def _():
        acc_ref[...] = jnp.zeros_like(acc_ref)

    acc_ref[...] += (jnp.dot(cs_ref[...], a_ref[...], preferred_element_type=F32)
                     - jnp.dot(ss_ref[...], b_ref[...], preferred_element_type=F32))

    @pl.when(kk == pl.num_programs(2) - 1)
    def _():
        y_ref[...] = acc_ref[...].astype(BF16)


def _dft_mats_kernel(ch_ref, sh_ref, cl_ref, sl_ref, cs_ref, ss_ref):
    cl, sl = cl_ref[...], sl_ref[...]
    for k in range(ch_ref.shape[1]):
        c, s_ = ch_ref[:, k:k + 1], sh_ref[:, k:k + 1]
        cols = slice(k * LANES, (k + 1) * LANES)
        cs_ref[:, cols] = (c * cl - s_ * sl).astype(BF16)
        ss_ref[:, cols] = (s_ * cl + c * sl).astype(BF16)


DFT_GEN_ROWS = 256


def _seq_dft_mats(s, interpret):
    lo = LANES
    hi = s // lo
    rows = jnp.arange(s, dtype=jnp.int32)[:, None]
    scale = 2.0 * math.pi / s
    ang_hi = ((rows * (jnp.arange(hi, dtype=jnp.int32)[None, :] * lo)) % s).astype(F32) * scale
    ang_lo = ((rows * jnp.arange(lo, dtype=jnp.int32)[None, :]) % s).astype(F32) * scale
    norm = s ** -0.5
    tr = DFT_GEN_ROWS
    coarse = pl.BlockSpec((tr, hi), lambda i: (i, 0))
    fine = pl.BlockSpec((tr, lo), lambda i: (i, 0))
    out = pl.BlockSpec((tr, s), lambda i: (i, 0))
    return pl.pallas_call(
        _dft_mats_kernel,
        out_shape=(jax.ShapeDtypeStruct((s, s), BF16),) * 2,
        grid=(s // tr,),
        in_specs=[coarse, coarse, fine, fine],
        out_specs=(out, out),
        compiler_params=_cparams(("arbitrary",)),
        name="dft_mats",
        interpret=interpret,
    )(jnp.cos(ang_hi), jnp.sin(ang_hi), jnp.cos(ang_lo) * norm, jnp.sin(ang_lo) * norm)


def _seq_dft(fa, fb, b, s, interpret):
    ts = min(DFT_TS, s)
    tk = min(DFT_TK, s)
    cs, ss = _seq_dft_mats(s, interpret)
    mat = pl.BlockSpec((ts, tk), lambda bi, i, kk: (i, kk))
    vec = pl.BlockSpec((None, tk, FOURIER_WIDTH), lambda bi, i, kk: (bi, kk, 0))
    y = pl.pallas_call(
        _dft_kernel,
        out_shape=jax.ShapeDtypeStruct((b, s, FOURIER_WIDTH), BF16),
        grid=(b, s // ts, s // tk),
        in_specs=[mat, mat, vec, vec],
        out_specs=pl.BlockSpec((None, ts, FOURIER_WIDTH), lambda bi, i, kk: (bi, i, 0)),
        scratch_shapes=[pltpu.VMEM((ts, FOURIER_WIDTH), F32)],
        compiler_params=_cparams(("arbitrary", "arbitrary", "arbitrary")),
        name="seq_dft",
        interpret=interpret,
    )(cs, ss, fa.reshape(b, s, FOURIER_WIDTH), fb.reshape(b, s, FOURIER_WIDTH))
    return y.reshape(b * s, FOURIER_WIDTH)


def _mix_kernel(x_ref, o0_ref, o1_ref, o2_ref, l0_ref, l1_ref, l2_ref, y_ref, sga_ref, sgb_ref,
                wat_ref, wfo_ref, wout_ref, gf_ref, wqy_ref, sk_ref, hx_ref,
                h_ref, hnt_ref, st_ref, ostage_ref, lstage_ref):
    tm = x_ref.shape[0]

    def token_order(ref, stage, gi):
        dilation = ATTN_GROUPS[gi][1]
        if dilation == 1:
            return ref[0].astype(F32)
        cols = []
        for c in range(ref.shape[-1] // LANES):
            for r in range(dilation):
                stage[gi - 1, c, pl.ds(r, tm // dilation, stride=dilation), :] = (
                    ref[r, :, c * LANES:(c + 1) * LANES].astype(F32))
            cols.append(stage[gi - 1, c])
        return cols[0] if len(cols) == 1 else jnp.concatenate(cols, axis=1)

    lds = [token_order(ref, lstage_ref, gi) for gi, ref in enumerate((l0_ref, l1_ref, l2_ref))]
    mx = jnp.maximum(jnp.maximum(lds[0], lds[1]), lds[2])
    ws = [jnp.exp(ld - mx) for ld in lds]
    inv = 1.0 / (ws[0] + ws[1] + ws[2])
    attn = jnp.zeros((tm, ATTN_GROUP_WIDTH), F32)
    for gi, ref in enumerate((o0_ref, o1_ref, o2_ref)):
        wexp = jnp.dot((ws[gi] * inv).astype(BF16), hx_ref[...], preferred_element_type=F32)
        attn = attn + wexp * token_order(ref, ostage_ref, gi)
    attn = jnp.dot(attn.astype(BF16), wat_ref[...], preferred_element_type=F32)
    four = jnp.dot(y_ref[...], wfo_ref[...], preferred_element_type=F32)
    mix = sga_ref[...].astype(F32) * four + sgb_ref[...].astype(F32) * attn
    h = x_ref[...] + jnp.dot(mix.astype(BF16), wout_ref[...], preferred_element_type=F32)
    h_ref[...] = h
    ms = jnp.mean(h * h, axis=-1, keepdims=True)
    hn32 = h * lax.rsqrt(ms + NORM_EPS) * gf_ref[...]
    hn = hn32.astype(BF16)
    hnt_ref[...] = hn32.T.astype(BF16)
    qy = jnp.dot(hn, wqy_ref[...], preferred_element_type=F32).astype(BF16)
    half = PEER_KEY_DIM // 2
    for hc in range(2 * PEER_HEADS):
        st_ref[hc] = lax.dot_general(sk_ref[hc], qy[:, hc * half:(hc + 1) * half],
                                     (((1,), (1,)), ((), ())), preferred_element_type=F32)


def _mix(x2, o_list, ld_list, yf, sga, sgb, w_attn, w_fourier, w_out, g_ffn, w_query, sub_keys,
         interpret):
    t = x2.shape[0]
    tm = TM_MIX
    spt = o_list[0].shape[2] // tm
    tok = lambda w: pl.BlockSpec((tm, w), lambda i: (i, 0))
    grp = lambda w: [pl.BlockSpec((None, d, tm // d, w), lambda i: (i // spt, 0, i % spt, 0))
                     for _, d in ATTN_GROUPS]
    head_expand = (jnp.arange(LANES)[:, None]
                   == LD_LANES * (jnp.arange(ATTN_GROUP_WIDTH)[None, :] // HEAD_DIM)).astype(BF16)
    n_staged = sum(1 for _, d in ATTN_GROUPS if d > 1)
    sk = jnp.stack([sub_keys[:, 0][:, jnp.array(_PAIR_ORDER)], sub_keys[:, 1]], axis=1)
    sk = sk.reshape(2 * PEER_HEADS, N_KEYS, PEER_KEY_DIM // 2).astype(BF16)
    wat, wfo, wout, wqy = (w_attn.astype(BF16), w_fourier.astype(BF16), w_out.astype(BF16),
                           w_query.astype(BF16))
    return pl.pallas_call(
        _mix_kernel,
        out_shape=(jax.ShapeDtypeStruct((t, D_MODEL), F32),
                   jax.ShapeDtypeStruct((D_MODEL, t), BF16),
                   jax.ShapeDtypeStruct((2 * PEER_HEADS, N_KEYS, t), F32)),
        grid=(t // tm,),
        in_specs=[tok(D_MODEL)] + grp(ATTN_GROUP_WIDTH) + grp(LANES) + [tok(FOURIER_WIDTH),
                  tok(D_MODEL), tok(D_MODEL),
                  _const_spec(wat.shape), _const_spec(wfo.shape), _const_spec(wout.shape),
                  _const_spec((1, D_MODEL)), _const_spec(wqy.shape), _const_spec(sk.shape),
                  _const_spec(head_expand.shape)],
        out_specs=(tok(D_MODEL), pl.BlockSpec((D_MODEL, tm), lambda i: (0, i)),
                   pl.BlockSpec((2 * PEER_HEADS, N_KEYS, tm), lambda i: (0, 0, i))),
        scratch_shapes=[pltpu.VMEM((n_staged, HEADS_PER_GROUP, tm, HEAD_DIM), F32),
                        pltpu.VMEM((n_staged, 1, tm, LANES), F32)],
        compiler_params=_cparams(("arbitrary",)),
        name="mix",
        interpret=interpret,
    )(x2, *o_list, *ld_list, yf, sga, sgb, wat, wfo, wout, g_ffn.reshape(1, D_MODEL), wqy, sk,
      head_expand)


N_PAIRS = N_KEYS // 2
_PAIR_ORDER = tuple(range(0, N_KEYS, 2)) + tuple(range(1, N_KEYS, 2))
SUBLANES = 8


def _batcher_network(n):
    def merge(lo, hi, r):
        step = r * 2
        if step < hi - lo:
            yield from merge(lo, hi, step)
            yield from merge(lo + r, hi, step)
            for i in range(lo + r, hi - r, step):
                yield (i, i + r)
        else:
            yield (lo, lo + r)

    def sort(lo, hi):
        if hi - lo >= 1:
            mid = lo + (hi - lo) // 2
            yield from sort(lo, mid)
            yield from sort(mid + 1, hi)
            yield from merge(lo, hi, 1)

    return tuple(sort(0, n - 1))


_SORT16 = _batcher_network(PEER_TOPK)


def _compare_exchange(v, i, j):
    v[i], v[j] = jnp.maximum(v[i], v[j]), jnp.minimum(v[i], v[j])


def _top16_sorted(keys):
    n = PEER_TOPK
    v = [keys[SUBLANES * k:SUBLANES * (k + 1)] for k in range(n)]
    for i, j in _SORT16:
        _compare_exchange(v, i, j)
    shift = SUBLANES // 2
    while shift >= 1:
        other = [pltpu.roll(x, shift, 0) for x in v]
        v = [jnp.maximum(v[k], other[n - 1 - k]) for k in range(n)]
        d = n // 2
        while d >= 1:
            for i in range(n):
                if not i & d:
                    _compare_exchange(v, i, i + d)
            d //= 2
        shift //= 2
    return [x[0:1] for x in v]


_CAND_PAIRS = tuple((i, j) for i in range(PEER_TOPK) for j in range(PEER_TOPK)
                    if (i + 1) * (j + 1) <= PEER_TOPK)
_CAND_ROWS = -(-len(_CAND_PAIRS) // SUBLANES) * SUBLANES
TIE_STEP = 2.0 ** -17


def _kth_slot_value(vals, k):
    slot = lax.broadcasted_iota(jnp.int32, vals.shape, 0).astype(F32)
    bumped = vals * (1.0 + slot * TIE_STEP)
    cur = bumped
    m = None
    for _ in range(k):
        m = jnp.max(cur, axis=0, keepdims=True)
        cur = jnp.where(cur == m, -1.0, cur)
    return bumped, m


def _candidates(fa, fb):
    rows = [fa[i] * fb[j] for i, j in _CAND_PAIRS]
    rows.append(jnp.zeros((_CAND_ROWS - len(rows),) + rows[0].shape[1:], rows[0].dtype))
    return jnp.concatenate(rows, axis=0)


def _topk_kernel(st_ref, e1_ref, e2_ref, th_ref):
    tm = st_ref.shape[2]
    for h in range(PEER_HEADS):
        for tc in range(tm // LANES):
            lanes = slice(tc * LANES, (tc + 1) * LANES)
            s1 = st_ref[2 * h, :, lanes]
            s2 = st_ref[2 * h + 1, :, lanes]
            a = _top16_sorted(s1)
            b = _top16_sorted(s2)
            ea = [jnp.exp(ai - a[0]) for ai in a]
            eb = [jnp.exp(bi - b[0]) for bi in b]
            cand = _candidates(ea, eb)
            bumped, kth = _kth_slot_value(cand, PEER_TOPK)
            z = jnp.sum(jnp.where(bumped >= kth, cand, 0.0), axis=0, keepdims=True)
            inv_z = 1.0 / z
            cand_b = _candidates([(x * inv_z).astype(BF16) for x in ea],
                                 [x.astype(BF16) for x in eb]).astype(F32)
            _, kth_b = _kth_slot_value(cand_b, PEER_TOPK)
            th_ref[h:h + 1, lanes] = kth_b.astype(BF16).astype(F32)
            e1 = jnp.exp(s1 - a[0]) * inv_z
            e1_ref[:, 2 * h, lanes] = e1[:N_PAIRS]
            e1_ref[:, 2 * h + 1, lanes] = e1[N_PAIRS:]
            e2_ref[h, :, lanes] = jnp.exp(s2 - b[0]).astype(BF16)


def _topk(st, interpret):
    t = st.shape[2]
    tm = TM_TOPK
    blk = lambda n: pl.BlockSpec((n, N_KEYS, tm), lambda i: (0, 0, i))
    return pl.pallas_call(
        _topk_kernel,
        out_shape=(jax.ShapeDtypeStruct((N_PAIRS, 2 * PEER_HEADS, t), F32),
                   jax.ShapeDtypeStruct((PEER_HEADS, N_KEYS, t), BF16),
                   jax.ShapeDtypeStruct((PEER_HEADS, t), F32)),
        grid=(t // tm,),
        in_specs=[blk(2 * PEER_HEADS)],
        out_specs=(pl.BlockSpec((N_PAIRS, 2 * PEER_HEADS, tm), lambda i: (0, 0, i)),
                   blk(PEER_HEADS),
                   pl.BlockSpec((PEER_HEADS, tm), lambda i: (0, i))),
        compiler_params=_cparams(("arbitrary",)),
        name="peer_topk",
        interpret=interpret,
    )(st)


SB_PEER = 2 * N_KEYS
KB_PEER = 32
SL_PEER = 4 * SB_PEER


def _peer_kernel(hnt_ref, u_ref, vt_ref, e1_ref, e2_ref, th_ref, h_ref, y_ref,
                 at0_ref, at1_ref, ht0_ref, ht1_ref, acc_ref, join_ref, *, nj):
    s = pl.program_id(0)
    tm = hnt_ref.shape[1]
    j_out = lax.rem(jnp.maximum(s - 2, 0), nj)

    @pl.when(s == 0)
    def _():
        for ref in (at0_ref, at1_ref, ht0_ref, ht1_ref):
            ref[...] = jnp.zeros_like(ref)

    @pl.when(j_out == 0)
    def _():
        acc_ref[...] = jnp.zeros_like(acc_ref)

    def stages(at_w, at_r, ht_w, ht_r):
        def slice_body(q, carry):
            r0 = pl.multiple_of(q * SL_PEER, SL_PEER)
            at_new = jnp.dot(u_ref[pl.ds(r0, SL_PEER), :], hnt_ref[...],
                             preferred_element_type=F32)
            at_w[pl.ds(r0, SL_PEER), :] = at_new
            join = at_new[SL_PEER - 8:, tm - LANES:]
            prev_zero = jnp.zeros((KB_PEER, LANES), BF16)
            for kp in range(SL_PEER // SB_PEER):
                key_pair = q * (SL_PEER // SB_PEER) + kp
                for par in range(2):
                    for tc in range(tm // LANES):
                        lanes = slice(tc * LANES, (tc + 1) * LANES)
                        e1b = [jnp.broadcast_to(
                            e1_ref[key_pair, 2 * h + par:2 * h + par + 1, lanes].astype(BF16),
                            (KB_PEER, LANES)) for h in range(PEER_HEADS)]
                        thb = [jnp.broadcast_to(th_ref[h:h + 1, lanes].astype(BF16),
                                                (KB_PEER, LANES)) for h in range(PEER_HEADS)]
                        for kb in range(N_KEYS // KB_PEER):
                            keys = slice(kb * KB_PEER, (kb + 1) * KB_PEER)
                            ra = pl.multiple_of(
                                r0 + kp * SB_PEER + par * N_KEYS + kb * KB_PEER, KB_PEER)
                            gate = prev_zero
                            for h in range(PEER_HEADS):
                                p = e1b[h] * e2_ref[h, keys, lanes]
                                gate = gate + jnp.where(p >= thb[h], p, jnp.zeros_like(p))
                            a = at_r[pl.ds(ra, KB_PEER), lanes].astype(BF16)
                            act = jax.nn.gelu(a) * gate
                            ht_w[pl.ds(ra, KB_PEER), lanes] = act
                            join = join + act[:8].astype(F32)
                            bits = pltpu.bitcast(act, jnp.uint32)
                            prev_zero = pltpu.bitcast((bits >> 16) >> 16, BF16)
            out_new = jnp.dot(vt_ref[pl.ds(r0, SL_PEER), :], ht_r[...],
                              preferred_element_type=F32)
            acc_ref[pl.ds(r0, SL_PEER), :] += out_new
            return carry + join + out_new[SL_PEER - 8:, tm - LANES:]

        jv = lax.fori_loop(0, u_ref.shape[0] // SL_PEER, slice_body, jnp.zeros((8, LANES), F32))
        join_ref[...] = jv

    parity = lax.rem(s, 2)

    @pl.when(parity == 0)
    def _():
        stages(at0_ref, at1_ref, ht1_ref, ht0_ref)

    @pl.when(parity == 1)
    def _():
        stages(at1_ref, at0_ref, ht0_ref, ht1_ref)

    @pl.when((j_out == nj - 1) & (s >= 2))
    def _():
        y_ref[...] = h_ref[...] + acc_ref[...].T


def _expert_table_kernel(u_ref, v_ref, ub_ref, vt_ref):
    ub_ref[...] = u_ref[...].astype(BF16)
    vt_ref[...] = v_ref[...].T.astype(BF16)


def _expert_tables(expert_u, expert_v, layer, eb, interpret):
    nj = N_EXPERTS // eb
    blk = pl.BlockSpec((eb, D_MODEL), lambda j: (j, 0))
    layer_blk = pl.BlockSpec((None, eb, D_MODEL), lambda j: (layer, j, 0))
    return pl.pallas_call(
        _expert_table_kernel,
        out_shape=(jax.ShapeDtypeStruct((N_EXPERTS, D_MODEL), BF16),
                   jax.ShapeDtypeStruct((nj, D_MODEL, eb), BF16)),
        grid=(nj,),
        in_specs=[layer_blk, layer_blk],
        out_specs=(blk, pl.BlockSpec((None, D_MODEL, eb), lambda j: (j, 0, 0))),
        compiler_params=_cparams(("arbitrary",)),
        name="expert_tables",
        interpret=interpret,
    )(expert_u, expert_v)


def _peer(hnt, h, e1, e2, th, u, vt, interpret):
    t = hnt.shape[1]
    tm, eb = TM_PEER, EB_PEER
    assert eb == D_MODEL
    nj = N_EXPERTS // eb
    n_pairs = (t // tm) * nj
    tile = lambda s: s // nj
    blk = lambda s: lax.rem(s, nj)
    s1 = lambda s: jnp.minimum(s, n_pairs - 1)
    s2 = lambda s: jnp.clip(s - 1, 0, n_pairs - 1)
    s3 = lambda s: jnp.maximum(s - 2, 0)
    return pl.pallas_call(
        functools.partial(_peer_kernel, nj=nj),
        out_shape=jax.ShapeDtypeStruct((t, D_MODEL), F32),
        grid=(n_pairs + 2,),
        in_specs=[pl.BlockSpec((D_MODEL, tm), lambda s: (0, tile(s1(s)))),
                  pl.BlockSpec((eb, D_MODEL), lambda s: (blk(s1(s)), 0)),
                  pl.BlockSpec((None, D_MODEL, eb), lambda s: (blk(s3(s)), 0, 0)),
                  pl.BlockSpec((eb // SB_PEER, 2 * PEER_HEADS, tm),
                               lambda s: (blk(s2(s)), 0, tile(s2(s)))),
                  pl.BlockSpec((PEER_HEADS, N_KEYS, tm), lambda s: (0, 0, tile(s2(s)))),
                  pl.BlockSpec((PEER_HEADS, tm), lambda s: (0, tile(s2(s)))),
                  pl.BlockSpec((tm, D_MODEL), lambda s: (tile(s3(s)), 0))],
        out_specs=pl.BlockSpec((tm, D_MODEL), lambda s: (tile(s3(s)), 0)),
        scratch_shapes=[pltpu.VMEM((eb, tm), F32), pltpu.VMEM((eb, tm), F32),
                        pltpu.VMEM((eb, tm), BF16), pltpu.VMEM((eb, tm), BF16),
                        pltpu.VMEM((D_MODEL, tm), F32), pltpu.VMEM((8, LANES), F32)],
        compiler_params=_cparams(("arbitrary",)),
        name="peer_experts",
        interpret=interpret,
    )(hnt, u, vt, e1, e2, th, h)


def _hybrid_block(x, g_mix, w_in, q_gain, k_gain, w_fourier, w_attn, w_out, g_ffn, w_query,
                  sub_keys, expert_u_b, expert_vt_b, interpret=False):
    b, s, _ = x.shape
    x2 = x.reshape(b * s, D_MODEL)
    fa, fb, q, k, v, sga, sgb = _proj(x2, s, g_mix, w_in, q_gain, k_gain, interpret)
    o_list, ld_list = [], []
    for gi, (_, dilation) in enumerate(ATTN_GROUPS):
        o, ld = _attn_group(q[gi], k[gi], v[gi], dilation, interpret)
        o_list.append(o)
        ld_list.append(ld)
    yf = _seq_dft(fa, fb, b, s, interpret)
    h, hnt, st = _mix(x2, o_list, ld_list, yf, sga, sgb, w_attn, w_fourier, w_out, g_ffn,
                      w_query, sub_keys, interpret)
    e1, e2, th = _topk(st, interpret)
    y = _peer(hnt, h, e1, e2, th, expert_u_b, expert_vt_b, interpret)
    return y.reshape(b, s, D_MODEL)


def kernel(x_prompt, x_sample, g_mix, w_in, q_gain, k_gain, w_fourier, w_attn, w_out, g_ffn,
           w_query, sub_keys, expert_u, expert_v):
    y_prompt, y_sample = x_prompt, x_sample
    for layer in range(g_mix.shape[0]):
        params = (g_mix[layer], w_in[layer], q_gain[layer], k_gain[layer], w_fourier[layer],
                  w_attn[layer], w_out[layer], g_ffn[layer], w_query[layer], sub_keys[layer],
                  *_expert_tables(expert_u, expert_v, layer, EB_PEER, False))
        y_prompt = _hybrid_block(y_prompt, *params)
        y_sample = _hybrid_block(y_sample, *params)
    return (y_prompt, y_sample)
```

```python
import functools
import math

import jax
import jax.numpy as jnp
from jax import lax
from jax.experimental import pallas as pl
from jax.experimental.pallas import tpu as pltpu

F32 = jnp.float32
BF16 = jnp.bfloat16

D_MODEL = 1024
FOURIER_GROUPS = 4
FOURIER_GROUP_DIM = 128
FOURIER_WIDTH = FOURIER_GROUPS * FOURIER_GROUP_DIM
ATTN_GROUPS = ((128, 1), (512, 4), (2048, 16))
N_ATTN_GROUPS = 3
HEADS_PER_GROUP = 4
HEAD_DIM = 128
ATTN_GROUP_WIDTH = HEADS_PER_GROUP * HEAD_DIM
ATTN_WIDTH = N_ATTN_GROUPS * ATTN_GROUP_WIDTH
ROPE_DIM = HEAD_DIM // 4
ROPE_THETA = 500000.0
N_KEYS = 128
N_EXPERTS = N_KEYS * N_KEYS
PEER_HEADS = 8
PEER_TOPK = 16
PEER_KEY_DIM = 256
NORM_EPS = 1e-6
NEG_INF = -1e30

LANES = 128
VMEM_LIMIT = 56 * 1024 * 1024

TM_PROJ = 512
TM_MIX = 512
TM_TOPK = 512
TM_PEER = 1024
EB_PEER = 1024
QB_ATTN = 128
DFT_TS = 1024
DFT_TK = 2048


def _cparams(sem):
    return pltpu.CompilerParams(dimension_semantics=sem, vmem_limit_bytes=VMEM_LIMIT)


def _const_spec(shape):
    nd = len(shape)
    return pl.BlockSpec(shape, lambda *_: (0,) * nd)


def _proj_kernel(x_ref, g_ref, wf_ref, wq_ref, wk_ref, wv_ref, wga_ref, wgb_ref,
                 qg_ref, kg_ref, cc_ref, sc_ref, rc_ref, rs1_ref, rs2_ref,
                 fa_ref, fb_ref, q0_ref, q1_ref, q2_ref, k0_ref, k1_ref, k2_ref,
                 v0_ref, v1_ref, v2_ref, sga_ref, sgb_ref, stage_ref):
    tm = x_ref.shape[0]
    hpg = HEADS_PER_GROUP
    stage_slot = [0]

    def store_group(out_ref, gi, cols):
        dilation = ATTN_GROUPS[gi][1]
        if dilation == 1:
            for hl, c in enumerate(cols):
                out_ref[0, :, hl * HEAD_DIM:(hl + 1) * HEAD_DIM] = c.astype(BF16)
            return
        slot = stage_slot[0]
        stage_slot[0] += 1
        for hl, c in enumerate(cols):
            stage_ref[slot, hl] = c
            for r in range(dilation):
                out_ref[r, :, hl * HEAD_DIM:(hl + 1) * HEAD_DIM] = stage_ref[
                    slot, hl, pl.ds(r, tm // dilation, stride=dilation), :].astype(BF16)

    x = x_ref[...]
    ms = jnp.mean(x * x, axis=-1, keepdims=True)
    xn = (x * lax.rsqrt(ms + NORM_EPS) * g_ref[...]).astype(BF16)

    f = jnp.dot(xn, wf_ref[...], preferred_element_type=F32).astype(BF16)
    fa_ref[...] = jnp.dot(f, cc_ref[...], preferred_element_type=F32).astype(BF16)
    fb_ref[...] = jnp.dot(f, sc_ref[...], preferred_element_type=F32).astype(BF16)

    rc = rc_ref[...]
    rs1 = rs1_ref[...]
    rs2 = rs2_ref[...]

    def norm_rope(w_ref, gain_ref, out_refs, scale):
        y = jnp.dot(xn, w_ref[...], preferred_element_type=F32)
        for gi in range(N_ATTN_GROUPS):
            cols = []
            for hh in range(gi * hpg, (gi + 1) * hpg):
                yh = y[:, hh * HEAD_DIM:(hh + 1) * HEAD_DIM]
                hms = jnp.mean(yh * yh, axis=-1, keepdims=True)
                yn = yh * lax.rsqrt(hms + NORM_EPS) * gain_ref[gi:gi + 1, :]
                r = (yn * rc + pltpu.roll(yn, LANES - ROPE_DIM // 2, 1) * rs1
                     + pltpu.roll(yn, ROPE_DIM // 2, 1) * rs2)
                cols.append(r * scale)
            store_group(out_refs[gi], gi, cols)

    norm_rope(wq_ref, qg_ref, (q0_ref, q1_ref, q2_ref), HEAD_DIM ** -0.5)
    norm_rope(wk_ref, kg_ref, (k0_ref, k1_ref, k2_ref), 1.0)
    yv = jnp.dot(xn, wv_ref[...], preferred_element_type=F32)
    for gi, out_ref in enumerate((v0_ref, v1_ref, v2_ref)):
        store_group(out_ref, gi, [yv[:, hh * HEAD_DIM:(hh + 1) * HEAD_DIM]
                                  for hh in range(gi * hpg, (gi + 1) * hpg)])
    sga_ref[...] = jax.nn.sigmoid(
        jnp.dot(xn, wga_ref[...], preferred_element_type=F32)).astype(BF16)
    sgb_ref[...] = jax.nn.sigmoid(
        jnp.dot(xn, wgb_ref[...], preferred_element_type=F32)).astype(BF16)


def _rope_tables(s):
    half = ROPE_DIM // 2
    inv_freq = ROPE_THETA ** (-jnp.arange(half, dtype=F32) * 2.0 / ROPE_DIM)
    ang = jnp.arange(s, dtype=F32)[:, None] * inv_freq[None, :]
    cos, sin = jnp.cos(ang), jnp.sin(ang)
    zeros = jnp.zeros((s, HEAD_DIM - ROPE_DIM), F32)
    zh = jnp.zeros((s, half), F32)
    rc = jnp.concatenate([cos, cos, jnp.ones((s, HEAD_DIM - ROPE_DIM), F32)], axis=1)
    rs1 = jnp.concatenate([-sin, zh, zeros], axis=1)
    rs2 = jnp.concatenate([zh, sin, zeros], axis=1)
    return rc, rs1, rs2


def _channel_dft_mats():
    n = FOURIER_GROUP_DIM
    idx = (jnp.arange(n)[:, None] * jnp.arange(n)[None, :]) % n
    ang = idx.astype(F32) * (2.0 * math.pi / n)
    eye = jnp.eye(FOURIER_GROUPS, dtype=F32)
    cc = jnp.kron(eye, jnp.cos(ang) * n ** -0.5)
    sc = jnp.kron(eye, jnp.sin(ang) * n ** -0.5)
    return cc.astype(BF16), sc.astype(BF16)


def _proj(x2, s, g_mix, w_in, q_gain, k_gain, interpret):
    t = x2.shape[0]
    tm = TM_PROJ
    c0 = FOURIER_WIDTH
    c1 = c0 + ATTN_WIDTH
    c2 = c1 + ATTN_WIDTH
    c3 = c2 + ATTN_WIDTH
    c4 = c3 + D_MODEL
    wb = w_in.astype(BF16)
    wf, wq, wk, wv, wga, wgb = (wb[:, :c0], wb[:, c0:c1], wb[:, c1:c2], wb[:, c2:c3],
                                wb[:, c3:c4], wb[:, c4:])
    cc, sc = _channel_dft_mats()
    rc, rs1, rs2 = _rope_tables(s)
    spt = s // tm
    b = t // s
    tok = lambda w: pl.BlockSpec((tm, w), lambda i: (i, 0))
    pos = pl.BlockSpec((tm, HEAD_DIM), lambda i: (i % spt, 0))
    grp_shapes = [jax.ShapeDtypeStruct((b, d, s // d, ATTN_GROUP_WIDTH), BF16)
                  for _, d in ATTN_GROUPS]
    grp_specs = [pl.BlockSpec((None, d, tm // d, ATTN_GROUP_WIDTH),
                              lambda i: (i // spt, 0, i % spt, 0)) for _, d in ATTN_GROUPS]
    out_shape = ([jax.ShapeDtypeStruct((t, FOURIER_WIDTH), BF16)] * 2 + grp_shapes * 3
                 + [jax.ShapeDtypeStruct((t, D_MODEL), BF16)] * 2)
    n_staged = 3 * sum(1 for _, d in ATTN_GROUPS if d > 1)
    outs = pl.pallas_call(
        _proj_kernel,
        out_shape=out_shape,
        grid=(t // tm,),
        in_specs=[tok(D_MODEL), _const_spec((1, D_MODEL)),
                  _const_spec(wf.shape), _const_spec(wq.shape), _const_spec(wk.shape),
                  _const_spec(wv.shape), _const_spec(wga.shape), _const_spec(wgb.shape),
                  _const_spec(q_gain.shape), _const_spec(k_gain.shape),
                  _const_spec(cc.shape), _const_spec(sc.shape), pos, pos, pos],
        out_specs=[tok(FOURIER_WIDTH)] * 2 + grp_specs * 3 + [tok(D_MODEL)] * 2,
        scratch_shapes=[pltpu.VMEM((n_staged, HEADS_PER_GROUP, tm, HEAD_DIM), F32)],
        compiler_params=_cparams(("arbitrary",)),
        name="proj",
        interpret=interpret,
    )(x2, g_mix.reshape(1, D_MODEL), wf, wq, wk, wv, wga, wgb, q_gain, k_gain, cc, sc,
      rc, rs1, rs2)
    fa, fb = outs[0], outs[1]
    q, k, v = outs[2:5], outs[5:8], outs[8:11]
    return fa, fb, q, k, v, outs[11], outs[12]


LD_LANES = LANES // HEADS_PER_GROUP


def _attn_kernel(q_ref, k_ref, v_ref, o_ref, ld_ref, *, sub_len, heads):
    kb = min(2 * QB_ATTN, sub_len)
    span = QB_ATTN // 2
    first_head = pl.program_id(2) * heads

    if heads < HEADS_PER_GROUP:
        @pl.when(first_head == 0)
        def _():
            ld_ref[...] = jnp.zeros_like(ld_ref)

    def body(i, carry):
        start = pl.multiple_of(i * QB_ATTN, QB_ATTN)
        ks = pl.multiple_of(jnp.clip(start - span, 0, sub_len - kb), span)
        qpos = start + lax.broadcasted_iota(jnp.int32, (QB_ATTN, kb), 0)
        kpos = ks + lax.broadcasted_iota(jnp.int32, (QB_ATTN, kb), 1)
        in_band = jnp.abs(kpos - qpos) <= span
        lane_head = lax.broadcasted_iota(jnp.int32, (QB_ATTN, LANES), 1) // LD_LANES
        if heads < HEADS_PER_GROUP:
            ld_all = ld_ref[pl.ds(start, QB_ATTN), :]
        else:
            ld_all = jnp.zeros((QB_ATTN, LANES), F32)
        for hl in range(heads):
            cols = slice(hl * HEAD_DIM, (hl + 1) * HEAD_DIM)
            q = q_ref[pl.ds(start, QB_ATTN), cols]
            k = k_ref[pl.ds(ks, kb), cols]
            v = v_ref[pl.ds(ks, kb), cols]
            s = lax.dot_general(q, k, (((1,), (1,)), ((), ())), preferred_element_type=F32)
            s = jnp.where(in_band, s, NEG_INF)
            m = jnp.max(s, axis=1, keepdims=True)
            p = jnp.exp(s - m)
            l = jnp.sum(p, axis=1, keepdims=True)
            o = jnp.dot(p.astype(BF16), v, preferred_element_type=F32) / l
            o_ref[pl.ds(start, QB_ATTN), cols] = o.astype(BF16)
            ld_all = jnp.where(lane_head == first_head + hl, m + jnp.log(l), ld_all)
        ld_ref[pl.ds(start, QB_ATTN), :] = ld_all
        return carry

    nq = sub_len // QB_ATTN
    lax.fori_loop(0, nq, body, 0, unroll=min(nq, 2 * HEADS_PER_GROUP // heads))


ATTN_MAX_WIDE_LEN = 2048


def _attn_group(q, k, v, dilation, interpret):
    b, _, sub_len, _ = q.shape
    assert sub_len % QB_ATTN == 0
    heads = HEADS_PER_GROUP if sub_len <= ATTN_MAX_WIDE_LEN else 1
    width = heads * HEAD_DIM
    spec = pl.BlockSpec((None, None, sub_len, width), lambda bi, r, hb: (bi, r, 0, hb))
    return pl.pallas_call(
        functools.partial(_attn_kernel, sub_len=sub_len, heads=heads),
        out_shape=(jax.ShapeDtypeStruct(q.shape, BF16),
                   jax.ShapeDtypeStruct((b, dilation, sub_len, LANES), F32)),
        grid=(b, dilation, HEADS_PER_GROUP // heads),
        in_specs=[spec, spec, spec],
        out_specs=(spec, pl.BlockSpec((None, None, sub_len, LANES),
                                      lambda bi, r, hb: (bi, r, 0, 0))),
        compiler_params=_cparams(("arbitrary", "arbitrary", "arbitrary")),
        name=f"attn_d{dilation}",
        interpret=interpret,
    )(q, k, v)


def _dft_kernel(cs_ref, ss_ref, a_ref, b_ref, y_ref, acc_ref):
    kk = pl.program_id(2)

    @pl.when(kk == 0)
    def _():
        acc_ref[...] = jnp.zeros_like(acc_ref)

    acc_ref[...] += (jnp.dot(cs_ref[...], a_ref[...], preferred_element_type=F32)
                     - jnp.dot(ss_ref[...], b_ref[...], preferred_element_type=F32))

    @pl.when(kk == pl.num_programs(2) - 1)
    def _():
        y_ref[...] = acc_ref[...].astype(BF16)


def _dft_mats_kernel(ch_ref, sh_ref, cl_ref, sl_ref, cs_ref, ss_ref):
    cl, sl = cl_ref[...], sl_ref[...]
    for k in range(ch_ref.shape[1]):
        c, s_ = ch_ref[:, k:k + 1], sh_ref[:, k:k + 1]
        cols = slice(k * LANES, (k + 1) * LANES)
        cs_ref[:, cols] = (c * cl - s_ * sl).astype(BF16)
        ss_ref[:, cols] = (s_ * cl + c * sl).astype(BF16)


DFT_GEN_ROWS = 256


def _seq_dft_mats(s, interpret):
    lo = LANES
    hi = s // lo
    rows = jnp.arange(s, dtype=jnp.int32)[:, None]
    scale = 2.0 * math.pi / s
    ang_hi = ((rows * (jnp.arange(hi, dtype=jnp.int32)[None, :] * lo)) % s).astype(F32) * scale
    ang_lo = ((rows * jnp.arange(lo, dtype=jnp.int32)[None, :]) % s).astype(F32) * scale
    norm = s ** -0.5
    tr = DFT_GEN_ROWS
    coarse = pl.BlockSpec((tr, hi), lambda i: (i, 0))
    fine = pl.BlockSpec((tr, lo), lambda i: (i, 0))
    out = pl.BlockSpec((tr, s), lambda i: (i, 0))
    return pl.pallas_call(
        _dft_mats_kernel,
        out_shape=(jax.ShapeDtypeStruct((s, s), BF16),) * 2,
        grid=(s // tr,),
        in_specs=[coarse, coarse, fine, fine],
        out_specs=(out, out),
        compiler_params=_cparams(("arbitrary",)),
        name="dft_mats",
        interpret=interpret,
    )(jnp.cos(ang_hi), jnp.sin(ang_hi), jnp.cos(ang_lo) * norm, jnp.sin(ang_lo) * norm)


def _seq_dft(fa, fb, b, s, interpret):
    ts = min(DFT_TS, s)
    tk = min(DFT_TK, s)
    cs, ss = _seq_dft_mats(s, interpret)
    mat = pl.BlockSpec((ts, tk), lambda bi, i, kk: (i, kk))
    vec = pl.BlockSpec((None, tk, FOURIER_WIDTH), lambda bi, i, kk: (bi, kk, 0))
    y = pl.pallas_call(
        _dft_kernel,
        out_shape=jax.ShapeDtypeStruct((b, s, FOURIER_WIDTH), BF16),
        grid=(b, s // ts, s // tk),
        in_specs=[mat, mat, vec, vec],
        out_specs=pl.BlockSpec((None, ts, FOURIER_WIDTH), lambda bi, i, kk: (bi, i, 0)),
        scratch_shapes=[pltpu.VMEM((ts, FOURIER_WIDTH), F32)],
        compiler_params=_cparams(("arbitrary", "arbitrary", "arbitrary")),
        name="seq_dft",
        interpret=interpret,
    )(cs, ss, fa.reshape(b, s, FOURIER_WIDTH), fb.reshape(b, s, FOURIER_WIDTH))
    return y.reshape(b * s, FOURIER_WIDTH)


def _mix_kernel(x_ref, o0_ref, o1_ref, o2_ref, l0_ref, l1_ref, l2_ref, y_ref, sga_ref, sgb_ref,
                wat_ref, wfo_ref, wout_ref, gf_ref, wqy_ref, sk_ref, hx_ref,
                h_ref, hnt_ref, st_ref, ostage_ref, lstage_ref):
    tm = x_ref.shape[0]

    def token_order(ref, stage, gi):
        dilation = ATTN_GROUPS[gi][1]
        if dilation == 1:
            return ref[0].astype(F32)
        cols = []
        for c in range(ref.shape[-1] // LANES):
            for r in range(dilation):
                stage[gi - 1, c, pl.ds(r, tm // dilation, stride=dilation), :] = (
                    ref[r, :, c * LANES:(c + 1) * LANES].astype(F32))
            cols.append(stage[gi - 1, c])
        return cols[0] if len(cols) == 1 else jnp.concatenate(cols, axis=1)

    lds = [token_order(ref, lstage_ref, gi) for gi, ref in enumerate((l0_ref, l1_ref, l2_ref))]
    mx = jnp.maximum(jnp.maximum(lds[0], lds[1]), lds[2])
    ws = [jnp.exp(ld - mx) for ld in lds]
    inv = 1.0 / (ws[0] + ws[1] + ws[2])
    attn = jnp.zeros((tm, ATTN_GROUP_WIDTH), F32)
    for gi, ref in enumerate((o0_ref, o1_ref, o2_ref)):
        wexp = jnp.dot((ws[gi] * inv).astype(BF16), hx_ref[...], preferred_element_type=F32)
        attn = attn + wexp * token_order(ref, ostage_ref, gi)
    attn = jnp.dot(attn.astype(BF16), wat_ref[...], preferred_element_type=F32)
    four = jnp.dot(y_ref[...], wfo_ref[...], preferred_element_type=F32)
    mix = sga_ref[...].astype(F32) * four + sgb_ref[...].astype(F32) * attn
    h = x_ref[...] + jnp.dot(mix.astype(BF16), wout_ref[...], preferred_element_type=F32)
    h_ref[...] = h
    ms = jnp.mean(h * h, axis=-1, keepdims=True)
    hn32 = h * lax.rsqrt(ms + NORM_EPS) * gf_ref[...]
    hn = hn32.astype(BF16)
    hnt_ref[...] = hn32.T.astype(BF16)
    qy = jnp.dot(hn, wqy_ref[...], preferred_element_type=F32).astype(BF16)
    half = PEER_KEY_DIM // 2
    for hc in range(2 * PEER_HEADS):
        st_ref[hc] = lax.dot_general(sk_ref[hc], qy[:, hc * half:(hc + 1) * half],
                                     (((1,), (1,)), ((), ())), preferred_element_type=F32)


def _mix(x2, o_list, ld_list, yf, sga, sgb, w_attn, w_fourier, w_out, g_ffn, w_query, sub_keys,
         interpret):
    t = x2.shape[0]
    tm = TM_MIX
    spt = o_list[0].shape[2] // tm
    tok = lambda w: pl.BlockSpec((tm, w), lambda i: (i, 0))
    grp = lambda w: [pl.BlockSpec((None, d, tm // d, w), lambda i: (i // spt, 0, i % spt, 0))
                     for _, d in ATTN_GROUPS]
    head_expand = (jnp.arange(LANES)[:, None]
                   == LD_LANES * (jnp.arange(ATTN_GROUP_WIDTH)[None, :] // HEAD_DIM)).astype(BF16)
    n_staged = sum(1 for _, d in ATTN_GROUPS if d > 1)
    sk = jnp.stack([sub_keys[:, 0][:, jnp.array(_PAIR_ORDER)], sub_keys[:, 1]], axis=1)
    sk = sk.reshape(2 * PEER_HEADS, N_KEYS, PEER_KEY_DIM // 2).astype(BF16)
    wat, wfo, wout, wqy = (w_attn.astype(BF16), w_fourier.astype(BF16), w_out.astype(BF16),
                           w_query.astype(BF16))
    return pl.pallas_call(
        _mix_kernel,
        out_shape=(jax.ShapeDtypeStruct((t, D_MODEL), F32),
                   jax.ShapeDtypeStruct((D_MODEL, t), BF16),
                   jax.ShapeDtypeStruct((2 * PEER_HEADS, N_KEYS, t), F32)),
        grid=(t // tm,),
        in_specs=[tok(D_MODEL)] + grp(ATTN_GROUP_WIDTH) + grp(LANES) + [tok(FOURIER_WIDTH),
                  tok(D_MODEL), tok(D_MODEL),
                  _const_spec(wat.shape), _const_spec(wfo.shape), _const_spec(wout.shape),
                  _const_spec((1, D_MODEL)), _const_spec(wqy.shape), _const_spec(sk.shape),
                  _const_spec(head_expand.shape)],
        out_specs=(tok(D_MODEL), pl.BlockSpec((D_MODEL, tm), lambda i: (0, i)),
                   pl.BlockSpec((2 * PEER_HEADS, N_KEYS, tm), lambda i: (0, 0, i))),
        scratch_shapes=[pltpu.VMEM((n_staged, HEADS_PER_GROUP, tm, HEAD_DIM), F32),
                        pltpu.VMEM((n_staged, 1, tm, LANES), F32)],
        compiler_params=_cparams(("arbitrary",)),
        name="mix",
        interpret=interpret,
    )(x2, *o_list, *ld_list, yf, sga, sgb, wat, wfo, wout, g_ffn.reshape(1, D_MODEL), wqy, sk,
      head_expand)


N_PAIRS = N_KEYS // 2
_PAIR_ORDER = tuple(range(0, N_KEYS, 2)) + tuple(range(1, N_KEYS, 2))
SUBLANES = 8


def _batcher_network(n):
    def merge(lo, hi, r):
        step = r * 2
        if step < hi - lo:
            yield from merge(lo, hi, step)
            yield from merge(lo + r, hi, step)
            for i in range(lo + r, hi - r, step):
                yield (i, i + r)
        else:
            yield (lo, lo + r)

    def sort(lo, hi):
        if hi - lo >= 1:
            mid = lo + (hi - lo) // 2
            yield from sort(lo, mid)
            yield from sort(mid + 1, hi)
            yield from merge(lo, hi, 1)

    return tuple(sort(0, n - 1))


_SORT16 = _batcher_network(PEER_TOPK)


def _compare_exchange(v, i, j):
    v[i], v[j] = jnp.maximum(v[i], v[j]), jnp.minimum(v[i], v[j])


def _top16_sorted(keys):
    n = PEER_TOPK
    v = [keys[SUBLANES * k:SUBLANES * (k + 1)] for k in range(n)]
    for i, j in _SORT16:
        _compare_exchange(v, i, j)
    shift = SUBLANES // 2
    while shift >= 1:
        other = [pltpu.roll(x, shift, 0) for x in v]
        v = [jnp.maximum(v[k], other[n - 1 - k]) for k in range(n)]
        d = n // 2
        while d >= 1:
            for i in range(n):
                if not i & d:
                    _compare_exchange(v, i, i + d)
            d //= 2
        shift //= 2
    return [x[0:1] for x in v]


_CAND_PAIRS = tuple((i, j) for i in range(PEER_TOPK) for j in range(PEER_TOPK)
                    if (i + 1) * (j + 1) <= PEER_TOPK)
_CAND_ROWS = -(-len(_CAND_PAIRS) // SUBLANES) * SUBLANES
TIE_STEP = 2.0 ** -17


def _kth_slot_value(vals, k):
    slot = lax.broadcasted_iota(jnp.int32, vals.shape, 0).astype(F32)
    bumped = vals * (1.0 + slot * TIE_STEP)
    cur = bumped
    m = None
    for _ in range(k):
        m = jnp.max(cur, axis=0, keepdims=True)
        cur = jnp.where(cur == m, -1.0, cur)
    return bumped, m


def _candidates(fa, fb):
    rows = [fa[i] * fb[j] for i, j in _CAND_PAIRS]
    rows.append(jnp.zeros((_CAND_ROWS - len(rows),) + rows[0].shape[1:], rows[0].dtype))
    return jnp.concatenate(rows, axis=0)


def _topk_kernel(st_ref, e1_ref, e2_ref, th_ref):
    tm = st_ref.shape[2]
    for h in range(PEER_HEADS):
        for tc in range(tm // LANES):
            lanes = slice(tc * LANES, (tc + 1) * LANES)
            s1 = st_ref[2 * h, :, lanes]
            s2 = st_ref[2 * h + 1, :, lanes]
            a = _top16_sorted(s1)
            b = _top16_sorted(s2)
            ea = [jnp.exp(ai - a[0]) for ai in a]
            eb = [jnp.exp(bi - b[0]) for bi in b]
            cand = _candidates(ea, eb)
            bumped, kth = _kth_slot_value(cand, PEER_TOPK)
            z = jnp.sum(jnp.where(bumped >= kth, cand, 0.0), axis=0, keepdims=True)
            inv_z = 1.0 / z
            cand_b = _candidates([(x * inv_z).astype(BF16) for x in ea],
                                 [x.astype(BF16) for x in eb]).astype(F32)
            _, kth_b = _kth_slot_value(cand_b, PEER_TOPK)
            th_ref[h:h + 1, lanes] = kth_b.astype(BF16).astype(F32)
            e1 = jnp.exp(s1 - a[0]) * inv_z
            e1_ref[:, 2 * h, lanes] = e1[:N_PAIRS]
            e1_ref[:, 2 * h + 1, lanes] = e1[N_PAIRS:]
            e2_ref[h, :, lanes] = jnp.exp(s2 - b[0]).astype(BF16)


def _topk(st, interpret):
    t = st.shape[2]
    tm = TM_TOPK
    blk = lambda n: pl.BlockSpec((n, N_KEYS, tm), lambda i: (0, 0, i))
    return pl.pallas_call(
        _topk_kernel,
        out_shape=(jax.ShapeDtypeStruct((N_PAIRS, 2 * PEER_HEADS, t), F32),
                   jax.ShapeDtypeStruct((PEER_HEADS, N_KEYS, t), BF16),
                   jax.ShapeDtypeStruct((PEER_HEADS, t), F32)),
        grid=(t // tm,),
        in_specs=[blk(2 * PEER_HEADS)],
        out_specs=(pl.BlockSpec((N_PAIRS, 2 * PEER_HEADS, tm), lambda i: (0, 0, i)),
                   blk(PEER_HEADS),
                   pl.BlockSpec((PEER_HEADS, tm), lambda i: (0, i))),
        compiler_params=_cparams(("arbitrary",)),
        name="peer_topk",
        interpret=interpret,
    )(st)


SB_PEER = 2 * N_KEYS
KB_PEER = 32
SL_PEER = 4 * SB_PEER


def _peer_kernel(hnt_ref, u_ref, vt_ref, e1_ref, e2_ref, th_ref, h_ref, y_ref,
                 at0_ref, at1_ref, ht0_ref, ht1_ref, acc_ref, join_ref, *, nj):
    s = pl.program_id(0)
    tm = hnt_ref.shape[1]
    j_out = lax.rem(jnp.maximum(s - 2, 0), nj)

    @pl.when(s == 0)
    def _():
        for ref in (at0_ref, at1_ref, ht0_ref, ht1_ref):
            ref[...] = jnp.zeros_like(ref)

    @pl.when(j_out == 0)
    def _():
        acc_ref[...] = jnp.zeros_like(acc_ref)

    def stages(at_w, at_r, ht_w, ht_r):
        def slice_body(q, carry):
            r0 = pl.multiple_of(q * SL_PEER, SL_PEER)
            at_new = jnp.dot(u_ref[pl.ds(r0, SL_PEER), :], hnt_ref[...],
                             preferred_element_type=F32)
            at_w[pl.ds(r0, SL_PEER), :] = at_new
            join = at_new[SL_PEER - 8:, tm - LANES:]
            prev_zero = jnp.zeros((KB_PEER, LANES), BF16)
            for kp in range(SL_PEER // SB_PEER):
                key_pair = q * (SL_PEER // SB_PEER) + kp
                for par in range(2):
                    for tc in range(tm // LANES):
                        lanes = slice(tc * LANES, (tc + 1) * LANES)
                        e1b = [jnp.broadcast_to(
                            e1_ref[key_pair, 2 * h + par:2 * h + par + 1, lanes].astype(BF16),
                            (KB_PEER, LANES)) for h in range(PEER_HEADS)]
                        thb = [jnp.broadcast_to(th_ref[h:h + 1, lanes].astype(BF16),
                                                (KB_PEER, LANES)) for h in range(PEER_HEADS)]
                        for kb in range(N_KEYS // KB_PEER):
                            keys = slice(kb * KB_PEER, (kb + 1) * KB_PEER)
                            ra = pl.multiple_of(
                                r0 + kp * SB_PEER + par * N_KEYS + kb * KB_PEER, KB_PEER)
                            gate = prev_zero
                            for h in range(PEER_HEADS):
                                p = e1b[h] * e2_ref[h, keys, lanes]
                                gate = gate + jnp.where(p >= thb[h], p, jnp.zeros_like(p))
                            a = at_r[pl.ds(ra, KB_PEER), lanes].astype(BF16)
                            act = jax.nn.gelu(a) * gate
                            ht_w[pl.ds(ra, KB_PEER), lanes] = act
                            join = join + act[:8].astype(F32)
                            bits = pltpu.bitcast(act, jnp.uint32)
                            prev_zero = pltpu.bitcast((bits >> 16) >> 16, BF16)
            out_new = jnp.dot(vt_ref[pl.ds(r0, SL_PEER), :], ht_r[...],
                              preferred_element_type=F32)
            acc_ref[pl.ds(r0, SL_PEER), :] += out_new
            return carry + join + out_new[SL_PEER - 8:, tm - LANES:]

        jv = lax.fori_loop(0, u_ref.shape[0] // SL_PEER, slice_body, jnp.zeros((8, LANES), F32))
        join_ref[...] = jv

    parity = lax.rem(s, 2)

    @pl.when(parity == 0)
    def _():
        stages(at0_ref, at1_ref, ht1_ref, ht0_ref)

    @pl.when(parity == 1)
    def _():
        stages(at1_ref, at0_ref, ht0_ref, ht1_ref)

    @pl.when((j_out == nj - 1) & (s >= 2))
    def _():
        y_ref[...] = h_ref[...] + acc_ref[...].T


def _expert_table_kernel(u_ref, v_ref, ub_ref, vt_ref):
    ub_ref[...] = u_ref[...].astype(BF16)
    vt_ref[...] = v_ref[...].T.astype(BF16)


def _expert_tables(expert_u, expert_v, layer, eb, interpret):
    nj = N_EXPERTS // eb
    blk = pl.BlockSpec((eb, D_MODEL), lambda j: (j, 0))
    layer_blk = pl.BlockSpec((None, eb, D_MODEL), lambda j: (layer, j, 0))
    return pl.pallas_call(
        _expert_table_kernel,
        out_shape=(jax.ShapeDtypeStruct((N_EXPERTS, D_MODEL), BF16),
                   jax.ShapeDtypeStruct((nj, D_MODEL, eb), BF16)),
        grid=(nj,),
        in_specs=[layer_blk, layer_blk],
        out_specs=(blk, pl.BlockSpec((None, D_MODEL, eb), lambda j: (j, 0, 0))),
        compiler_params=_cparams(("arbitrary",)),
        name="expert_tables",
        interpret=interpret,
    )(expert_u, expert_v)


def _peer(hnt, h, e1, e2, th, u, vt, interpret):
    t = hnt.shape[1]
    tm, eb = TM_PEER, EB_PEER
    assert eb == D_MODEL
    nj = N_EXPERTS // eb
    n_pairs = (t // tm) * nj
    tile = lambda s: s // nj
    blk = lambda s: lax.rem(s, nj)
    s1 = lambda s: jnp.minimum(s, n_pairs - 1)
    s2 = lambda s: jnp.clip(s - 1, 0, n_pairs - 1)
    s3 = lambda s: jnp.maximum(s - 2, 0)
    return pl.pallas_call(
        functools.partial(_peer_kernel, nj=nj),
        out_shape=jax.ShapeDtypeStruct((t, D_MODEL), F32),
        grid=(n_pairs + 2,),
        in_specs=[pl.BlockSpec((D_MODEL, tm), lambda s: (0, tile(s1(s)))),
                  pl.BlockSpec((eb, D_MODEL), lambda s: (blk(s1(s)), 0)),
                  pl.BlockSpec((None, D_MODEL, eb), lambda s: (blk(s3(s)), 0, 0)),
                  pl.BlockSpec((eb // SB_PEER, 2 * PEER_HEADS, tm),
                               lambda s: (blk(s2(s)), 0, tile(s2(s)))),
                  pl.BlockSpec((PEER_HEADS, N_KEYS, tm), lambda s: (0, 0, tile(s2(s)))),
                  pl.BlockSpec((PEER_HEADS, tm), lambda s: (0, tile(s2(s)))),
                  pl.BlockSpec((tm, D_MODEL), lambda s: (tile(s3(s)), 0))],
        out_specs=pl.BlockSpec((tm, D_MODEL), lambda s: (tile(s3(s)), 0)),
        scratch_shapes=[pltpu.VMEM((eb, tm), F32), pltpu.VMEM((eb, tm), F32),
                        pltpu.VMEM((eb, tm), BF16), pltpu.VMEM((eb, tm), BF16),
                        pltpu.VMEM((D_MODEL, tm), F32), pltpu.VMEM((8, LANES), F32)],
        compiler_params=_cparams(("arbitrary",)),
        name="peer_experts",
        interpret=interpret,
    )(hnt, u, vt, e1, e2, th, h)


def _hybrid_block(x, g_mix, w_in, q_gain, k_gain, w_fourier, w_attn, w_out, g_ffn, w_query,
                  sub_keys, expert_u_b, expert_vt_b, interpret=False):
    b, s, _ = x.shape
    x2 = x.reshape(b * s, D_MODEL)
    fa, fb, q, k, v, sga, sgb = _proj(x2, s, g_mix, w_in, q_gain, k_gain, interpret)
    o_list, ld_list = [], []
    for gi, (_, dilation) in enumerate(ATTN_GROUPS):
        o, ld = _attn_group(q[gi], k[gi], v[gi], dilation, interpret)
        o_list.append(o)
        ld_list.append(ld)
    yf = _seq_dft(fa, fb, b, s, interpret)
    h, hnt, st = _mix(x2, o_list, ld_list, yf, sga, sgb, w_attn, w_fourier, w_out, g_ffn,
                      w_query, sub_keys, interpret)
    e1, e2, th = _topk(st, interpret)
    y = _peer(hnt, h, e1, e2, th, expert_u_b, expert_vt_b, interpret)
    return y.reshape(b, s, D_MODEL)


def kernel(x_prompt, x_sample, g_mix, w_in, q_gain, k_gain, w_fourier, w_attn, w_out, g_ffn,
           w_query, sub_keys, expert_u, expert_v):
    y_prompt, y_sample = x_prompt, x_sample
    for layer in range(g_mix.shape[0]):
        params = (g_mix[layer], w_in[layer], q_gain[layer], k_gain[layer], w_fourier[layer],
                  w_attn[layer], w_out[layer], g_ffn[layer], w_query[layer], sub_keys[layer],
                  *_expert_tables(expert_u, expert_v, layer, EB_PEER, False))
        y_prompt = _hybrid_block(y_prompt, *params)
        y_sample = _hybrid_block(y_sample, *params)
    return (y_prompt, y_sample)
```

```python
import functools
import math

import jax
import jax.numpy as jnp
from jax import lax
from jax.experimental import pallas as pl
from jax.experimental.pallas import tpu as pltpu

F32 = jnp.float32
BF16 = jnp.bfloat16

D_MODEL = 1024
FOURIER_GROUPS = 4
FOURIER_GROUP_DIM = 128
FOURIER_WIDTH = FOURIER_GROUPS * FOURIER_GROUP_DIM
ATTN_GROUPS = ((128, 1), (512, 4), (2048, 16))
N_ATTN_GROUPS = 3
HEADS_PER_GROUP = 4
HEAD_DIM = 128
ATTN_GROUP_WIDTH = HEADS_PER_GROUP * HEAD_DIM
ATTN_WIDTH = N_ATTN_GROUPS * ATTN_GROUP_WIDTH
ROPE_DIM = HEAD_DIM // 4
ROPE_THETA = 500000.0
N_KEYS = 128
N_EXPERTS = N_KEYS * N_KEYS
PEER_HEADS = 8
PEER_TOPK = 16
PEER_KEY_DIM = 256
NORM_EPS = 1e-6
NEG_INF = -1e30

LANES = 128
VMEM_LIMIT = 56 * 1024 * 1024

TM_PROJ = 512
TM_MIX = 512
TM_TOPK = 512
TM_PEER = 512
EB_PEER = 1024
QB_ATTN = 128
DFT_TS = 1024
DFT_TK = 2048


def _cparams(sem):
    return pltpu.CompilerParams(dimension_semantics=sem, vmem_limit_bytes=VMEM_LIMIT)


def _const_spec(shape):
    nd = len(shape)
    return pl.BlockSpec(shape, lambda *_: (0,) * nd)


def _proj_kernel(x_ref, g_ref, wf_ref, wq_ref, wk_ref, wv_ref, wga_ref, wgb_ref,
                 qg_ref, kg_ref, cc_ref, sc_ref, rc_ref, rs1_ref, rs2_ref,
                 fa_ref, fb_ref, q0_ref, q1_ref, q2_ref, k0_ref, k1_ref, k2_ref,
                 v0_ref, v1_ref, v2_ref, sga_ref, sgb_ref, stage_ref):
    tm = x_ref.shape[0]
    hpg = HEADS_PER_GROUP
    stage_slot = [0]

    def store_group(out_ref, gi, cols):
        dilation = ATTN_GROUPS[gi][1]
        if dilation == 1:
            for hl, c in enumerate(cols):
                out_ref[0, :, hl * HEAD_DIM:(hl + 1) * HEAD_DIM] = c.astype(BF16)
            return
        slot = stage_slot[0]
        stage_slot[0] += 1
        for hl, c in enumerate(cols):
            stage_ref[slot, hl] = c
            for r in range(dilation):
                out_ref[r, :, hl * HEAD_DIM:(hl + 1) * HEAD_DIM] = stage_ref[
                    slot, hl, pl.ds(r, tm // dilation, stride=dilation), :].astype(BF16)

    x = x_ref[...]
    ms = jnp.mean(x * x, axis=-1, keepdims=True)
    xn = (x * lax.rsqrt(ms + NORM_EPS) * g_ref[...]).astype(BF16)

    f = jnp.dot(xn, wf_ref[...], preferred_element_type=F32).astype(BF16)
    fa_ref[...] = jnp.dot(f, cc_ref[...], preferred_element_type=F32).astype(BF16)
    fb_ref[...] = jnp.dot(f, sc_ref[...], preferred_element_type=F32).astype(BF16)

    rc = rc_ref[...]
    rs1 = rs1_ref[...]
    rs2 = rs2_ref[...]

    def norm_rope(w_ref, gain_ref, out_refs, scale):
        y = jnp.dot(xn, w_ref[...], preferred_element_type=F32)
        for gi in range(N_ATTN_GROUPS):
            cols = []
            for hh in range(gi * hpg, (gi + 1) * hpg):
                yh = y[:, hh * HEAD_DIM:(hh + 1) * HEAD_DIM]
                hms = jnp.mean(yh * yh, axis=-1, keepdims=True)
                yn = yh * lax.rsqrt(hms + NORM_EPS) * gain_ref[gi:gi + 1, :]
                r = (yn * rc + pltpu.roll(yn, LANES - ROPE_DIM // 2, 1) * rs1
                     + pltpu.roll(yn, ROPE_DIM // 2, 1) * rs2)
                cols.append(r * scale)
            store_group(out_refs[gi], gi, cols)

    norm_rope(wq_ref, qg_ref, (q0_ref, q1_ref, q2_ref), HEAD_DIM ** -0.5)
    norm_rope(wk_ref, kg_ref, (k0_ref, k1_ref, k2_ref), 1.0)
    yv = jnp.dot(xn, wv_ref[...], preferred_element_type=F32)
    for gi, out_ref in enumerate((v0_ref, v1_ref, v2_ref)):
        store_group(out_ref, gi, [yv[:, hh * HEAD_DIM:(hh + 1) * HEAD_DIM]
                                  for hh in range(gi * hpg, (gi + 1) * hpg)])
    sga_ref[...] = jax.nn.sigmoid(
        jnp.dot(xn, wga_ref[...], preferred_element_type=F32)).astype(BF16)
    sgb_ref[...] = jax.nn.sigmoid(
        jnp.dot(xn, wgb_ref[...], preferred_element_type=F32)).astype(BF16)


def _rope_tables(s):
    half = ROPE_DIM // 2
    inv_freq = ROPE_THETA ** (-jnp.arange(half, dtype=F32) * 2.0 / ROPE_DIM)
    ang = jnp.arange(s, dtype=F32)[:, None] * inv_freq[None, :]
    cos, sin = jnp.cos(ang), jnp.sin(ang)
    zeros = jnp.zeros((s, HEAD_DIM - ROPE_DIM), F32)
    zh = jnp.zeros((s, half), F32)
    rc = jnp.concatenate([cos, cos, jnp.ones((s, HEAD_DIM - ROPE_DIM), F32)], axis=1)
    rs1 = jnp.concatenate([-sin, zh, zeros], axis=1)
    rs2 = jnp.concatenate([zh, sin, zeros], axis=1)
    return rc, rs1, rs2


def _channel_dft_mats():
    n = FOURIER_GROUP_DIM
    idx = (jnp.arange(n)[:, None] * jnp.arange(n)[None, :]) % n
    ang = idx.astype(F32) * (2.0 * math.pi / n)
    eye = jnp.eye(FOURIER_GROUPS, dtype=F32)
    cc = jnp.kron(eye, jnp.cos(ang) * n ** -0.5)
    sc = jnp.kron(eye, jnp.sin(ang) * n ** -0.5)
    return cc.astype(BF16), sc.astype(BF16)


def _proj(x2, s, g_mix, w_in, q_gain, k_gain, interpret):
    t = x2.shape[0]
    tm = TM_PROJ
    c0 = FOURIER_WIDTH
    c1 = c0 + ATTN_WIDTH
    c2 = c1 + ATTN_WIDTH
    c3 = c2 + ATTN_WIDTH
    c4 = c3 + D_MODEL
    wb = w_in.astype(BF16)
    wf, wq, wk, wv, wga, wgb = (wb[:, :c0], wb[:, c0:c1], wb[:, c1:c2], wb[:, c2:c3],
                                wb[:, c3:c4], wb[:, c4:])
    cc, sc = _channel_dft_mats()
    rc, rs1, rs2 = _rope_tables(s)
    spt = s // tm
    b = t // s
    tok = lambda w: pl.BlockSpec((tm, w), lambda i: (i, 0))
    pos = pl.BlockSpec((tm, HEAD_DIM), lambda i: (i % spt, 0))
    grp_shapes = [jax.ShapeDtypeStruct((b, d, s // d, ATTN_GROUP_WIDTH), BF16)
                  for _, d in ATTN_GROUPS]
    grp_specs = [pl.BlockSpec((None, d, tm // d, ATTN_GROUP_WIDTH),
                              lambda i: (i // spt, 0, i % spt, 0)) for _, d in ATTN_GROUPS]
    out_shape = ([jax.ShapeDtypeStruct((t, FOURIER_WIDTH), BF16)] * 2 + grp_shapes * 3
                 + [jax.ShapeDtypeStruct((t, D_MODEL), BF16)] * 2)
    n_staged = 3 * sum(1 for _, d in ATTN_GROUPS if d > 1)
    outs = pl.pallas_call(
        _proj_kernel,
        out_shape=out_shape,
        grid=(t // tm,),
        in_specs=[tok(D_MODEL), _const_spec((1, D_MODEL)),
                  _const_spec(wf.shape), _const_spec(wq.shape), _const_spec(wk.shape),
                  _const_spec(wv.shape), _const_spec(wga.shape), _const_spec(wgb.shape),
                  _const_spec(q_gain.shape), _const_spec(k_gain.shape),
                  _const_spec(cc.shape), _const_spec(sc.shape), pos, pos, pos],
        out_specs=[tok(FOURIER_WIDTH)] * 2 + grp_specs * 3 + [tok(D_MODEL)] * 2,
        scratch_shapes=[pltpu.VMEM((n_staged, HEADS_PER_GROUP, tm, HEAD_DIM), F32)],
        compiler_params=_cparams(("arbitrary",)),
        name="proj",
        interpret=interpret,
    )(x2, g_mix.reshape(1, D_MODEL), wf, wq, wk, wv, wga, wgb, q_gain, k_gain, cc, sc,
      rc, rs1, rs2)
    fa, fb = outs[0], outs[1]
    q, k, v = outs[2:5], outs[5:8], outs[8:11]
    return fa, fb, q, k, v, outs[11], outs[12]


LD_LANES = LANES // HEADS_PER_GROUP


def _attn_kernel(q_ref, k_ref, v_ref, o_ref, ld_ref, *, sub_len, heads):
    kb = min(2 * QB_ATTN, sub_len)
    span = QB_ATTN // 2
    first_head = pl.program_id(2) * heads
    residues = q_ref.shape[0]

    if heads < HEADS_PER_GROUP:
        @pl.when(first_head == 0)
        def _():
            ld_ref[...] = jnp.zeros_like(ld_ref)

    def body(i, carry):
        start = pl.multiple_of(i * QB_ATTN, QB_ATTN)
        ks = pl.multiple_of(jnp.clip(start - span, 0, sub_len - kb), span)
        qpos = start + lax.broadcasted_iota(jnp.int32, (QB_ATTN, kb), 0)
        kpos = ks + lax.broadcasted_iota(jnp.int32, (QB_ATTN, kb), 1)
        in_band = jnp.abs(kpos - qpos) <= span
        lane_head = lax.broadcasted_iota(jnp.int32, (QB_ATTN, LANES), 1) // LD_LANES
        for g in range(residues):
            if heads < HEADS_PER_GROUP:
                ld_all = ld_ref[g, pl.ds(start, QB_ATTN), :]
            else:
                ld_all = jnp.zeros((QB_ATTN, LANES), F32)
            for hl in range(heads):
                cols = slice(hl * HEAD_DIM, (hl + 1) * HEAD_DIM)
                q = q_ref[g, pl.ds(start, QB_ATTN), cols]
                k = k_ref[g, pl.ds(ks, kb), cols]
                v = v_ref[g, pl.ds(ks, kb), cols]
                s = lax.dot_general(q, k, (((1,), (1,)), ((), ())),
                                    preferred_element_type=F32)
                s = jnp.where(in_band, s, NEG_INF)
                m = jnp.max(s, axis=1, keepdims=True)
                p = jnp.exp(s - m)
                l = jnp.sum(p, axis=1, keepdims=True)
                o = jnp.dot(p.astype(BF16), v, preferred_element_type=F32) / l
                o_ref[g, pl.ds(start, QB_ATTN), cols] = o.astype(BF16)
                ld_all = jnp.where(lane_head == first_head + hl, m + jnp.log(l), ld_all)
            ld_ref[g, pl.ds(start, QB_ATTN), :] = ld_all
        return carry

    nq = sub_len // QB_ATTN
    lax.fori_loop(0, nq, body, 0, unroll=max(1, min(nq, ATTN_CHAINS // (heads * residues))))


ATTN_MAX_WIDE_LEN = 2048
ATTN_CHAINS = 16


def _attn_group(q, k, v, dilation, interpret):
    b, _, sub_len, _ = q.shape
    assert sub_len % QB_ATTN == 0
    heads = HEADS_PER_GROUP if sub_len <= ATTN_MAX_WIDE_LEN else 1
    width = heads * HEAD_DIM
    residues = max(1, min(dilation, ATTN_CHAINS * QB_ATTN // (heads * sub_len)))
    spec = pl.BlockSpec((None, residues, sub_len, width), lambda bi, r, hb: (bi, r, 0, hb))
    return pl.pallas_call(
        functools.partial(_attn_kernel, sub_len=sub_len, heads=heads),
        out_shape=(jax.ShapeDtypeStruct(q.shape, BF16),
                   jax.ShapeDtypeStruct((b, dilation, sub_len, LANES), F32)),
        grid=(b, dilation // residues, HEADS_PER_GROUP // heads),
        in_specs=[spec, spec, spec],
        out_specs=(spec, pl.BlockSpec((None, residues, sub_len, LANES),
                                      lambda bi, r, hb: (bi, r, 0, 0))),
        compiler_params=_cparams(("arbitrary", "arbitrary", "arbitrary")),
        name=f"attn_d{dilation}",
        interpret=interpret,
    )(q, k, v)


def _dft_kernel(cs_ref, ss_ref, a_ref, b_ref, y_ref, acc_ref):
    kk = pl.program_id(2)

    @pl.when(kk == 0)
    def _():
        acc_ref[...] = jnp.zeros_like(acc_ref)

    acc_ref[...] += (jnp.dot(cs_ref[...], a_ref[...], preferred_element_type=F32)
                     - jnp.dot(ss_ref[...], b_ref[...], preferred_element_type=F32))

    @pl.when(kk == pl.num_programs(2) - 1)
    def _():
        y_ref[...] = acc_ref[...].astype(BF16)


def _dft_mats_kernel(ch_ref, sh_ref, cl_ref, sl_ref, cs_ref, ss_ref):
    cl, sl = cl_ref[...], sl_ref[...]
    for k in range(ch_ref.shape[1]):
        c, s_ = ch_ref[:, k:k + 1], sh_ref[:, k:k + 1]
        cols = slice(k * LANES, (k + 1) * LANES)
        cs_ref[:, cols] = (c * cl - s_ * sl).astype(BF16)
        ss_ref[:, cols] = (s_ * cl + c * sl).astype(BF16)


DFT_GEN_ROWS = 256


def _seq_dft_mats(s, interpret):
    lo = LANES
    hi = s // lo
    rows = jnp.arange(s, dtype=jnp.int32)[:, None]
    scale = 2.0 * math.pi / s
    ang_hi = ((rows * (jnp.arange(hi, dtype=jnp.int32)[None, :] * lo)) % s).astype(F32) * scale
    ang_lo = ((rows * jnp.arange(lo, dtype=jnp.int32)[None, :]) % s).astype(F32) * scale
    norm = s ** -0.5
    tr = DFT_GEN_ROWS
    coarse = pl.BlockSpec((tr, hi), lambda i: (i, 0))
    fine = pl.BlockSpec((tr, lo), lambda i: (i, 0))
    out = pl.BlockSpec((tr, s), lambda i: (i, 0))
    return pl.pallas_call(
        _dft_mats_kernel,
        out_shape=(jax.ShapeDtypeStruct((s, s), BF16),) * 2,
        grid=(s // tr,),
        in_specs=[coarse, coarse, fine, fine],
        out_specs=(out, out),
        compiler_params=_cparams(("arbitrary",)),
        name="dft_mats",
        interpret=interpret,
    )(jnp.cos(ang_hi), jnp.sin(ang_hi), jnp.cos(ang_lo) * norm, jnp.sin(ang_lo) * norm)


def _seq_dft(fa, fb, b, s, interpret):
    ts = min(DFT_TS, s)
    tk = min(DFT_TK, s)
    cs, ss = _seq_dft_mats(s, interpret)
    mat = pl.BlockSpec((ts, tk), lambda bi, i, kk: (i, kk))
    vec = pl.BlockSpec((None, tk, FOURIER_WIDTH), lambda bi, i, kk: (bi, kk, 0))
    y = pl.pallas_call(
        _dft_kernel,
        out_shape=jax.ShapeDtypeStruct((b, s, FOURIER_WIDTH), BF16),
        grid=(b, s // ts, s // tk),
        in_specs=[mat, mat, vec, vec],
        out_specs=pl.BlockSpec((None, ts, FOURIER_WIDTH), lambda bi, i, kk: (bi, i, 0)),
        scratch_shapes=[pltpu.VMEM((ts, FOURIER_WIDTH), F32)],
        compiler_params=_cparams(("arbitrary", "arbitrary", "arbitrary")),
        name="seq_dft",
        interpret=interpret,
    )(cs, ss, fa.reshape(b, s, FOURIER_WIDTH), fb.reshape(b, s, FOURIER_WIDTH))
    return y.reshape(b * s, FOURIER_WIDTH)


def _mix_kernel(x_ref, o0_ref, o1_ref, o2_ref, l0_ref, l1_ref, l2_ref, y_ref, sga_ref, sgb_ref,
                wat_ref, wfo_ref, wout_ref, gf_ref, wqy_ref, sk_ref, hx_ref,
                h_ref, hnt_ref, st_ref, ostage_ref, lstage_ref):
    tm = x_ref.shape[0]

    def token_order(ref, stage, gi):
        dilation = ATTN_GROUPS[gi][1]
        if dilation == 1:
            return ref[0].astype(F32)
        cols = []
        for c in range(ref.shape[-1] // LANES):
            for r in range(dilation):
                stage[gi - 1, c, pl.ds(r, tm // dilation, stride=dilation), :] = (
                    ref[r, :, c * LANES:(c + 1) * LANES].astype(F32))
            cols.append(stage[gi - 1, c])
        return cols[0] if len(cols) == 1 else jnp.concatenate(cols, axis=1)

    lds = [token_order(ref, lstage_ref, gi) for gi, ref in enumerate((l0_ref, l1_ref, l2_ref))]
    mx = jnp.maximum(jnp.maximum(lds[0], lds[1]), lds[2])
    ws = [jnp.exp(ld - mx) for ld in lds]
    inv = 1.0 / (ws[0] + ws[1] + ws[2])
    attn = jnp.zeros((tm, ATTN_GROUP_WIDTH), F32)
    for gi, ref in enumerate((o0_ref, o1_ref, o2_ref)):
        wexp = jnp.dot((ws[gi] * inv).astype(BF16), hx_ref[...], preferred_element_type=F32)
        attn = attn + wexp * token_order(ref, ostage_ref, gi)
    attn = jnp.dot(attn.astype(BF16), wat_ref[...], preferred_element_type=F32)
    four = jnp.dot(y_ref[...], wfo_ref[...], preferred_element_type=F32)
    mix = sga_ref[...].astype(F32) * four + sgb_ref[...].astype(F32) * attn
    h = x_ref[...] + jnp.dot(mix.astype(BF16), wout_ref[...], preferred_element_type=F32)
    h_ref[...] = h
    ms = jnp.mean(h * h, axis=-1, keepdims=True)
    hn32 = h * lax.rsqrt(ms + NORM_EPS) * gf_ref[...]
    hn = hn32.astype(BF16)
    hnt_ref[...] = hn32.T.astype(BF16)
    qy = jnp.dot(hn, wqy_ref[...], preferred_element_type=F32).astype(BF16)
    half = PEER_KEY_DIM // 2
    for hc in range(2 * PEER_HEADS):
        st_ref[hc] = lax.dot_general(sk_ref[hc], qy[:, hc * half:(hc + 1) * half],
                                     (((1,), (1,)), ((), ())), preferred_element_type=F32)


def _mix(x2, o_list, ld_list, yf, sga, sgb, w_attn, w_fourier, w_out, g_ffn, w_query, sub_keys,
         interpret):
    t = x2.shape[0]
    tm = TM_MIX
    spt = o_list[0].shape[2] // tm
    tok = lambda w: pl.BlockSpec((tm, w), lambda i: (i, 0))
    grp = lambda w: [pl.BlockSpec((None, d, tm // d, w), lambda i: (i // spt, 0, i % spt, 0))
                     for _, d in ATTN_GROUPS]
    head_expand = (jnp.arange(LANES)[:, None]
                   == LD_LANES * (jnp.arange(ATTN_GROUP_WIDTH)[None, :] // HEAD_DIM)).astype(BF16)
    n_staged = sum(1 for _, d in ATTN_GROUPS if d > 1)
    sk = jnp.stack([sub_keys[:, 0][:, jnp.array(_PAIR_ORDER)], sub_keys[:, 1]], axis=1)
    sk = sk.reshape(2 * PEER_HEADS, N_KEYS, PEER_KEY_DIM // 2).astype(BF16)
    wat, wfo, wout, wqy = (w_attn.astype(BF16), w_fourier.astype(BF16), w_out.astype(BF16),
                           w_query.astype(BF16))
    return pl.pallas_call(
        _mix_kernel,
        out_shape=(jax.ShapeDtypeStruct((t, D_MODEL), F32),
                   jax.ShapeDtypeStruct((D_MODEL, t), BF16),
                   jax.ShapeDtypeStruct((2 * PEER_HEADS, N_KEYS, t), F32)),
        grid=(t // tm,),
        in_specs=[tok(D_MODEL)] + grp(ATTN_GROUP_WIDTH) + grp(LANES) + [tok(FOURIER_WIDTH),
                  tok(D_MODEL), tok(D_MODEL),
                  _const_spec(wat.shape), _const_spec(wfo.shape), _const_spec(wout.shape),
                  _const_spec((1, D_MODEL)), _const_spec(wqy.shape), _const_spec(sk.shape),
                  _const_spec(head_expand.shape)],
        out_specs=(tok(D_MODEL), pl.BlockSpec((D_MODEL, tm), lambda i: (0, i)),
                   pl.BlockSpec((2 * PEER_HEADS, N_KEYS, tm), lambda i: (0, 0, i))),
        scratch_shapes=[pltpu.VMEM((n_staged, HEADS_PER_GROUP, tm, HEAD_DIM), F32),
                        pltpu.VMEM((n_staged, 1, tm, LANES), F32)],
        compiler_params=_cparams(("arbitrary",)),
        name="mix",
        interpret=interpret,
    )(x2, *o_list, *ld_list, yf, sga, sgb, wat, wfo, wout, g_ffn.reshape(1, D_MODEL), wqy, sk,
      head_expand)


N_PAIRS = N_KEYS // 2
_PAIR_ORDER = tuple(range(0, N_KEYS, 2)) + tuple(range(1, N_KEYS, 2))
SUBLANES = 8


def _batcher_network(n):
    def merge(lo, hi, r):
        step = r * 2
        if step < hi - lo:
            yield from merge(lo, hi, step)
            yield from merge(lo + r, hi, step)
            for i in range(lo + r, hi - r, step):
                yield (i, i + r)
        else:
            yield (lo, lo + r)

    def sort(lo, hi):
        if hi - lo >= 1:
            mid = lo + (hi - lo) // 2
            yield from sort(lo, mid)
            yield from sort(mid + 1, hi)
            yield from merge(lo, hi, 1)

    return tuple(sort(0, n - 1))


_SORT16 = _batcher_network(PEER_TOPK)


def _compare_exchange(v, i, j):
    v[i], v[j] = jnp.maximum(v[i], v[j]), jnp.minimum(v[i], v[j])


def _top16_sorted(keys):
    n = PEER_TOPK
    v = [keys[SUBLANES * k:SUBLANES * (k + 1)] for k in range(n)]
    for i, j in _SORT16:
        _compare_exchange(v, i, j)
    shift = SUBLANES // 2
    while shift >= 1:
        other = [pltpu.roll(x, shift, 0) for x in v]
        v = [jnp.maximum(v[k], other[n - 1 - k]) for k in range(n)]
        d = n // 2
        while d >= 1:
            for i in range(n):
                if not i & d:
                    _compare_exchange(v, i, i + d)
            d //= 2
        shift //= 2
    return [x[0:1] for x in v]


_CAND_PAIRS = tuple((i, j) for i in range(PEER_TOPK) for j in range(PEER_TOPK)
                    if (i + 1) * (j + 1) <= PEER_TOPK)
_CAND_ROWS = -(-len(_CAND_PAIRS) // SUBLANES) * SUBLANES
TIE_STEP = 2.0 ** -17


def _kth_slot_value(vals, k):
    slot = lax.broadcasted_iota(jnp.int32, vals.shape, 0).astype(F32)
    bumped = vals * (1.0 + slot * TIE_STEP)
    cur = bumped
    m = None
    for _ in range(k):
        m = jnp.max(cur, axis=0, keepdims=True)
        cur = jnp.where(cur == m, -1.0, cur)
    return bumped, m


def _candidates(fa, fb):
    rows = [fa[i] * fb[j] for i, j in _CAND_PAIRS]
    rows.append(jnp.zeros((_CAND_ROWS - len(rows),) + rows[0].shape[1:], rows[0].dtype))
    return jnp.concatenate(rows, axis=0)


def _topk_kernel(st_ref, e1_ref, e2_ref, th_ref):
    tm = st_ref.shape[2]
    for h in range(PEER_HEADS):
        for tc in range(tm // LANES):
            lanes = slice(tc * LANES, (tc + 1) * LANES)
            s1 = st_ref[2 * h, :, lanes]
            s2 = st_ref[2 * h + 1, :, lanes]
            a = _top16_sorted(s1)
            b = _top16_sorted(s2)
            ea = [jnp.exp(ai - a[0]) for ai in a]
            eb = [jnp.exp(bi - b[0]) for bi in b]
            cand = _candidates(ea, eb)
            bumped, kth = _kth_slot_value(cand, PEER_TOPK)
            z = jnp.sum(jnp.where(bumped >= kth, cand, 0.0), axis=0, keepdims=True)
            inv_z = 1.0 / z
            cand_b = _candidates([(x * inv_z).astype(BF16) for x in ea],
                                 [x.astype(BF16) for x in eb]).astype(F32)
            _, kth_b = _kth_slot_value(cand_b, PEER_TOPK)
            th_ref[h:h + 1, lanes] = kth_b.astype(BF16).astype(F32)
            e1 = jnp.exp(s1 - a[0]) * inv_z
            e1_ref[:, 2 * h, lanes] = e1[:N_PAIRS]
            e1_ref[:, 2 * h + 1, lanes] = e1[N_PAIRS:]
            e2_ref[h, :, lanes] = jnp.exp(s2 - b[0]).astype(BF16)


def _topk(st, interpret):
    t = st.shape[2]
    tm = TM_TOPK
    blk = lambda n: pl.BlockSpec((n, N_KEYS, tm), lambda i: (0, 0, i))
    return pl.pallas_call(
        _topk_kernel,
        out_shape=(jax.ShapeDtypeStruct((N_PAIRS, 2 * PEER_HEADS, t), F32),
                   jax.ShapeDtypeStruct((PEER_HEADS, N_KEYS, t), BF16),
                   jax.ShapeDtypeStruct((PEER_HEADS, t), F32)),
        grid=(t // tm,),
        in_specs=[blk(2 * PEER_HEADS)],
        out_specs=(pl.BlockSpec((N_PAIRS, 2 * PEER_HEADS, tm), lambda i: (0, 0, i)),
                   blk(PEER_HEADS),
                   pl.BlockSpec((PEER_HEADS, tm), lambda i: (0, i))),
        compiler_params=_cparams(("arbitrary",)),
        name="peer_topk",
        interpret=interpret,
    )(st)


SB_PEER = 2 * N_KEYS
KB_PEER = 32
SL_PEER = 4 * SB_PEER


def _peer_kernel(hnt_ref, u_ref, vt_ref, e1_ref, e2_ref, th_ref, h_ref, y_ref,
                 at0_ref, at1_ref, ht0_ref, ht1_ref, acc_ref, join_ref, *, nj):
    s = pl.program_id(0)
    tm = hnt_ref.shape[1]
    j_out = lax.rem(jnp.maximum(s - 2, 0), nj)

    @pl.when(s == 0)
    def _():
        for ref in (at0_ref, at1_ref, ht0_ref, ht1_ref):
            ref[...] = jnp.zeros_like(ref)

    @pl.when(j_out == 0)
    def _():
        acc_ref[...] = jnp.zeros_like(acc_ref)

    def stages(at_w, at_r, ht_w, ht_r):
        def slice_body(q, carry):
            r0 = pl.multiple_of(q * SL_PEER, SL_PEER)
            at_new = jnp.dot(u_ref[pl.ds(r0, SL_PEER), :], hnt_ref[...],
                             preferred_element_type=F32)
            at_w[pl.ds(r0, SL_PEER), :] = at_new
            join = at_new[SL_PEER - 8:, tm - LANES:]
            prev_zero = jnp.zeros((KB_PEER, LANES), BF16)
            for kp in range(SL_PEER // SB_PEER):
                key_pair = q * (SL_PEER // SB_PEER) + kp
                for par in range(2):
                    for tc in range(tm // LANES):
                        lanes = slice(tc * LANES, (tc + 1) * LANES)
                        e1b = [jnp.broadcast_to(
                            e1_ref[key_pair, 2 * h + par:2 * h + par + 1, lanes].astype(BF16),
                            (KB_PEER, LANES)) for h in range(PEER_HEADS)]
                        thb = [jnp.broadcast_to(th_ref[h:h + 1, lanes].astype(BF16),
                                                (KB_PEER, LANES)) for h in range(PEER_HEADS)]
                        for kb in range(N_KEYS // KB_PEER):
                            keys = slice(kb * KB_PEER, (kb + 1) * KB_PEER)
                            ra = pl.multiple_of(
                                r0 + kp * SB_PEER + par * N_KEYS + kb * KB_PEER, KB_PEER)
                            gate = prev_zero
                            for h in range(PEER_HEADS):
                                p = e1b[h] * e2_ref[h, keys, lanes]
                                gate = gate + jnp.where(p >= thb[h], p, jnp.zeros_like(p))
                            a = at_r[pl.ds(ra, KB_PEER), lanes].astype(BF16)
                            act = jax.nn.gelu(a) * gate
                            ht_w[pl.ds(ra, KB_PEER), lanes] = act
                            join = join + act[:8].astype(F32)
                            bits = pltpu.bitcast(act, jnp.uint32)
                            prev_zero = pltpu.bitcast((bits >> 16) >> 16, BF16)
            out_new = jnp.dot(vt_ref[pl.ds(r0, SL_PEER), :], ht_r[...],
                              preferred_element_type=F32)
            acc_ref[pl.ds(r0, SL_PEER), :] += out_new
            return carry + join + out_new[SL_PEER - 8:, tm - LANES:]

        jv = lax.fori_loop(0, u_ref.shape[0] // SL_PEER, slice_body, jnp.zeros((8, LANES), F32))
        join_ref[...] = jv

    parity = lax.rem(s, 2)

    @pl.when(parity == 0)
    def _():
        stages(at0_ref, at1_ref, ht1_ref, ht0_ref)

    @pl.when(parity == 1)
    def _():
        stages(at1_ref, at0_ref, ht0_ref, ht1_ref)

    @pl.when((j_out == nj - 1) & (s >= 2))
    def _():
        y_ref[...] = h_ref[...] + acc_ref[...].T


def _expert_table_kernel(u_ref, v_ref, ub_ref, vt_ref):
    ub_ref[...] = u_ref[...].astype(BF16)
    vt_ref[...] = v_ref[...].T.astype(BF16)


def _expert_tables(expert_u, expert_v, layer, eb, interpret):
    nj = N_EXPERTS // eb
    blk = pl.BlockSpec((eb, D_MODEL), lambda j: (j, 0))
    layer_blk = pl.BlockSpec((None, eb, D_MODEL), lambda j: (layer, j, 0))
    return pl.pallas_call(
        _expert_table_kernel,
        out_shape=(jax.ShapeDtypeStruct((N_EXPERTS, D_MODEL), BF16),
                   jax.ShapeDtypeStruct((nj, D_MODEL, eb), BF16)),
        grid=(nj,),
        in_specs=[layer_blk, layer_blk],
        out_specs=(blk, pl.BlockSpec((None, D_MODEL, eb), lambda j: (j, 0, 0))),
        compiler_params=_cparams(("arbitrary",)),
        name="expert_tables",
        interpret=interpret,
    )(expert_u, expert_v)


def _peer(hnt, h, e1, e2, th, u, vt, interpret):
    t = hnt.shape[1]
    tm, eb = TM_PEER, EB_PEER
    assert eb == D_MODEL
    nj = N_EXPERTS // eb
    n_pairs = (t // tm) * nj
    tile = lambda s: s // nj
    blk = lambda s: lax.rem(s, nj)
    s1 = lambda s: jnp.minimum(s, n_pairs - 1)
    s2 = lambda s: jnp.clip(s - 1, 0, n_pairs - 1)
    s3 = lambda s: jnp.maximum(s - 2, 0)
    return pl.pallas_call(
        functools.partial(_peer_kernel, nj=nj),
        out_shape=jax.ShapeDtypeStruct((t, D_MODEL), F32),
        grid=(n_pairs + 2,),
        in_specs=[pl.BlockSpec((D_MODEL, tm), lambda s: (0, tile(s1(s)))),
                  pl.BlockSpec((eb, D_MODEL), lambda s: (blk(s1(s)), 0)),
                  pl.BlockSpec((None, D_MODEL, eb), lambda s: (blk(s3(s)), 0, 0)),
                  pl.BlockSpec((eb // SB_PEER, 2 * PEER_HEADS, tm),
                               lambda s: (blk(s2(s)), 0, tile(s2(s)))),
                  pl.BlockSpec((PEER_HEADS, N_KEYS, tm), lambda s: (0, 0, tile(s2(s)))),
                  pl.BlockSpec((PEER_HEADS, tm), lambda s: (0, tile(s2(s)))),
                  pl.BlockSpec((tm, D_MODEL), lambda s: (tile(s3(s)), 0))],
        out_specs=pl.BlockSpec((tm, D_MODEL), lambda s: (tile(s3(s)), 0)),
        scratch_shapes=[pltpu.VMEM((eb, tm), F32), pltpu.VMEM((eb, tm), F32),
                        pltpu.VMEM((eb, tm), BF16), pltpu.VMEM((eb, tm), BF16),
                        pltpu.VMEM((D_MODEL, tm), F32), pltpu.VMEM((8, LANES), F32)],
        compiler_params=_cparams(("arbitrary",)),
        name="peer_experts",
        interpret=interpret,
    )(hnt, u, vt, e1, e2, th, h)


def _hybrid_block(x, g_mix, w_in, q_gain, k_gain, w_fourier, w_attn, w_out, g_ffn, w_query,
                  sub_keys, expert_u_b, expert_vt_b, interpret=False):
    b, s, _ = x.shape
    x2 = x.reshape(b * s, D_MODEL)
    fa, fb, q, k, v, sga, sgb = _proj(x2, s, g_mix, w_in, q_gain, k_gain, interpret)
    o_list, ld_list = [], []
    for gi, (_, dilation) in enumerate(ATTN_GROUPS):
        o, ld = _attn_group(q[gi], k[gi], v[gi], dilation, interpret)
        o_list.append(o)
        ld_list.append(ld)
    yf = _seq_dft(fa, fb, b, s, interpret)
    h, hnt, st = _mix(x2, o_list, ld_list, yf, sga, sgb, w_attn, w_fourier, w_out, g_ffn,
                      w_query, sub_keys, interpret)
    e1, e2, th = _topk(st, interpret)
    y = _peer(hnt, h, e1, e2, th, expert_u_b, expert_vt_b, interpret)
    return y.reshape(b, s, D_MODEL)


def kernel(x_prompt, x_sample, g_mix, w_in, q_gain, k_gain, w_fourier, w_attn, w_out, g_ffn,
           w_query, sub_keys, expert_u, expert_v):
    y_prompt, y_sample = x_prompt, x_sample
    for layer in range(g_mix.shape[0]):
        params = (g_mix[layer], w_in[layer], q_gain[layer], k_gain[layer], w_fourier[layer],
                  w_attn[layer], w_out[layer], g_ffn[layer], w_query[layer], sub_keys[layer],
                  *_expert_tables(expert_u, expert_v, layer, EB_PEER, False))
        y_prompt = _hybrid_block(y_prompt, *params)
        y_sample = _hybrid_block(y_sample, *params)
    return (y_prompt, y_sample)
```

```python
import functools
import math

import jax
import jax.numpy as jnp
from jax import lax
from jax.experimental import pallas as pl
from jax.experimental.pallas import tpu as pltpu

F32 = jnp.float32
BF16 = jnp.bfloat16

D_MODEL = 1024
FOURIER_GROUPS = 4
FOURIER_GROUP_DIM = 128
FOURIER_WIDTH = FOURIER_GROUPS * FOURIER_GROUP_DIM
ATTN_GROUPS = ((128, 1), (512, 4), (2048, 16))
N_ATTN_GROUPS = 3
HEADS_PER_GROUP = 4
HEAD_DIM = 128
ATTN_GROUP_WIDTH = HEADS_PER_GROUP * HEAD_DIM
ATTN_WIDTH = N_ATTN_GROUPS * ATTN_GROUP_WIDTH
ROPE_DIM = HEAD_DIM // 4
ROPE_THETA = 500000.0
N_KEYS = 128
N_EXPERTS = N_KEYS * N_KEYS
PEER_HEADS = 8
PEER_TOPK = 16
PEER_KEY_DIM = 256
NORM_EPS = 1e-6
NEG_INF = -1e30

LANES = 128
VMEM_LIMIT = 56 * 1024 * 1024

TM_PROJ = 512
TM_MIX = 512
TM_TOPK = 512
TM_PEER = 512
EB_PEER = 2048
QB_ATTN = 128
DFT_TS = 1024
DFT_TK = 2048


def _cparams(sem):
    return pltpu.CompilerParams(dimension_semantics=sem, vmem_limit_bytes=VMEM_LIMIT)


def _const_spec(shape):
    nd = len(shape)
    return pl.BlockSpec(shape, lambda *_: (0,) * nd)


def _proj_kernel(x_ref, g_ref, wf_ref, wq_ref, wk_ref, wv_ref, wga_ref, wgb_ref,
                 qg_ref, kg_ref, cc_ref, sc_ref, rc_ref, rs1_ref, rs2_ref,
                 fa_ref, fb_ref, q0_ref, q1_ref, q2_ref, k0_ref, k1_ref, k2_ref,
                 v0_ref, v1_ref, v2_ref, sga_ref, sgb_ref, stage_ref):
    tm = x_ref.shape[0]
    hpg = HEADS_PER_GROUP
    stage_slot = [0]

    def store_group(out_ref, gi, cols):
        dilation = ATTN_GROUPS[gi][1]
        if dilation == 1:
            for hl, c in enumerate(cols):
                out_ref[0, :, hl * HEAD_DIM:(hl + 1) * HEAD_DIM] = c.astype(BF16)
            return
        slot = stage_slot[0]
        stage_slot[0] += 1
        for hl, c in enumerate(cols):
            stage_ref[slot, hl] = c
            for r in range(dilation):
                out_ref[r, :, hl * HEAD_DIM:(hl + 1) * HEAD_DIM] = stage_ref[
                    slot, hl, pl.ds(r, tm // dilation, stride=dilation), :].astype(BF16)

    x = x_ref[...]
    ms = jnp.mean(x * x, axis=-1, keepdims=True)
    xn = (x * lax.rsqrt(ms + NORM_EPS) * g_ref[...]).astype(BF16)

    f = jnp.dot(xn, wf_ref[...], preferred_element_type=F32).astype(BF16)
    fa_ref[...] = jnp.dot(f, cc_ref[...], preferred_element_type=F32).astype(BF16)
    fb_ref[...] = jnp.dot(f, sc_ref[...], preferred_element_type=F32).astype(BF16)

    rc = rc_ref[...]
    rs1 = rs1_ref[...]
    rs2 = rs2_ref[...]

    def norm_rope(w_ref, gain_ref, out_refs, scale):
        y = jnp.dot(xn, w_ref[...], preferred_element_type=F32)
        for gi in range(N_ATTN_GROUPS):
            cols = []
            for hh in range(gi * hpg, (gi + 1) * hpg):
                yh = y[:, hh * HEAD_DIM:(hh + 1) * HEAD_DIM]
                hms = jnp.mean(yh * yh, axis=-1, keepdims=True)
                yn = yh * lax.rsqrt(hms + NORM_EPS) * gain_ref[gi:gi + 1, :]
                r = (yn * rc + pltpu.roll(yn, LANES - ROPE_DIM // 2, 1) * rs1
                     + pltpu.roll(yn, ROPE_DIM // 2, 1) * rs2)
                cols.append(r * scale)
            store_group(out_refs[gi], gi, cols)

    norm_rope(wq_ref, qg_ref, (q0_ref, q1_ref, q2_ref), HEAD_DIM ** -0.5)
    norm_rope(wk_ref, kg_ref, (k0_ref, k1_ref, k2_ref), 1.0)
    yv = jnp.dot(xn, wv_ref[...], preferred_element_type=F32)
    for gi, out_ref in enumerate((v0_ref, v1_ref, v2_ref)):
        store_group(out_ref, gi, [yv[:, hh * HEAD_DIM:(hh + 1) * HEAD_DIM]
                                  for hh in range(gi * hpg, (gi + 1) * hpg)])
    sga_ref[...] = jax.nn.sigmoid(
        jnp.dot(xn, wga_ref[...], preferred_element_type=F32)).astype(BF16)
    sgb_ref[...] = jax.nn.sigmoid(
        jnp.dot(xn, wgb_ref[...], preferred_element_type=F32)).astype(BF16)


def _rope_tables(s):
    half = ROPE_DIM // 2
    inv_freq = ROPE_THETA ** (-jnp.arange(half, dtype=F32) * 2.0 / ROPE_DIM)
    ang = jnp.arange(s, dtype=F32)[:, None] * inv_freq[None, :]
    cos, sin = jnp.cos(ang), jnp.sin(ang)
    zeros = jnp.zeros((s, HEAD_DIM - ROPE_DIM), F32)
    zh = jnp.zeros((s, half), F32)
    rc = jnp.concatenate([cos, cos, jnp.ones((s, HEAD_DIM - ROPE_DIM), F32)], axis=1)
    rs1 = jnp.concatenate([-sin, zh, zeros], axis=1)
    rs2 = jnp.concatenate([zh, sin, zeros], axis=1)
    return rc, rs1, rs2


def _channel_dft_mats():
    n = FOURIER_GROUP_DIM
    idx = (jnp.arange(n)[:, None] * jnp.arange(n)[None, :]) % n
    ang = idx.astype(F32) * (2.0 * math.pi / n)
    eye = jnp.eye(FOURIER_GROUPS, dtype=F32)
    cc = jnp.kron(eye, jnp.cos(ang) * n ** -0.5)
    sc = jnp.kron(eye, jnp.sin(ang) * n ** -0.5)
    return cc.astype(BF16), sc.astype(BF16)


def _proj(x2, s, g_mix, w_in, q_gain, k_gain, interpret):
    t = x2.shape[0]
    tm = TM_PROJ
    c0 = FOURIER_WIDTH
    c1 = c0 + ATTN_WIDTH
    c2 = c1 + ATTN_WIDTH
    c3 = c2 + ATTN_WIDTH
    c4 = c3 + D_MODEL
    wb = w_in.astype(BF16)
    wf, wq, wk, wv, wga, wgb = (wb[:, :c0], wb[:, c0:c1], wb[:, c1:c2], wb[:, c2:c3],
                                wb[:, c3:c4], wb[:, c4:])
    cc, sc = _channel_dft_mats()
    rc, rs1, rs2 = _rope_tables(s)
    spt = s // tm
    b = t // s
    tok = lambda w: pl.BlockSpec((tm, w), lambda i: (i, 0))
    pos = pl.BlockSpec((tm, HEAD_DIM), lambda i: (i % spt, 0))
    grp_shapes = [jax.ShapeDtypeStruct((b, d, s // d, ATTN_GROUP_WIDTH), BF16)
                  for _, d in ATTN_GROUPS]
    grp_specs = [pl.BlockSpec((None, d, tm // d, ATTN_GROUP_WIDTH),
                              lambda i: (i // spt, 0, i % spt, 0)) for _, d in ATTN_GROUPS]
    out_shape = ([jax.ShapeDtypeStruct((t, FOURIER_WIDTH), BF16)] * 2 + grp_shapes * 3
                 + [jax.ShapeDtypeStruct((t, D_MODEL), BF16)] * 2)
    n_staged = 3 * sum(1 for _, d in ATTN_GROUPS if d > 1)
    outs = pl.pallas_call(
        _proj_kernel,
        out_shape=out_shape,
        grid=(t // tm,),
        in_specs=[tok(D_MODEL), _const_spec((1, D_MODEL)),
                  _const_spec(wf.shape), _const_spec(wq.shape), _const_spec(wk.shape),
                  _const_spec(wv.shape), _const_spec(wga.shape), _const_spec(wgb.shape),
                  _const_spec(q_gain.shape), _const_spec(k_gain.shape),
                  _const_spec(cc.shape), _const_spec(sc.shape), pos, pos, pos],
        out_specs=[tok(FOURIER_WIDTH)] * 2 + grp_specs * 3 + [tok(D_MODEL)] * 2,
        scratch_shapes=[pltpu.VMEM((n_staged, HEADS_PER_GROUP, tm, HEAD_DIM), F32)],
        compiler_params=_cparams(("arbitrary",)),
        name="proj",
        interpret=interpret,
    )(x2, g_mix.reshape(1, D_MODEL), wf, wq, wk, wv, wga, wgb, q_gain, k_gain, cc, sc,
      rc, rs1, rs2)
    fa, fb = outs[0], outs[1]
    q, k, v = outs[2:5], outs[5:8], outs[8:11]
    return fa, fb, q, k, v, outs[11], outs[12]


LD_LANES = LANES // HEADS_PER_GROUP


def _attn_kernel(q_ref, k_ref, v_ref, o_ref, ld_ref, *, sub_len, heads):
    kb = min(2 * QB_ATTN, sub_len)
    span = QB_ATTN // 2
    first_head = pl.program_id(2) * heads
    residues = q_ref.shape[0]

    if heads < HEADS_PER_GROUP:
        @pl.when(first_head == 0)
        def _():
            ld_ref[...] = jnp.zeros_like(ld_ref)

    def body(i, carry):
        start = pl.multiple_of(i * QB_ATTN, QB_ATTN)
        ks = pl.multiple_of(jnp.clip(start - span, 0, sub_len - kb), span)
        qpos = start + lax.broadcasted_iota(jnp.int32, (QB_ATTN, kb), 0)
        kpos = ks + lax.broadcasted_iota(jnp.int32, (QB_ATTN, kb), 1)
        in_band = jnp.abs(kpos - qpos) <= span
        lane_head = lax.broadcasted_iota(jnp.int32, (QB_ATTN, LANES), 1) // LD_LANES
        for g in range(residues):
            if heads < HEADS_PER_GROUP:
                ld_all = ld_ref[g, pl.ds(start, QB_ATTN), :]
            else:
                ld_all = jnp.zeros((QB_ATTN, LANES), F32)
            for hl in range(heads):
                cols = slice(hl * HEAD_DIM, (hl + 1) * HEAD_DIM)
                q = q_ref[g, pl.ds(start, QB_ATTN), cols]
                k = k_ref[g, pl.ds(ks, kb), cols]
                v = v_ref[g, pl.ds(ks, kb), cols]
                s = lax.dot_general(q, k, (((1,), (1,)), ((), ())),
                                    preferred_element_type=F32)
                s = jnp.where(in_band, s, NEG_INF)
                m = jnp.max(s, axis=1, keepdims=True)
                p = jnp.exp(s - m)
                l = jnp.sum(p, axis=1, keepdims=True)
                o = jnp.dot(p.astype(BF16), v, preferred_element_type=F32) / l
                o_ref[g, pl.ds(start, QB_ATTN), cols] = o.astype(BF16)
                ld_all = jnp.where(lane_head == first_head + hl, m + jnp.log(l), ld_all)
            ld_ref[g, pl.ds(start, QB_ATTN), :] = ld_all
        return carry

    nq = sub_len // QB_ATTN
    lax.fori_loop(0, nq, body, 0, unroll=max(1, min(nq, ATTN_CHAINS // (heads * residues))))


ATTN_MAX_WIDE_LEN = 2048
ATTN_CHAINS = 16


def _attn_group(q, k, v, dilation, interpret):
    b, _, sub_len, _ = q.shape
    assert sub_len % QB_ATTN == 0
    heads = HEADS_PER_GROUP if sub_len <= ATTN_MAX_WIDE_LEN else 1
    width = heads * HEAD_DIM
    residues = max(1, min(dilation, ATTN_CHAINS * QB_ATTN // (heads * sub_len)))
    spec = pl.BlockSpec((None, residues, sub_len, width), lambda bi, r, hb: (bi, r, 0, hb))
    return pl.pallas_call(
        functools.partial(_attn_kernel, sub_len=sub_len, heads=heads),
        out_shape=(jax.ShapeDtypeStruct(q.shape, BF16),
                   jax.ShapeDtypeStruct((b, dilation, sub_len, LANES), F32)),
        grid=(b, dilation // residues, HEADS_PER_GROUP // heads),
        in_specs=[spec, spec, spec],
        out_specs=(spec, pl.BlockSpec((None, residues, sub_len, LANES),
                                      lambda bi, r, hb: (bi, r, 0, 0))),
        compiler_params=_cparams(("arbitrary", "arbitrary", "arbitrary")),
        name=f"attn_d{dilation}",
        interpret=interpret,
    )(q, k, v)


def _dft_kernel(cs_ref, ss_ref, a_ref, b_ref, y_ref, acc_ref):
    kk = pl.program_id(2)

    @pl.when(kk == 0)
    def _():
        acc_ref[...] = jnp.zeros_like(acc_ref)

    acc_ref[...] += (jnp.dot(cs_ref[...], a_ref[...], preferred_element_type=F32)
                     - jnp.dot(ss_ref[...], b_ref[...], preferred_element_type=F32))

    @pl.when(kk == pl.num_programs(2) - 1)
    def _():
        y_ref[...] = acc_ref[...].astype(BF16)


def _dft_mats_kernel(ch_ref, sh_ref, cl_ref, sl_ref, cs_ref, ss_ref):
    cl, sl = cl_ref[...], sl_ref[...]
    for k in range(ch_ref.shape[1]):
        c, s_ = ch_ref[:, k:k + 1], sh_ref[:, k:k + 1]
        cols = slice(k * LANES, (k + 1) * LANES)
        cs_ref[:, cols] = (c * cl - s_ * sl).astype(BF16)
        ss_ref[:, cols] = (s_ * cl + c * sl).astype(BF16)


DFT_GEN_ROWS = 256


def _seq_dft_mats(s, interpret):
    lo = LANES
    hi = s // lo
    rows = jnp.arange(s, dtype=jnp.int32)[:, None]
    scale = 2.0 * math.pi / s
    ang_hi = ((rows * (jnp.arange(hi, dtype=jnp.int32)[None, :] * lo)) % s).astype(F32) * scale
    ang_lo = ((rows * jnp.arange(lo, dtype=jnp.int32)[None, :]) % s).astype(F32) * scale
    norm = s ** -0.5
    tr = DFT_GEN_ROWS
    coarse = pl.BlockSpec((tr, hi), lambda i: (i, 0))
    fine = pl.BlockSpec((tr, lo), lambda i: (i, 0))
    out = pl.BlockSpec((tr, s), lambda i: (i, 0))
    return pl.pallas_call(
        _dft_mats_kernel,
        out_shape=(jax.ShapeDtypeStruct((s, s), BF16),) * 2,
        grid=(s // tr,),
        in_specs=[coarse, coarse, fine, fine],
        out_specs=(out, out),
        compiler_params=_cparams(("arbitrary",)),
        name="dft_mats",
        interpret=interpret,
    )(jnp.cos(ang_hi), jnp.sin(ang_hi), jnp.cos(ang_lo) * norm, jnp.sin(ang_lo) * norm)


def _seq_dft(fa, fb, b, s, interpret):
    ts = min(DFT_TS, s)
    tk = min(DFT_TK, s)
    cs, ss = _seq_dft_mats(s, interpret)
    mat = pl.BlockSpec((ts, tk), lambda bi, i, kk: (i, kk))
    vec = pl.BlockSpec((None, tk, FOURIER_WIDTH), lambda bi, i, kk: (bi, kk, 0))
    y = pl.pallas_call(
        _dft_kernel,
        out_shape=jax.ShapeDtypeStruct((b, s, FOURIER_WIDTH), BF16),
        grid=(b, s // ts, s // tk),
        in_specs=[mat, mat, vec, vec],
        out_specs=pl.BlockSpec((None, ts, FOURIER_WIDTH), lambda bi, i, kk: (bi, i, 0)),
        scratch_shapes=[pltpu.VMEM((ts, FOURIER_WIDTH), F32)],
        compiler_params=_cparams(("arbitrary", "arbitrary", "arbitrary")),
        name="seq_dft",
        interpret=interpret,
    )(cs, ss, fa.reshape(b, s, FOURIER_WIDTH), fb.reshape(b, s, FOURIER_WIDTH))
    return y.reshape(b * s, FOURIER_WIDTH)


def _mix_kernel(x_ref, o0_ref, o1_ref, o2_ref, l0_ref, l1_ref, l2_ref, y_ref, sga_ref, sgb_ref,
                wat_ref, wfo_ref, wout_ref, gf_ref, wqy_ref, sk_ref, hx_ref,
                h_ref, hnt_ref, st_ref, ostage_ref, lstage_ref):
    tm = x_ref.shape[0]

    def token_order(ref, stage, gi):
        dilation = ATTN_GROUPS[gi][1]
        if dilation == 1:
            return ref[0].astype(F32)
        cols = []
        for c in range(ref.shape[-1] // LANES):
            for r in range(dilation):
                stage[gi - 1, c, pl.ds(r, tm // dilation, stride=dilation), :] = (
                    ref[r, :, c * LANES:(c + 1) * LANES].astype(F32))
            cols.append(stage[gi - 1, c])
        return cols[0] if len(cols) == 1 else jnp.concatenate(cols, axis=1)

    lds = [token_order(ref, lstage_ref, gi) for gi, ref in enumerate((l0_ref, l1_ref, l2_ref))]
    mx = jnp.maximum(jnp.maximum(lds[0], lds[1]), lds[2])
    ws = [jnp.exp(ld - mx) for ld in lds]
    inv = 1.0 / (ws[0] + ws[1] + ws[2])
    attn = jnp.zeros((tm, ATTN_GROUP_WIDTH), F32)
    for gi, ref in enumerate((o0_ref, o1_ref, o2_ref)):
        wexp = jnp.dot((ws[gi] * inv).astype(BF16), hx_ref[...], preferred_element_type=F32)
        attn = attn + wexp * token_order(ref, ostage_ref, gi)
    attn = jnp.dot(attn.astype(BF16), wat_ref[...], preferred_element_type=F32)
    four = jnp.dot(y_ref[...], wfo_ref[...], preferred_element_type=F32)
    mix = sga_ref[...].astype(F32) * four + sgb_ref[...].astype(F32) * attn
    h = x_ref[...] + jnp.dot(mix.astype(BF16), wout_ref[...], preferred_element_type=F32)
    h_ref[...] = h
    ms = jnp.mean(h * h, axis=-1, keepdims=True)
    hn32 = h * lax.rsqrt(ms + NORM_EPS) * gf_ref[...]
    hn = hn32.astype(BF16)
    hnt_ref[...] = hn32.T.astype(BF16)
    qy = jnp.dot(hn, wqy_ref[...], preferred_element_type=F32).astype(BF16)
    half = PEER_KEY_DIM // 2
    for hc in range(2 * PEER_HEADS):
        st_ref[hc] = lax.dot_general(sk_ref[hc], qy[:, hc * half:(hc + 1) * half],
                                     (((1,), (1,)), ((), ())), preferred_element_type=F32)


def _mix(x2, o_list, ld_list, yf, sga, sgb, w_attn, w_fourier, w_out, g_ffn, w_query, sub_keys,
         interpret):
    t = x2.shape[0]
    tm = TM_MIX
    spt = o_list[0].shape[2] // tm
    tok = lambda w: pl.BlockSpec((tm, w), lambda i: (i, 0))
    grp = lambda w: [pl.BlockSpec((None, d, tm // d, w), lambda i: (i // spt, 0, i % spt, 0))
                     for _, d in ATTN_GROUPS]
    head_expand = (jnp.arange(LANES)[:, None]
                   == LD_LANES * (jnp.arange(ATTN_GROUP_WIDTH)[None, :] // HEAD_DIM)).astype(BF16)
    n_staged = sum(1 for _, d in ATTN_GROUPS if d > 1)
    sk = jnp.stack([sub_keys[:, 0][:, jnp.array(_PAIR_ORDER)], sub_keys[:, 1]], axis=1)
    sk = sk.reshape(2 * PEER_HEADS, N_KEYS, PEER_KEY_DIM // 2).astype(BF16)
    wat, wfo, wout, wqy = (w_attn.astype(BF16), w_fourier.astype(BF16), w_out.astype(BF16),
                           w_query.astype(BF16))
    return pl.pallas_call(
        _mix_kernel,
        out_shape=(jax.ShapeDtypeStruct((t, D_MODEL), F32),
                   jax.ShapeDtypeStruct((D_MODEL, t), BF16),
                   jax.ShapeDtypeStruct((2 * PEER_HEADS, N_KEYS, t), F32)),
        grid=(t // tm,),
        in_specs=[tok(D_MODEL)] + grp(ATTN_GROUP_WIDTH) + grp(LANES) + [tok(FOURIER_WIDTH),
                  tok(D_MODEL), tok(D_MODEL),
                  _const_spec(wat.shape), _const_spec(wfo.shape), _const_spec(wout.shape),
                  _const_spec((1, D_MODEL)), _const_spec(wqy.shape), _const_spec(sk.shape),
                  _const_spec(head_expand.shape)],
        out_specs=(tok(D_MODEL), pl.BlockSpec((D_MODEL, tm), lambda i: (0, i)),
                   pl.BlockSpec((2 * PEER_HEADS, N_KEYS, tm), lambda i: (0, 0, i))),
        scratch_shapes=[pltpu.VMEM((n_staged, HEADS_PER_GROUP, tm, HEAD_DIM), F32),
                        pltpu.VMEM((n_staged, 1, tm, LANES), F32)],
        compiler_params=_cparams(("arbitrary",)),
        name="mix",
        interpret=interpret,
    )(x2, *o_list, *ld_list, yf, sga, sgb, wat, wfo, wout, g_ffn.reshape(1, D_MODEL), wqy, sk,
      head_expand)


N_PAIRS = N_KEYS // 2
_PAIR_ORDER = tuple(range(0, N_KEYS, 2)) + tuple(range(1, N_KEYS, 2))
SUBLANES = 8


def _batcher_network(n):
    def merge(lo, hi, r):
        step = r * 2
        if step < hi - lo:
            yield from merge(lo, hi, step)
            yield from merge(lo + r, hi, step)
            for i in range(lo + r, hi - r, step):
                yield (i, i + r)
        else:
            yield (lo, lo + r)

    def sort(lo, hi):
        if hi - lo >= 1:
            mid = lo + (hi - lo) // 2
            yield from sort(lo, mid)
            yield from sort(mid + 1, hi)
            yield from merge(lo, hi, 1)

    return tuple(sort(0, n - 1))


_SORT16 = _batcher_network(PEER_TOPK)


def _compare_exchange(v, i, j):
    v[i], v[j] = jnp.maximum(v[i], v[j]), jnp.minimum(v[i], v[j])


def _top16_sorted(keys):
    n = PEER_TOPK
    v = [keys[SUBLANES * k:SUBLANES * (k + 1)] for k in range(n)]
    for i, j in _SORT16:
        _compare_exchange(v, i, j)
    shift = SUBLANES // 2
    while shift >= 1:
        other = [pltpu.roll(x, shift, 0) for x in v]
        v = [jnp.maximum(v[k], other[n - 1 - k]) for k in range(n)]
        d = n // 2
        while d >= 1:
            for i in range(n):
                if not i & d:
                    _compare_exchange(v, i, i + d)
            d //= 2
        shift //= 2
    return [x[0:1] for x in v]


_CAND_PAIRS = tuple((i, j) for i in range(PEER_TOPK) for j in range(PEER_TOPK)
                    if (i + 1) * (j + 1) <= PEER_TOPK)
_CAND_ROWS = -(-len(_CAND_PAIRS) // SUBLANES) * SUBLANES
TIE_STEP = 2.0 ** -17


def _kth_slot_value(vals, k):
    slot = lax.broadcasted_iota(jnp.int32, vals.shape, 0).astype(F32)
    bumped = vals * (1.0 + slot * TIE_STEP)
    cur = bumped
    m = None
    for _ in range(k):
        m = jnp.max(cur, axis=0, keepdims=True)
        cur = jnp.where(cur == m, -1.0, cur)
    return bumped, m


def _candidates(fa, fb):
    rows = [fa[i] * fb[j] for i, j in _CAND_PAIRS]
    rows.append(jnp.zeros((_CAND_ROWS - len(rows),) + rows[0].shape[1:], rows[0].dtype))
    return jnp.concatenate(rows, axis=0)


def _topk_kernel(st_ref, e1_ref, e2_ref, th_ref):
    tm = st_ref.shape[2]
    for h in range(PEER_HEADS):
        for tc in range(tm // LANES):
            lanes = slice(tc * LANES, (tc + 1) * LANES)
            s1 = st_ref[2 * h, :, lanes]
            s2 = st_ref[2 * h + 1, :, lanes]
            a = _top16_sorted(s1)
            b = _top16_sorted(s2)
            ea = [jnp.exp(ai - a[0]) for ai in a]
            eb = [jnp.exp(bi - b[0]) for bi in b]
            cand = _candidates(ea, eb)
            bumped, kth = _kth_slot_value(cand, PEER_TOPK)
            z = jnp.sum(jnp.where(bumped >= kth, cand, 0.0), axis=0, keepdims=True)
            inv_z = 1.0 / z
            cand_b = _candidates([(x * inv_z).astype(BF16) for x in ea],
                                 [x.astype(BF16) for x in eb]).astype(F32)
            _, kth_b = _kth_slot_value(cand_b, PEER_TOPK)
            th_ref[h:h + 1, lanes] = kth_b.astype(BF16).astype(F32)
            e1 = jnp.exp(s1 - a[0]) * inv_z
            e1_ref[:, 2 * h, lanes] = e1[:N_PAIRS]
            e1_ref[:, 2 * h + 1, lanes] = e1[N_PAIRS:]
            e2_ref[h, :, lanes] = jnp.exp(s2 - b[0]).astype(BF16)


def _topk(st, interpret):
    t = st.shape[2]
    tm = TM_TOPK
    blk = lambda n: pl.BlockSpec((n, N_KEYS, tm), lambda i: (0, 0, i))
    return pl.pallas_call(
        _topk_kernel,
        out_shape=(jax.ShapeDtypeStruct((N_PAIRS, 2 * PEER_HEADS, t), F32),
                   jax.ShapeDtypeStruct((PEER_HEADS, N_KEYS, t), BF16),
                   jax.ShapeDtypeStruct((PEER_HEADS, t), F32)),
        grid=(t // tm,),
        in_specs=[blk(2 * PEER_HEADS)],
        out_specs=(pl.BlockSpec((N_PAIRS, 2 * PEER_HEADS, tm), lambda i: (0, 0, i)),
                   blk(PEER_HEADS),
                   pl.BlockSpec((PEER_HEADS, tm), lambda i: (0, i))),
        compiler_params=_cparams(("arbitrary",)),
        name="peer_topk",
        interpret=interpret,
    )(st)


SB_PEER = 2 * N_KEYS
KB_PEER = 32


def _peer_kernel(hnt_ref, u_ref, vt_ref, e1_ref, e2_ref, th_ref, h_ref, y_ref,
                 at0_ref, at1_ref, ht0_ref, ht1_ref, acc_ref, join_ref, *, nj):
    s = pl.program_id(0)
    tm = hnt_ref.shape[1]
    j_out = lax.rem(jnp.maximum(s - 2, 0), nj)

    @pl.when(s == 0)
    def _():
        for ref in (at0_ref, at1_ref, ht0_ref, ht1_ref):
            ref[...] = jnp.zeros_like(ref)

    @pl.when(j_out == 0)
    def _():
        acc_ref[...] = jnp.zeros_like(acc_ref)

    def stages(at_w, at_r, ht_w, ht_r):
        at_new = jnp.dot(u_ref[...], hnt_ref[...], preferred_element_type=F32)
        at_w[...] = at_new
        join = at_new[at_new.shape[0] - 8:, tm - LANES:]
        prev_zero = jnp.zeros((KB_PEER, LANES), BF16)
        for key_pair in range(u_ref.shape[0] // SB_PEER):
            for par in range(2):
                for tc in range(tm // LANES):
                    lanes = slice(tc * LANES, (tc + 1) * LANES)
                    e1b = [jnp.broadcast_to(
                        e1_ref[key_pair, 2 * h + par:2 * h + par + 1, lanes].astype(BF16),
                        (KB_PEER, LANES)) for h in range(PEER_HEADS)]
                    thb = [jnp.broadcast_to(th_ref[h:h + 1, lanes].astype(BF16),
                                            (KB_PEER, LANES)) for h in range(PEER_HEADS)]
                    for kb in range(N_KEYS // KB_PEER):
                        keys = slice(kb * KB_PEER, (kb + 1) * KB_PEER)
                        ra = key_pair * SB_PEER + par * N_KEYS + kb * KB_PEER
                        rows = slice(ra, ra + KB_PEER)
                        gate = prev_zero
                        for h in range(PEER_HEADS):
                            p = e1b[h] * e2_ref[h, keys, lanes]
                            gate = gate + jnp.where(p >= thb[h], p, jnp.zeros_like(p))
                        act = jax.nn.gelu(at_r[rows, lanes].astype(BF16)) * gate
                        ht_w[rows, lanes] = act
                        join = join + act[:8].astype(F32)
                        bits = pltpu.bitcast(act, jnp.uint32)
                        prev_zero = pltpu.bitcast((bits >> 16) >> 16, BF16)
        out_new = jnp.dot(vt_ref[...], ht_r[...], preferred_element_type=F32)
        acc_ref[...] += out_new
        join_ref[...] = join + out_new[D_MODEL - 8:, tm - LANES:]

    parity = lax.rem(s, 2)

    @pl.when(parity == 0)
    def _():
        stages(at0_ref, at1_ref, ht1_ref, ht0_ref)

    @pl.when(parity == 1)
    def _():
        stages(at1_ref, at0_ref, ht0_ref, ht1_ref)

    @pl.when((j_out == nj - 1) & (s >= 2))
    def _():
        y_ref[...] = h_ref[...] + acc_ref[...].T


def _expert_table_kernel(u_ref, v_ref, ub_ref, vt_ref):
    ub_ref[...] = u_ref[...].astype(BF16)
    vt_ref[...] = v_ref[...].T.astype(BF16)


def _expert_tables(expert_u, expert_v, layer, eb, interpret):
    nj = N_EXPERTS // eb
    blk = pl.BlockSpec((eb, D_MODEL), lambda j: (j, 0))
    layer_blk = pl.BlockSpec((None, eb, D_MODEL), lambda j: (layer, j, 0))
    return pl.pallas_call(
        _expert_table_kernel,
        out_shape=(jax.ShapeDtypeStruct((N_EXPERTS, D_MODEL), BF16),
                   jax.ShapeDtypeStruct((nj, D_MODEL, eb), BF16)),
        grid=(nj,),
        in_specs=[layer_blk, layer_blk],
        out_specs=(blk, pl.BlockSpec((None, D_MODEL, eb), lambda j: (j, 0, 0))),
        compiler_params=_cparams(("arbitrary",)),
        name="expert_tables",
        interpret=interpret,
    )(expert_u, expert_v)


def _peer(hnt, h, e1, e2, th, u, vt, interpret):
    t = hnt.shape[1]
    tm, eb = TM_PEER, EB_PEER
    nj = N_EXPERTS // eb
    n_pairs = (t // tm) * nj
    tile = lambda s: s // nj
    blk = lambda s: lax.rem(s, nj)
    s1 = lambda s: jnp.minimum(s, n_pairs - 1)
    s2 = lambda s: jnp.clip(s - 1, 0, n_pairs - 1)
    s3 = lambda s: jnp.maximum(s - 2, 0)
    return pl.pallas_call(
        functools.partial(_peer_kernel, nj=nj),
        out_shape=jax.ShapeDtypeStruct((t, D_MODEL), F32),
        grid=(n_pairs + 2,),
        in_specs=[pl.BlockSpec((D_MODEL, tm), lambda s: (0, tile(s1(s)))),
                  pl.BlockSpec((eb, D_MODEL), lambda s: (blk(s1(s)), 0)),
                  pl.BlockSpec((None, D_MODEL, eb), lambda s: (blk(s3(s)), 0, 0)),
                  pl.BlockSpec((eb // SB_PEER, 2 * PEER_HEADS, tm),
                               lambda s: (blk(s2(s)), 0, tile(s2(s)))),
                  pl.BlockSpec((PEER_HEADS, N_KEYS, tm), lambda s: (0, 0, tile(s2(s)))),
                  pl.BlockSpec((PEER_HEADS, tm), lambda s: (0, tile(s2(s)))),
                  pl.BlockSpec((tm, D_MODEL), lambda s: (tile(s3(s)), 0))],
        out_specs=pl.BlockSpec((tm, D_MODEL), lambda s: (tile(s3(s)), 0)),
        scratch_shapes=[pltpu.VMEM((eb, tm), F32), pltpu.VMEM((eb, tm), F32),
                        pltpu.VMEM((eb, tm), BF16), pltpu.VMEM((eb, tm), BF16),
                        pltpu.VMEM((D_MODEL, tm), F32), pltpu.VMEM((8, LANES), F32)],
        compiler_params=_cparams(("arbitrary",)),
        name="peer_experts",
        interpret=interpret,
    )(hnt, u, vt, e1, e2, th, h)


def _hybrid_block(x, g_mix, w_in, q_gain, k_gain, w_fourier, w_attn, w_out, g_ffn, w_query,
                  sub_keys, expert_u_b, expert_vt_b, interpret=False):
    b, s, _ = x.shape
    x2 = x.reshape(b * s, D_MODEL)
    fa, fb, q, k, v, sga, sgb = _proj(x2, s, g_mix, w_in, q_gain, k_gain, interpret)
    o_list, ld_list = [], []
    for gi, (_, dilation) in enumerate(ATTN_GROUPS):
        o, ld = _attn_group(q[gi], k[gi], v[gi], dilation, interpret)
        o_list.append(o)
        ld_list.append(ld)
    yf = _seq_dft(fa, fb, b, s, interpret)
    h, hnt, st = _mix(x2, o_list, ld_list, yf, sga, sgb, w_attn, w_fourier, w_out, g_ffn,
                      w_query, sub_keys, interpret)
    e1, e2, th = _topk(st, interpret)
    y = _peer(hnt, h, e1, e2, th, expert_u_b, expert_vt_b, interpret)
    return y.reshape(b, s, D_MODEL)


def kernel(x_prompt, x_sample, g_mix, w_in, q_gain, k_gain, w_fourier, w_attn, w_out, g_ffn,
           w_query, sub_keys, expert_u, expert_v):
    y_prompt, y_sample = x_prompt, x_sample
    for layer in range(g_mix.shape[0]):
        params = (g_mix[layer], w_in[layer], q_gain[layer], k_gain[layer], w_fourier[layer],
                  w_attn[layer], w_out[layer], g_ffn[layer], w_query[layer], sub_keys[layer],
                  *_expert_tables(expert_u, expert_v, layer, EB_PEER, False))
        y_prompt = _hybrid_block(y_prompt, *params)
        y_sample = _hybrid_block(y_sample, *params)
    return (y_prompt, y_sample)
```

```python
import functools
import math

import jax
import jax.numpy as jnp
from jax import lax
from jax.experimental import pallas as pl
from jax.experimental.pallas import tpu as pltpu

F32 = jnp.float32
BF16 = jnp.bfloat16

D_MODEL = 1024
FOURIER_GROUPS = 4
FOURIER_GROUP_DIM = 128
FOURIER_WIDTH = FOURIER_GROUPS * FOURIER_GROUP_DIM
ATTN_GROUPS = ((128, 1), (512, 4), (2048, 16))
N_ATTN_GROUPS = 3
HEADS_PER_GROUP = 4
HEAD_DIM = 128
ATTN_GROUP_WIDTH = HEADS_PER_GROUP * HEAD_DIM
ATTN_WIDTH = N_ATTN_GROUPS * ATTN_GROUP_WIDTH
ROPE_DIM = HEAD_DIM // 4
ROPE_THETA = 500000.0
N_KEYS = 128
N_EXPERTS = N_KEYS * N_KEYS
PEER_HEADS = 8
PEER_TOPK = 16
PEER_KEY_DIM = 256
NORM_EPS = 1e-6
NEG_INF = -1e30

LANES = 128
VMEM_LIMIT = 56 * 1024 * 1024

TM_PROJ = 512
TM_MIX = 512
TM_TOPK = 512
TM_PEER = 512
EB_PEER = 2048
QB_ATTN = 128
DFT_TS = 1024
DFT_TK = 2048


def _cparams(sem):
    return pltpu.CompilerParams(dimension_semantics=sem, vmem_limit_bytes=VMEM_LIMIT)


def _const_spec(shape):
    nd = len(shape)
    return pl.BlockSpec(shape, lambda *_: (0,) * nd)


def _proj_kernel(x_ref, g_ref, wf_ref, wq_ref, wk_ref, wv_ref, wga_ref, wgb_ref,
                 qg_ref, kg_ref, cc_ref, sc_ref, rc_ref, rs1_ref, rs2_ref,
                 fa_ref, fb_ref, q0_ref, q1_ref, q2_ref, k0_ref, k1_ref, k2_ref,
                 v0_ref, v1_ref, v2_ref, sga_ref, sgb_ref, stage_ref):
    tm = x_ref.shape[0]
    hpg = HEADS_PER_GROUP
    stage_slot = [0]

    def store_group(out_ref, gi, cols):
        dilation = ATTN_GROUPS[gi][1]
        if dilation == 1:
            for hl, c in enumerate(cols):
                out_ref[0, :, hl * HEAD_DIM:(hl + 1) * HEAD_DIM] = c.astype(BF16)
            return
        slot = stage_slot[0]
        stage_slot[0] += 1
        for hl, c in enumerate(cols):
            stage_ref[slot, hl] = c
            for r in range(dilation):
                out_ref[r, :, hl * HEAD_DIM:(hl + 1) * HEAD_DIM] = stage_ref[
                    slot, hl, pl.ds(r, tm // dilation, stride=dilation), :].astype(BF16)

    x = x_ref[...]
    ms = jnp.mean(x * x, axis=-1, keepdims=True)
    xn = (x * lax.rsqrt(ms + NORM_EPS) * g_ref[...]).astype(BF16)

    f = jnp.dot(xn, wf_ref[...], preferred_element_type=F32).astype(BF16)
    fa_ref[...] = jnp.dot(f, cc_ref[...], preferred_element_type=F32).astype(BF16)
    fb_ref[...] = jnp.dot(f, sc_ref[...], preferred_element_type=F32).astype(BF16)

    rc = rc_ref[...]
    rs1 = rs1_ref[...]
    rs2 = rs2_ref[...]

    def norm_rope(w_ref, gain_ref, out_refs, scale):
        y = jnp.dot(xn, w_ref[...], preferred_element_type=F32)
        for gi in range(N_ATTN_GROUPS):
            cols = []
            for hh in range(gi * hpg, (gi + 1) * hpg):
                yh = y[:, hh * HEAD_DIM:(hh + 1) * HEAD_DIM]
                hms = jnp.mean(yh * yh, axis=-1, keepdims=True)
                yn = yh * lax.rsqrt(hms + NORM_EPS) * gain_ref[gi:gi + 1, :]
                r = (yn * rc + pltpu.roll(yn, LANES - ROPE_DIM // 2, 1) * rs1
                     + pltpu.roll(yn, ROPE_DIM // 2, 1) * rs2)
                cols.append(r * scale)
            store_group(out_refs[gi], gi, cols)

    norm_rope(wq_ref, qg_ref, (q0_ref, q1_ref, q2_ref), HEAD_DIM ** -0.5)
    norm_rope(wk_ref, kg_ref, (k0_ref, k1_ref, k2_ref), 1.0)
    yv = jnp.dot(xn, wv_ref[...], preferred_element_type=F32)
    for gi, out_ref in enumerate((v0_ref, v1_ref, v2_ref)):
        store_group(out_ref, gi, [yv[:, hh * HEAD_DIM:(hh + 1) * HEAD_DIM]
                                  for hh in range(gi * hpg, (gi + 1) * hpg)])
    sga_ref[...] = jax.nn.sigmoid(
        jnp.dot(xn, wga_ref[...], preferred_element_type=F32)).astype(BF16)
    sgb_ref[...] = jax.nn.sigmoid(
        jnp.dot(xn, wgb_ref[...], preferred_element_type=F32)).astype(BF16)


def _rope_tables(s):
    half = ROPE_DIM // 2
    inv_freq = ROPE_THETA ** (-jnp.arange(half, dtype=F32) * 2.0 / ROPE_DIM)
    ang = jnp.arange(s, dtype=F32)[:, None] * inv_freq[None, :]
    cos, sin = jnp.cos(ang), jnp.sin(ang)
    zeros = jnp.zeros((s, HEAD_DIM - ROPE_DIM), F32)
    zh = jnp.zeros((s, half), F32)
    rc = jnp.concatenate([cos, cos, jnp.ones((s, HEAD_DIM - ROPE_DIM), F32)], axis=1)
    rs1 = jnp.concatenate([-sin, zh, zeros], axis=1)
    rs2 = jnp.concatenate([zh, sin, zeros], axis=1)
    return rc, rs1, rs2


def _channel_dft_mats():
    n = FOURIER_GROUP_DIM
    idx = (jnp.arange(n)[:, None] * jnp.arange(n)[None, :]) % n
    ang = idx.astype(F32) * (2.0 * math.pi / n)
    eye = jnp.eye(FOURIER_GROUPS, dtype=F32)
    cc = jnp.kron(eye, jnp.cos(ang) * n ** -0.5)
    sc = jnp.kron(eye, jnp.sin(ang) * n ** -0.5)
    return cc.astype(BF16), sc.astype(BF16)


def _proj(x2, s, g_mix, w_in, q_gain, k_gain, interpret):
    t = x2.shape[0]
    tm = TM_PROJ
    c0 = FOURIER_WIDTH
    c1 = c0 + ATTN_WIDTH
    c2 = c1 + ATTN_WIDTH
    c3 = c2 + ATTN_WIDTH
    c4 = c3 + D_MODEL
    wb = w_in.astype(BF16)
    wf, wq, wk, wv, wga, wgb = (wb[:, :c0], wb[:, c0:c1], wb[:, c1:c2], wb[:, c2:c3],
                                wb[:, c3:c4], wb[:, c4:])
    cc, sc = _channel_dft_mats()
    rc, rs1, rs2 = _rope_tables(s)
    spt = s // tm
    b = t // s
    tok = lambda w: pl.BlockSpec((tm, w), lambda i: (i, 0))
    pos = pl.BlockSpec((tm, HEAD_DIM), lambda i: (i % spt, 0))
    grp_shapes = [jax.ShapeDtypeStruct((b, d, s // d, ATTN_GROUP_WIDTH), BF16)
                  for _, d in ATTN_GROUPS]
    grp_specs = [pl.BlockSpec((None, d, tm // d, ATTN_GROUP_WIDTH),
                              lambda i: (i // spt, 0, i % spt, 0)) for _, d in ATTN_GROUPS]
    out_shape = ([jax.ShapeDtypeStruct((t, FOURIER_WIDTH), BF16)] * 2 + grp_shapes * 3
                 + [jax.ShapeDtypeStruct((t, D_MODEL), BF16)] * 2)
    n_staged = 3 * sum(1 for _, d in ATTN_GROUPS if d > 1)
    outs = pl.pallas_call(
        _proj_kernel,
        out_shape=out_shape,
        grid=(t // tm,),
        in_specs=[tok(D_MODEL), _const_spec((1, D_MODEL)),
                  _const_spec(wf.shape), _const_spec(wq.shape), _const_spec(wk.shape),
                  _const_spec(wv.shape), _const_spec(wga.shape), _const_spec(wgb.shape),
                  _const_spec(q_gain.shape), _const_spec(k_gain.shape),
                  _const_spec(cc.shape), _const_spec(sc.shape), pos, pos, pos],
        out_specs=[tok(FOURIER_WIDTH)] * 2 + grp_specs * 3 + [tok(D_MODEL)] * 2,
        scratch_shapes=[pltpu.VMEM((n_staged, HEADS_PER_GROUP, tm, HEAD_DIM), F32)],
        compiler_params=_cparams(("arbitrary",)),
        name="proj",
        interpret=interpret,
    )(x2, g_mix.reshape(1, D_MODEL), wf, wq, wk, wv, wga, wgb, q_gain, k_gain, cc, sc,
      rc, rs1, rs2)
    fa, fb = outs[0], outs[1]
    q, k, v = outs[2:5], outs[5:8], outs[8:11]
    return fa, fb, q, k, v, outs[11], outs[12]


LD_LANES = LANES // HEADS_PER_GROUP


def _attn_kernel(q_ref, k_ref, v_ref, o_ref, ld_ref, *, sub_len, heads):
    kb = min(2 * QB_ATTN, sub_len)
    span = QB_ATTN // 2
    first_head = pl.program_id(2) * heads
    residues = q_ref.shape[0]

    if heads < HEADS_PER_GROUP:
        @pl.when(first_head == 0)
        def _():
            ld_ref[...] = jnp.zeros_like(ld_ref)

    def body(i, carry):
        start = pl.multiple_of(i * QB_ATTN, QB_ATTN)
        ks = pl.multiple_of(jnp.clip(start - span, 0, sub_len - kb), span)
        qpos = start + lax.broadcasted_iota(jnp.int32, (QB_ATTN, kb), 0)
        kpos = ks + lax.broadcasted_iota(jnp.int32, (QB_ATTN, kb), 1)
        in_band = jnp.abs(kpos - qpos) <= span
        lane_head = lax.broadcasted_iota(jnp.int32, (QB_ATTN, LANES), 1) // LD_LANES
        for g in range(residues):
            if heads < HEADS_PER_GROUP:
                ld_all = ld_ref[g, pl.ds(start, QB_ATTN), :]
            else:
                ld_all = jnp.zeros((QB_ATTN, LANES), F32)
            for hl in range(heads):
                cols = slice(hl * HEAD_DIM, (hl + 1) * HEAD_DIM)
                q = q_ref[g, pl.ds(start, QB_ATTN), cols]
                k = k_ref[g, pl.ds(ks, kb), cols]
                v = v_ref[g, pl.ds(ks, kb), cols]
                s = lax.dot_general(q, k, (((1,), (1,)), ((), ())),
                                    preferred_element_type=F32)
                s = jnp.where(in_band, s, NEG_INF)
                m = jnp.max(s, axis=1, keepdims=True)
                p = jnp.exp(s - m)
                l = jnp.sum(p, axis=1, keepdims=True)
                o = jnp.dot(p.astype(BF16), v, preferred_element_type=F32) / l
                o_ref[g, pl.ds(start, QB_ATTN), cols] = o.astype(BF16)
                ld_all = jnp.where(lane_head == first_head + hl, m + jnp.log(l), ld_all)
            ld_ref[g, pl.ds(start, QB_ATTN), :] = ld_all
        return carry

    nq = sub_len // QB_ATTN
    lax.fori_loop(0, nq, body, 0, unroll=max(1, min(nq, ATTN_CHAINS // (heads * residues))))


ATTN_MAX_WIDE_LEN = 2048
ATTN_CHAINS = 32


def _attn_group(q, k, v, dilation, interpret):
    b, _, sub_len, _ = q.shape
    assert sub_len % QB_ATTN == 0
    heads = HEADS_PER_GROUP if sub_len <= ATTN_MAX_WIDE_LEN else 1
    width = heads * HEAD_DIM
    residues = max(1, min(dilation, ATTN_CHAINS * QB_ATTN // (heads * sub_len)))
    spec = pl.BlockSpec((None, residues, sub_len, width), lambda bi, r, hb: (bi, r, 0, hb))
    return pl.pallas_call(
        functools.partial(_attn_kernel, sub_len=sub_len, heads=heads),
        out_shape=(jax.ShapeDtypeStruct(q.shape, BF16),
                   jax.ShapeDtypeStruct((b, dilation, sub_len, LANES), F32)),
        grid=(b, dilation // residues, HEADS_PER_GROUP // heads),
        in_specs=[spec, spec, spec],
        out_specs=(spec, pl.BlockSpec((None, residues, sub_len, LANES),
                                      lambda bi, r, hb: (bi, r, 0, 0))),
        compiler_params=_cparams(("arbitrary", "arbitrary", "arbitrary")),
        name=f"attn_d{dilation}",
        interpret=interpret,
    )(q, k, v)


def _dft_kernel(cs_ref, ss_ref, a_ref, b_ref, y_ref, acc_ref):
    kk = pl.program_id(2)

    @pl.when(kk == 0)
    def _():
        acc_ref[...] = jnp.zeros_like(acc_ref)

    acc_ref[...] += (jnp.dot(cs_ref[...], a_ref[...], preferred_element_type=F32)
                     - jnp.dot(ss_ref[...], b_ref[...], preferred_element_type=F32))

    @pl.when(kk == pl.num_programs(2) - 1)
    def _():
        y_ref[...] = acc_ref[...].astype(BF16)


def _dft_mats_kernel(ch_ref, sh_ref, cl_ref, sl_ref, cs_ref, ss_ref):
    cl, sl = cl_ref[...], sl_ref[...]
    for k in range(ch_ref.shape[1]):
        c, s_ = ch_ref[:, k:k + 1], sh_ref[:, k:k + 1]
        cols = slice(k * LANES, (k + 1) * LANES)
        cs_ref[:, cols] = (c * cl - s_ * sl).astype(BF16)
        ss_ref[:, cols] = (s_ * cl + c * sl).astype(BF16)


DFT_GEN_ROWS = 256


def _seq_dft_mats(s, interpret):
    lo = LANES
    hi = s // lo
    rows = jnp.arange(s, dtype=jnp.int32)[:, None]
    scale = 2.0 * math.pi / s
    ang_hi = ((rows * (jnp.arange(hi, dtype=jnp.int32)[None, :] * lo)) % s).astype(F32) * scale
    ang_lo = ((rows * jnp.arange(lo, dtype=jnp.int32)[None, :]) % s).astype(F32) * scale
    norm = s ** -0.5
    tr = DFT_GEN_ROWS
    coarse = pl.BlockSpec((tr, hi), lambda i: (i, 0))
    fine = pl.BlockSpec((tr, lo), lambda i: (i, 0))
    out = pl.BlockSpec((tr, s), lambda i: (i, 0))
    return pl.pallas_call(
        _dft_mats_kernel,
        out_shape=(jax.ShapeDtypeStruct((s, s), BF16),) * 2,
        grid=(s // tr,),
        in_specs=[coarse, coarse, fine, fine],
        out_specs=(out, out),
        compiler_params=_cparams(("arbitrary",)),
        name="dft_mats",
        interpret=interpret,
    )(jnp.cos(ang_hi), jnp.sin(ang_hi), jnp.cos(ang_lo) * norm, jnp.sin(ang_lo) * norm)


def _seq_dft(fa, fb, b, s, interpret):
    ts = min(DFT_TS, s)
    tk = min(DFT_TK, s)
    cs, ss = _seq_dft_mats(s, interpret)
    mat = pl.BlockSpec((ts, tk), lambda bi, i, kk: (i, kk))
    vec = pl.BlockSpec((None, tk, FOURIER_WIDTH), lambda bi, i, kk: (bi, kk, 0))
    y = pl.pallas_call(
        _dft_kernel,
        out_shape=jax.ShapeDtypeStruct((b, s, FOURIER_WIDTH), BF16),
        grid=(b, s // ts, s // tk),
        in_specs=[mat, mat, vec, vec],
        out_specs=pl.BlockSpec((None, ts, FOURIER_WIDTH), lambda bi, i, kk: (bi, i, 0)),
        scratch_shapes=[pltpu.VMEM((ts, FOURIER_WIDTH), F32)],
        compiler_params=_cparams(("arbitrary", "arbitrary", "arbitrary")),
        name="seq_dft",
        interpret=interpret,
    )(cs, ss, fa.reshape(b, s, FOURIER_WIDTH), fb.reshape(b, s, FOURIER_WIDTH))
    return y.reshape(b * s, FOURIER_WIDTH)


def _mix_kernel(x_ref, o0_ref, o1_ref, o2_ref, l0_ref, l1_ref, l2_ref, y_ref, sga_ref, sgb_ref,
                wat_ref, wfo_ref, wout_ref, gf_ref, wqy_ref, sk_ref, hx_ref,
                h_ref, hnt_ref, st_ref, ostage_ref, lstage_ref):
    tm = x_ref.shape[0]

    def token_order(ref, stage, gi):
        dilation = ATTN_GROUPS[gi][1]
        if dilation == 1:
            return ref[0].astype(F32)
        cols = []
        for c in range(ref.shape[-1] // LANES):
            for r in range(dilation):
                stage[gi - 1, c, pl.ds(r, tm // dilation, stride=dilation), :] = (
                    ref[r, :, c * LANES:(c + 1) * LANES].astype(F32))
            cols.append(stage[gi - 1, c])
        return cols[0] if len(cols) == 1 else jnp.concatenate(cols, axis=1)

    lds = [token_order(ref, lstage_ref, gi) for gi, ref in enumerate((l0_ref, l1_ref, l2_ref))]
    mx = jnp.maximum(jnp.maximum(lds[0], lds[1]), lds[2])
    ws = [jnp.exp(ld - mx) for ld in lds]
    inv = 1.0 / (ws[0] + ws[1] + ws[2])
    attn = jnp.zeros((tm, ATTN_GROUP_WIDTH), F32)
    for gi, ref in enumerate((o0_ref, o1_ref, o2_ref)):
        wexp = jnp.dot((ws[gi] * inv).astype(BF16), hx_ref[...], preferred_element_type=F32)
        attn = attn + wexp * token_order(ref, ostage_ref, gi)
    attn = jnp.dot(attn.astype(BF16), wat_ref[...], preferred_element_type=F32)
    four = jnp.dot(y_ref[...], wfo_ref[...], preferred_element_type=F32)
    mix = sga_ref[...].astype(F32) * four + sgb_ref[...].astype(F32) * attn
    h = x_ref[...] + jnp.dot(mix.astype(BF16), wout_ref[...], preferred_element_type=F32)
    h_ref[...] = h
    ms = jnp.mean(h * h, axis=-1, keepdims=True)
    hn32 = h * lax.rsqrt(ms + NORM_EPS) * gf_ref[...]
    hn = hn32.astype(BF16)
    hnt_ref[...] = hn32.T.astype(BF16)
    qy = jnp.dot(hn, wqy_ref[...], preferred_element_type=F32).astype(BF16)
    half = PEER_KEY_DIM // 2
    for hc in range(2 * PEER_HEADS):
        st_ref[hc] = lax.dot_general(sk_ref[hc], qy[:, hc * half:(hc + 1) * half],
                                     (((1,), (1,)), ((), ())), preferred_element_type=F32)


def _mix(x2, o_list, ld_list, yf, sga, sgb, w_attn, w_fourier, w_out, g_ffn, w_query, sub_keys,
         interpret):
    t = x2.shape[0]
    tm = TM_MIX
    spt = o_list[0].shape[2] // tm
    tok = lambda w: pl.BlockSpec((tm, w), lambda i: (i, 0))
    grp = lambda w: [pl.BlockSpec((None, d, tm // d, w), lambda i: (i // spt, 0, i % spt, 0))
                     for _, d in ATTN_GROUPS]
    head_expand = (jnp.arange(LANES)[:, None]
                   == LD_LANES * (jnp.arange(ATTN_GROUP_WIDTH)[None, :] // HEAD_DIM)).astype(BF16)
    n_staged = sum(1 for _, d in ATTN_GROUPS if d > 1)
    sk = jnp.stack([sub_keys[:, 0][:, jnp.array(_PAIR_ORDER)], sub_keys[:, 1]], axis=1)
    sk = sk.reshape(2 * PEER_HEADS, N_KEYS, PEER_KEY_DIM // 2).astype(BF16)
    wat, wfo, wout, wqy = (w_attn.astype(BF16), w_fourier.astype(BF16), w_out.astype(BF16),
                           w_query.astype(BF16))
    return pl.pallas_call(
        _mix_kernel,
        out_shape=(jax.ShapeDtypeStruct((t, D_MODEL), F32),
                   jax.ShapeDtypeStruct((D_MODEL, t), BF16),
                   jax.ShapeDtypeStruct((2 * PEER_HEADS, N_KEYS, t), F32)),
        grid=(t // tm,),
        in_specs=[tok(D_MODEL)] + grp(ATTN_GROUP_WIDTH) + grp(LANES) + [tok(FOURIER_WIDTH),
                  tok(D_MODEL), tok(D_MODEL),
                  _const_spec(wat.shape), _const_spec(wfo.shape), _const_spec(wout.shape),
                  _const_spec((1, D_MODEL)), _const_spec(wqy.shape), _const_spec(sk.shape),
                  _const_spec(head_expand.shape)],
        out_specs=(tok(D_MODEL), pl.BlockSpec((D_MODEL, tm), lambda i: (0, i)),
                   pl.BlockSpec((2 * PEER_HEADS, N_KEYS, tm), lambda i: (0, 0, i))),
        scratch_shapes=[pltpu.VMEM((n_staged, HEADS_PER_GROUP, tm, HEAD_DIM), F32),
                        pltpu.VMEM((n_staged, 1, tm, LANES), F32)],
        compiler_params=_cparams(("arbitrary",)),
        name="mix",
        interpret=interpret,
    )(x2, *o_list, *ld_list, yf, sga, sgb, wat, wfo, wout, g_ffn.reshape(1, D_MODEL), wqy, sk,
      head_expand)


N_PAIRS = N_KEYS // 2
_PAIR_ORDER = tuple(range(0, N_KEYS, 2)) + tuple(range(1, N_KEYS, 2))
SUBLANES = 8


def _batcher_network(n):
    def merge(lo, hi, r):
        step = r * 2
        if step < hi - lo:
            yield from merge(lo, hi, step)
            yield from merge(lo + r, hi, step)
            for i in range(lo + r, hi - r, step):
                yield (i, i + r)
        else:
            yield (lo, lo + r)

    def sort(lo, hi):
        if hi - lo >= 1:
            mid = lo + (hi - lo) // 2
            yield from sort(lo, mid)
            yield from sort(mid + 1, hi)
            yield from merge(lo, hi, 1)

    return tuple(sort(0, n - 1))


_SORT16 = _batcher_network(PEER_TOPK)


def _compare_exchange(v, i, j):
    v[i], v[j] = jnp.maximum(v[i], v[j]), jnp.minimum(v[i], v[j])


def _top16_sorted(keys):
    n = PEER_TOPK
    v = [keys[SUBLANES * k:SUBLANES * (k + 1)] for k in range(n)]
    for i, j in _SORT16:
        _compare_exchange(v, i, j)
    shift = SUBLANES // 2
    while shift >= 1:
        other = [pltpu.roll(x, shift, 0) for x in v]
        v = [jnp.maximum(v[k], other[n - 1 - k]) for k in range(n)]
        d = n // 2
        while d >= 1:
            for i in range(n):
                if not i & d:
                    _compare_exchange(v, i, i + d)
            d //= 2
        shift //= 2
    return [x[0:1] for x in v]


_CAND_PAIRS = tuple((i, j) for i in range(PEER_TOPK) for j in range(PEER_TOPK)
                    if (i + 1) * (j + 1) <= PEER_TOPK)
_CAND_ROWS = -(-len(_CAND_PAIRS) // SUBLANES) * SUBLANES
TIE_STEP = 2.0 ** -17


def _kth_slot_value(vals, k):
    slot = lax.broadcasted_iota(jnp.int32, vals.shape, 0).astype(F32)
    bumped = vals * (1.0 + slot * TIE_STEP)
    cur = bumped
    m = None
    for _ in range(k):
        m = jnp.max(cur, axis=0, keepdims=True)
        cur = jnp.where(cur == m, -1.0, cur)
    return bumped, m


def _candidates(fa, fb):
    rows = [fa[i] * fb[j] for i, j in _CAND_PAIRS]
    rows.append(jnp.zeros((_CAND_ROWS - len(rows),) + rows[0].shape[1:], rows[0].dtype))
    return jnp.concatenate(rows, axis=0)


def _topk_kernel(st_ref, e1_ref, e2_ref, th_ref):
    tm = st_ref.shape[2]
    for h in range(PEER_HEADS):
        for tc in range(tm // LANES):
            lanes = slice(tc * LANES, (tc + 1) * LANES)
            s1 = st_ref[2 * h, :, lanes]
            s2 = st_ref[2 * h + 1, :, lanes]
            a = _top16_sorted(s1)
            b = _top16_sorted(s2)
            ea = [jnp.exp(ai - a[0]) for ai in a]
            eb = [jnp.exp(bi - b[0]) for bi in b]
            cand = _candidates(ea, eb)
            bumped, kth = _kth_slot_value(cand, PEER_TOPK)
            z = jnp.sum(jnp.where(bumped >= kth, cand, 0.0), axis=0, keepdims=True)
            inv_z = 1.0 / z
            cand_b = _candidates([(x * inv_z).astype(BF16) for x in ea],
                                 [x.astype(BF16) for x in eb]).astype(F32)
            _, kth_b = _kth_slot_value(cand_b, PEER_TOPK)
            th_ref[h:h + 1, lanes] = kth_b.astype(BF16).astype(F32)
            e1 = jnp.exp(s1 - a[0]) * inv_z
            e1_ref[:, 2 * h, lanes] = e1[:N_PAIRS]
            e1_ref[:, 2 * h + 1, lanes] = e1[N_PAIRS:]
            e2_ref[h, :, lanes] = jnp.exp(s2 - b[0]).astype(BF16)


def _topk(st, interpret):
    t = st.shape[2]
    tm = TM_TOPK
    blk = lambda n: pl.BlockSpec((n, N_KEYS, tm), lambda i: (0, 0, i))
    return pl.pallas_call(
        _topk_kernel,
        out_shape=(jax.ShapeDtypeStruct((N_PAIRS, 2 * PEER_HEADS, t), F32),
                   jax.ShapeDtypeStruct((PEER_HEADS, N_KEYS, t), BF16),
                   jax.ShapeDtypeStruct((PEER_HEADS, t), F32)),
        grid=(t // tm,),
        in_specs=[blk(2 * PEER_HEADS)],
        out_specs=(pl.BlockSpec((N_PAIRS, 2 * PEER_HEADS, tm), lambda i: (0, 0, i)),
                   blk(PEER_HEADS),
                   pl.BlockSpec((PEER_HEADS, tm), lambda i: (0, i))),
        compiler_params=_cparams(("arbitrary",)),
        name="peer_topk",
        interpret=interpret,
    )(st)


SB_PEER = 2 * N_KEYS
KB_PEER = 32


def _peer_kernel(hnt_ref, u_ref, vt_ref, e1_ref, e2_ref, th_ref, h_ref, y_ref,
                 at0_ref, at1_ref, ht0_ref, ht1_ref, acc_ref, join_ref, *, nj):
    s = pl.program_id(0)
    tm = hnt_ref.shape[1]
    j_out = lax.rem(jnp.maximum(s - 2, 0), nj)

    @pl.when(s == 0)
    def _():
        for ref in (at0_ref, at1_ref, ht0_ref, ht1_ref):
            ref[...] = jnp.zeros_like(ref)

    @pl.when(j_out == 0)
    def _():
        acc_ref[...] = jnp.zeros_like(acc_ref)

    def stages(at_w, at_r, ht_w, ht_r):
        at_new = jnp.dot(u_ref[...], hnt_ref[...], preferred_element_type=F32)
        at_w[...] = at_new
        join = at_new[at_new.shape[0] - 8:, tm - LANES:]
        prev_zero = jnp.zeros((KB_PEER, LANES), BF16)
        for key_pair in range(u_ref.shape[0] // SB_PEER):
            for par in range(2):
                for tc in range(tm // LANES):
                    lanes = slice(tc * LANES, (tc + 1) * LANES)
                    e1b = [jnp.broadcast_to(
                        e1_ref[key_pair, 2 * h + par:2 * h + par + 1, lanes].astype(BF16),
                        (KB_PEER, LANES)) for h in range(PEER_HEADS)]
                    thb = [jnp.broadcast_to(th_ref[h:h + 1, lanes].astype(BF16),
                                            (KB_PEER, LANES)) for h in range(PEER_HEADS)]
                    for kb in range(N_KEYS // KB_PEER):
                        keys = slice(kb * KB_PEER, (kb + 1) * KB_PEER)
                        ra = key_pair * SB_PEER + par * N_KEYS + kb * KB_PEER
                        rows = slice(ra, ra + KB_PEER)
                        gate = prev_zero
                        for h in range(PEER_HEADS):
                            p = e1b[h] * e2_ref[h, keys, lanes]
                            gate = gate + jnp.where(p >= thb[h], p, jnp.zeros_like(p))
                        act = jax.nn.gelu(at_r[rows, lanes].astype(BF16)) * gate
                        ht_w[rows, lanes] = act
                        join = join + act[:8].astype(F32)
                        bits = pltpu.bitcast(act, jnp.uint32)
                        prev_zero = pltpu.bitcast((bits >> 16) >> 16, BF16)
        out_new = jnp.dot(vt_ref[...], ht_r[...], preferred_element_type=F32)
        acc_ref[...] += out_new
        join_ref[...] = join + out_new[D_MODEL - 8:, tm - LANES:]

    parity = lax.rem(s, 2)

    @pl.when(parity == 0)
    def _():
        stages(at0_ref, at1_ref, ht1_ref, ht0_ref)

    @pl.when(parity == 1)
    def _():
        stages(at1_ref, at0_ref, ht0_ref, ht1_ref)

    @pl.when((j_out == nj - 1) & (s >= 2))
    def _():
        y_ref[...] = h_ref[...] + acc_ref[...].T


def _expert_table_kernel(u_ref, v_ref, ub_ref, vt_ref):
    ub_ref[...] = u_ref[...].astype(BF16)
    vt_ref[...] = v_ref[...].T.astype(BF16)


def _expert_tables(expert_u, expert_v, layer, eb, interpret):
    nj = N_EXPERTS // eb
    blk = pl.BlockSpec((eb, D_MODEL), lambda j: (j, 0))
    layer_blk = pl.BlockSpec((None, eb, D_MODEL), lambda j: (layer, j, 0))
    return pl.pallas_call(
        _expert_table_kernel,
        out_shape=(jax.ShapeDtypeStruct((N_EXPERTS, D_MODEL), BF16),
                   jax.ShapeDtypeStruct((nj, D_MODEL, eb), BF16)),
        grid=(nj,),
        in_specs=[layer_blk, layer_blk],
        out_specs=(blk, pl.BlockSpec((None, D_MODEL, eb), lambda j: (j, 0, 0))),
        compiler_params=_cparams(("arbitrary",)),
        name="expert_tables",
        interpret=interpret,
    )(expert_u, expert_v)


def _peer(hnt, h, e1, e2, th, u, vt, interpret):
    t = hnt.shape[1]
    tm, eb = TM_PEER, EB_PEER
    nj = N_EXPERTS // eb
    n_pairs = (t // tm) * nj
    tile = lambda s: s // nj
    blk = lambda s: lax.rem(s, nj)
    s1 = lambda s: jnp.minimum(s, n_pairs - 1)
    s2 = lambda s: jnp.clip(s - 1, 0, n_pairs - 1)
    s3 = lambda s: jnp.maximum(s - 2, 0)
    return pl.pallas_call(
        functools.partial(_peer_kernel, nj=nj),
        out_shape=jax.ShapeDtypeStruct((t, D_MODEL), F32),
        grid=(n_pairs + 2,),
        in_specs=[pl.BlockSpec((D_MODEL, tm), lambda s: (0, tile(s1(s)))),
                  pl.BlockSpec((eb, D_MODEL), lambda s: (blk(s1(s)), 0)),
                  pl.BlockSpec((None, D_MODEL, eb), lambda s: (blk(s3(s)), 0, 0)),
                  pl.BlockSpec((eb // SB_PEER, 2 * PEER_HEADS, tm),
                               lambda s: (blk(s2(s)), 0, tile(s2(s)))),
                  pl.BlockSpec((PEER_HEADS, N_KEYS, tm), lambda s: (0, 0, tile(s2(s)))),
                  pl.BlockSpec((PEER_HEADS, tm), lambda s: (0, tile(s2(s)))),
                  pl.BlockSpec((tm, D_MODEL), lambda s: (tile(s3(s)), 0))],
        out_specs=pl.BlockSpec((tm, D_MODEL), lambda s: (tile(s3(s)), 0)),
        scratch_shapes=[pltpu.VMEM((eb, tm), F32), pltpu.VMEM((eb, tm), F32),
                        pltpu.VMEM((eb, tm), BF16), pltpu.VMEM((eb, tm), BF16),
                        pltpu.VMEM((D_MODEL, tm), F32), pltpu.VMEM((8, LANES), F32)],
        compiler_params=_cparams(("arbitrary",)),
        name="peer_experts",
        interpret=interpret,
    )(hnt, u, vt, e1, e2, th, h)


def _hybrid_block(x, g_mix, w_in, q_gain, k_gain, w_fourier, w_attn, w_out, g_ffn, w_query,
                  sub_keys, expert_u_b, expert_vt_b, interpret=False):
    b, s, _ = x.shape
    x2 = x.reshape(b * s, D_MODEL)
    fa, fb, q, k, v, sga, sgb = _proj(x2, s, g_mix, w_in, q_gain, k_gain, interpret)
    o_list, ld_list = [], []
    for gi, (_, dilation) in enumerate(ATTN_GROUPS):
        o, ld = _attn_group(q[gi], k[gi], v[gi], dilation, interpret)
        o_list.append(o)
        ld_list.append(ld)
    yf = _seq_dft(fa, fb, b, s, interpret)
    h, hnt, st = _mix(x2, o_list, ld_list, yf, sga, sgb, w_attn, w_fourier, w_out, g_ffn,
                      w_query, sub_keys, interpret)
    e1, e2, th = _topk(st, interpret)
    y = _peer(hnt, h, e1, e2, th, expert_u_b, expert_vt_b, interpret)
    return y.reshape(b, s, D_MODEL)


def kernel(x_prompt, x_sample, g_mix, w_in, q_gain, k_gain, w_fourier, w_attn, w_out, g_ffn,
           w_query, sub_keys, expert_u, expert_v):
    y_prompt, y_sample = x_prompt, x_sample
    for layer in range(g_mix.shape[0]):
        params = (g_mix[layer], w_in[layer], q_gain[layer], k_gain[layer], w_fourier[layer],
                  w_attn[layer], w_out[layer], g_ffn[layer], w_query[layer], sub_keys[layer],
                  *_expert_tables(expert_u, expert_v, layer, EB_PEER, False))
        y_prompt = _hybrid_block(y_prompt, *params)
        y_sample = _hybrid_block(y_sample, *params)
    return (y_prompt, y_sample)
```

```python
import functools
import math

import jax
import jax.numpy as jnp
from jax import lax
from jax.experimental import pallas as pl
from jax.experimental.pallas import tpu as pltpu

F32 = jnp.float32
BF16 = jnp.bfloat16

D_MODEL = 1024
FOURIER_GROUPS = 4
FOURIER_GROUP_DIM = 128
FOURIER_WIDTH = FOURIER_GROUPS * FOURIER_GROUP_DIM
ATTN_GROUPS = ((128, 1), (512, 4), (2048, 16))
N_ATTN_GROUPS = 3
HEADS_PER_GROUP = 4
HEAD_DIM = 128
ATTN_GROUP_WIDTH = HEADS_PER_GROUP * HEAD_DIM
ATTN_WIDTH = N_ATTN_GROUPS * ATTN_GROUP_WIDTH
ROPE_DIM = HEAD_DIM // 4
ROPE_THETA = 500000.0
N_KEYS = 128
N_EXPERTS = N_KEYS * N_KEYS
PEER_HEADS = 8
PEER_TOPK = 16
PEER_KEY_DIM = 256
NORM_EPS = 1e-6
NEG_INF = -1e30

LANES = 128
VMEM_LIMIT = 56 * 1024 * 1024

TM_PROJ = 512
TM_MIX = 512
TM_TOPK = 512
TM_PEER = 512
EB_PEER = 2048
QB_ATTN = 128
DFT_TS = 1024
DFT_TK = 2048


def _cparams(sem):
    return pltpu.CompilerParams(dimension_semantics=sem, vmem_limit_bytes=VMEM_LIMIT)


def _const_spec(shape):
    nd = len(shape)
    return pl.BlockSpec(shape, lambda *_: (0,) * nd)


def _proj_kernel(x_ref, g_ref, wf_ref, wq_ref, wk_ref, wv_ref, wga_ref, wgb_ref,
                 qg_ref, kg_ref, cc_ref, sc_ref, rc_ref, rs1_ref, rs2_ref,
                 fa_ref, fb_ref, q0_ref, q1_ref, q2_ref, k0_ref, k1_ref, k2_ref,
                 v0_ref, v1_ref, v2_ref, sga_ref, sgb_ref, stage_ref):
    tm = x_ref.shape[0]
    hpg = HEADS_PER_GROUP
    stage_slot = [0]

    def store_group(out_ref, gi, cols):
        dilation = ATTN_GROUPS[gi][1]
        if dilation == 1:
            for hl, c in enumerate(cols):
                out_ref[0, :, hl * HEAD_DIM:(hl + 1) * HEAD_DIM] = c.astype(BF16)
            return
        slot = stage_slot[0]
        stage_slot[0] += 1
        for hl, c in enumerate(cols):
            stage_ref[slot, hl] = c
            for r in range(dilation):
                out_ref[r, :, hl * HEAD_DIM:(hl + 1) * HEAD_DIM] = stage_ref[
                    slot, hl, pl.ds(r, tm // dilation, stride=dilation), :].astype(BF16)

    x = x_ref[...]
    ms = jnp.mean(x * x, axis=-1, keepdims=True)
    xn = (x * lax.rsqrt(ms + NORM_EPS) * g_ref[...]).astype(BF16)

    f = jnp.dot(xn, wf_ref[...], preferred_element_type=F32).astype(BF16)
    fa_ref[...] = jnp.dot(f, cc_ref[...], preferred_element_type=F32).astype(BF16)
    fb_ref[...] = jnp.dot(f, sc_ref[...], preferred_element_type=F32).astype(BF16)

    rc = rc_ref[...]
    rs1 = rs1_ref[...]
    rs2 = rs2_ref[...]

    def norm_rope(w_ref, gain_ref, out_refs, scale):
        y = jnp.dot(xn, w_ref[...], preferred_element_type=F32)
        for gi in range(N_ATTN_GROUPS):
            cols = []
            for hh in range(gi * hpg, (gi + 1) * hpg):
                yh = y[:, hh * HEAD_DIM:(hh + 1) * HEAD_DIM]
                hms = jnp.mean(yh * yh, axis=-1, keepdims=True)
                yn = yh * lax.rsqrt(hms + NORM_EPS) * gain_ref[gi:gi + 1, :]
                r = (yn * rc + pltpu.roll(yn, LANES - ROPE_DIM // 2, 1) * rs1
                     + pltpu.roll(yn, ROPE_DIM // 2, 1) * rs2)
                cols.append(r * scale)
            store_group(out_refs[gi], gi, cols)

    norm_rope(wq_ref, qg_ref, (q0_ref, q1_ref, q2_ref), HEAD_DIM ** -0.5)
    norm_rope(wk_ref, kg_ref, (k0_ref, k1_ref, k2_ref), 1.0)
    yv = jnp.dot(xn, wv_ref[...], preferred_element_type=F32)
    for gi, out_ref in enumerate((v0_ref, v1_ref, v2_ref)):
        store_group(out_ref, gi, [yv[:, hh * HEAD_DIM:(hh + 1) * HEAD_DIM]
                                  for hh in range(gi * hpg, (gi + 1) * hpg)])
    sga_ref[...] = jax.nn.sigmoid(
        jnp.dot(xn, wga_ref[...], preferred_element_type=F32)).astype(BF16)
    sgb_ref[...] = jax.nn.sigmoid(
        jnp.dot(xn, wgb_ref[...], preferred_element_type=F32)).astype(BF16)


def _rope_tables(s):
    half = ROPE_DIM // 2
    inv_freq = ROPE_THETA ** (-jnp.arange(half, dtype=F32) * 2.0 / ROPE_DIM)
    ang = jnp.arange(s, dtype=F32)[:, None] * inv_freq[None, :]
    cos, sin = jnp.cos(ang), jnp.sin(ang)
    zeros = jnp.zeros((s, HEAD_DIM - ROPE_DIM), F32)
    zh = jnp.zeros((s, half), F32)
    rc = jnp.concatenate([cos, cos, jnp.ones((s, HEAD_DIM - ROPE_DIM), F32)], axis=1)
    rs1 = jnp.concatenate([-sin, zh, zeros], axis=1)
    rs2 = jnp.concatenate([zh, sin, zeros], axis=1)
    return rc, rs1, rs2


def _channel_dft_mats():
    n = FOURIER_GROUP_DIM
    idx = (jnp.arange(n)[:, None] * jnp.arange(n)[None, :]) % n
    ang = idx.astype(F32) * (2.0 * math.pi / n)
    eye = jnp.eye(FOURIER_GROUPS, dtype=F32)
    cc = jnp.kron(eye, jnp.cos(ang) * n ** -0.5)
    sc = jnp.kron(eye, jnp.sin(ang) * n ** -0.5)
    return cc.astype(BF16), sc.astype(BF16)


def _proj(x2, s, g_mix, w_in, q_gain, k_gain, interpret):
    t = x2.shape[0]
    tm = TM_PROJ
    c0 = FOURIER_WIDTH
    c1 = c0 + ATTN_WIDTH
    c2 = c1 + ATTN_WIDTH
    c3 = c2 + ATTN_WIDTH
    c4 = c3 + D_MODEL
    wb = w_in.astype(BF16)
    wf, wq, wk, wv, wga, wgb = (wb[:, :c0], wb[:, c0:c1], wb[:, c1:c2], wb[:, c2:c3],
                                wb[:, c3:c4], wb[:, c4:])
    cc, sc = _channel_dft_mats()
    rc, rs1, rs2 = _rope_tables(s)
    spt = s // tm
    b = t // s
    tok = lambda w: pl.BlockSpec((tm, w), lambda i: (i, 0))
    pos = pl.BlockSpec((tm, HEAD_DIM), lambda i: (i % spt, 0))
    grp_shapes = [jax.ShapeDtypeStruct((b, d, s // d, ATTN_GROUP_WIDTH), BF16)
                  for _, d in ATTN_GROUPS]
    grp_specs = [pl.BlockSpec((None, d, tm // d, ATTN_GROUP_WIDTH),
                              lambda i: (i // spt, 0, i % spt, 0)) for _, d in ATTN_GROUPS]
    out_shape = ([jax.ShapeDtypeStruct((t, FOURIER_WIDTH), BF16)] * 2 + grp_shapes * 3
                 + [jax.ShapeDtypeStruct((t, D_MODEL), BF16)] * 2)
    n_staged = 3 * sum(1 for _, d in ATTN_GROUPS if d > 1)
    outs = pl.pallas_call(
        _proj_kernel,
        out_shape=out_shape,
        grid=(t // tm,),
        in_specs=[tok(D_MODEL), _const_spec((1, D_MODEL)),
                  _const_spec(wf.shape), _const_spec(wq.shape), _const_spec(wk.shape),
                  _const_spec(wv.shape), _const_spec(wga.shape), _const_spec(wgb.shape),
                  _const_spec(q_gain.shape), _const_spec(k_gain.shape),
                  _const_spec(cc.shape), _const_spec(sc.shape), pos, pos, pos],
        out_specs=[tok(FOURIER_WIDTH)] * 2 + grp_specs * 3 + [tok(D_MODEL)] * 2,
        scratch_shapes=[pltpu.VMEM((n_staged, HEADS_PER_GROUP, tm, HEAD_DIM), F32)],
        compiler_params=_cparams(("arbitrary",)),
        name="proj",
        interpret=interpret,
    )(x2, g_mix.reshape(1, D_MODEL), wf, wq, wk, wv, wga, wgb, q_gain, k_gain, cc, sc,
      rc, rs1, rs2)
    fa, fb = outs[0], outs[1]
    q, k, v = outs[2:5], outs[5:8], outs[8:11]
    return fa, fb, q, k, v, outs[11], outs[12]


LD_LANES = LANES // HEADS_PER_GROUP


def _attn_kernel(q_ref, k_ref, v_ref, o_ref, ld_ref, *, sub_len, heads):
    kb = min(2 * QB_ATTN, sub_len)
    span = QB_ATTN // 2
    first_head = pl.program_id(2) * heads
    residues = q_ref.shape[0]

    if heads < HEADS_PER_GROUP:
        @pl.when(first_head == 0)
        def _():
            ld_ref[...] = jnp.zeros_like(ld_ref)

    def body(i, carry):
        start = pl.multiple_of(i * QB_ATTN, QB_ATTN)
        ks = pl.multiple_of(jnp.clip(start - span, 0, sub_len - kb), span)
        qpos = start + lax.broadcasted_iota(jnp.int32, (QB_ATTN, kb), 0)
        kpos = ks + lax.broadcasted_iota(jnp.int32, (QB_ATTN, kb), 1)
        in_band = jnp.abs(kpos - qpos) <= span
        lane_head = lax.broadcasted_iota(jnp.int32, (QB_ATTN, LANES), 1) // LD_LANES
        for g in range(residues):
            if heads < HEADS_PER_GROUP:
                ld_all = ld_ref[g, pl.ds(start, QB_ATTN), :]
            else:
                ld_all = jnp.zeros((QB_ATTN, LANES), F32)
            for hl in range(heads):
                cols = slice(hl * HEAD_DIM, (hl + 1) * HEAD_DIM)
                q = q_ref[g, pl.ds(start, QB_ATTN), cols]
                k = k_ref[g, pl.ds(ks, kb), cols]
                v = v_ref[g, pl.ds(ks, kb), cols]
                s = lax.dot_general(q, k, (((1,), (1,)), ((), ())),
                                    preferred_element_type=F32)
                s = jnp.where(in_band, s, NEG_INF)
                m = jnp.max(s, axis=1, keepdims=True)
                p = jnp.exp(s - m)
                l = jnp.sum(p, axis=1, keepdims=True)
                o = jnp.dot(p.astype(BF16), v, preferred_element_type=F32) / l
                o_ref[g, pl.ds(start, QB_ATTN), cols] = o.astype(BF16)
                ld_all = jnp.where(lane_head == first_head + hl, m + jnp.log(l), ld_all)
            ld_ref[g, pl.ds(start, QB_ATTN), :] = ld_all
        return carry

    nq = sub_len // QB_ATTN
    lax.fori_loop(0, nq, body, 0, unroll=max(1, min(nq, ATTN_CHAINS // (heads * residues))))


ATTN_MAX_WIDE_LEN = 2048
ATTN_CHAINS = 32


def _attn_group(q, k, v, dilation, interpret):
    b, _, sub_len, _ = q.shape
    assert sub_len % QB_ATTN == 0
    heads = HEADS_PER_GROUP if sub_len <= ATTN_MAX_WIDE_LEN else 1
    width = heads * HEAD_DIM
    residues = max(1, min(dilation, ATTN_CHAINS * QB_ATTN // (heads * sub_len)))
    spec = pl.BlockSpec((None, residues, sub_len, width), lambda bi, r, hb: (bi, r, 0, hb))
    return pl.pallas_call(
        functools.partial(_attn_kernel, sub_len=sub_len, heads=heads),
        out_shape=(jax.ShapeDtypeStruct(q.shape, BF16),
                   jax.ShapeDtypeStruct((b, dilation, sub_len, LANES), F32)),
        grid=(b, dilation // residues, HEADS_PER_GROUP // heads),
        in_specs=[spec, spec, spec],
        out_specs=(spec, pl.BlockSpec((None, residues, sub_len, LANES),
                                      lambda bi, r, hb: (bi, r, 0, 0))),
        compiler_params=_cparams(("arbitrary", "arbitrary", "arbitrary")),
        name=f"attn_d{dilation}",
        interpret=interpret,
    )(q, k, v)


def _dft_kernel(cs_ref, ss_ref, a_ref, b_ref, y_ref, acc_ref):
    kk = pl.program_id(2)

    @pl.when(kk == 0)
    def _():
        acc_ref[...] = jnp.zeros_like(acc_ref)

    acc_ref[...] += (jnp.dot(cs_ref[...], a_ref[...], preferred_element_type=F32)
                     - jnp.dot(ss_ref[...], b_ref[...], preferred_element_type=F32))

    @pl.when(kk == pl.num_programs(2) - 1)
    def _():
        y_ref[...] = acc_ref[...].astype(BF16)


def _dft_mats_kernel(ch_ref, sh_ref, cl_ref, sl_ref, cs_ref, ss_ref):
    cl, sl = cl_ref[...], sl_ref[...]
    for k in range(ch_ref.shape[1]):
        c, s_ = ch_ref[:, k:k + 1], sh_ref[:, k:k + 1]
        cols = slice(k * LANES, (k + 1) * LANES)
        cs_ref[:, cols] = (c * cl - s_ * sl).astype(BF16)
        ss_ref[:, cols] = (s_ * cl + c * sl).astype(BF16)


DFT_GEN_ROWS = 256


def _seq_dft_mats(s, interpret):
    lo = LANES
    hi = s // lo
    rows = jnp.arange(s, dtype=jnp.int32)[:, None]
    scale = 2.0 * math.pi / s
    ang_hi = ((rows * (jnp.arange(hi, dtype=jnp.int32)[None, :] * lo)) % s).astype(F32) * scale
    ang_lo = ((rows * jnp.arange(lo, dtype=jnp.int32)[None, :]) % s).astype(F32) * scale
    norm = s ** -0.5
    tr = DFT_GEN_ROWS
    coarse = pl.BlockSpec((tr, hi), lambda i: (i, 0))
    fine = pl.BlockSpec((tr, lo), lambda i: (i, 0))
    out = pl.BlockSpec((tr, s), lambda i: (i, 0))
    return pl.pallas_call(
        _dft_mats_kernel,
        out_shape=(jax.ShapeDtypeStruct((s, s), BF16),) * 2,
        grid=(s // tr,),
        in_specs=[coarse, coarse, fine, fine],
        out_specs=(out, out),
        compiler_params=_cparams(("arbitrary",)),
        name="dft_mats",
        interpret=interpret,
    )(jnp.cos(ang_hi), jnp.sin(ang_hi), jnp.cos(ang_lo) * norm, jnp.sin(ang_lo) * norm)


def _seq_dft(fa, fb, b, s, interpret):
    ts = min(DFT_TS, s)
    tk = min(DFT_TK, s)
    cs, ss = _seq_dft_mats(s, interpret)
    mat = pl.BlockSpec((ts, tk), lambda bi, i, kk: (i, kk))
    vec = pl.BlockSpec((None, tk, FOURIER_WIDTH), lambda bi, i, kk: (bi, kk, 0))
    y = pl.pallas_call(
        _dft_kernel,
        out_shape=jax.ShapeDtypeStruct((b, s, FOURIER_WIDTH), BF16),
        grid=(b, s // ts, s // tk),
        in_specs=[mat, mat, vec, vec],
        out_specs=pl.BlockSpec((None, ts, FOURIER_WIDTH), lambda bi, i, kk: (bi, i, 0)),
        scratch_shapes=[pltpu.VMEM((ts, FOURIER_WIDTH), F32)],
        compiler_params=_cparams(("arbitrary", "arbitrary", "arbitrary")),
        name="seq_dft",
        interpret=interpret,
    )(cs, ss, fa.reshape(b, s, FOURIER_WIDTH), fb.reshape(b, s, FOURIER_WIDTH))
    return y.reshape(b * s, FOURIER_WIDTH)


def _mix_kernel(x_ref, o0_ref, o1_ref, o2_ref, l0_ref, l1_ref, l2_ref, y_ref, sga_ref, sgb_ref,
                wat_ref, wfo_ref, wout_ref, gf_ref, wqy_ref, sk_ref, hx_ref,
                h_ref, hnt_ref, st_ref, ostage_ref, lstage_ref):
    tm = x_ref.shape[0]

    def token_order(ref, stage, gi):
        dilation = ATTN_GROUPS[gi][1]
        if dilation == 1:
            return ref[0].astype(F32)
        cols = []
        for c in range(ref.shape[-1] // LANES):
            for r in range(dilation):
                stage[gi - 1, c, pl.ds(r, tm // dilation, stride=dilation), :] = (
                    ref[r, :, c * LANES:(c + 1) * LANES].astype(F32))
            cols.append(stage[gi - 1, c])
        return cols[0] if len(cols) == 1 else jnp.concatenate(cols, axis=1)

    lds = [token_order(ref, lstage_ref, gi) for gi, ref in enumerate((l0_ref, l1_ref, l2_ref))]
    mx = jnp.maximum(jnp.maximum(lds[0], lds[1]), lds[2])
    ws = [jnp.exp(ld - mx) for ld in lds]
    inv = 1.0 / (ws[0] + ws[1] + ws[2])
    attn = jnp.zeros((tm, ATTN_GROUP_WIDTH), F32)
    for gi, ref in enumerate((o0_ref, o1_ref, o2_ref)):
        wexp = jnp.dot((ws[gi] * inv).astype(BF16), hx_ref[...], preferred_element_type=F32)
        attn = attn + wexp * token_order(ref, ostage_ref, gi)
    attn = jnp.dot(attn.astype(BF16), wat_ref[...], preferred_element_type=F32)
    four = jnp.dot(y_ref[...], wfo_ref[...], preferred_element_type=F32)
    mix = sga_ref[...].astype(F32) * four + sgb_ref[...].astype(F32) * attn
    h = x_ref[...] + jnp.dot(mix.astype(BF16), wout_ref[...], preferred_element_type=F32)
    h_ref[...] = h
    ms = jnp.mean(h * h, axis=-1, keepdims=True)
    hn32 = h * lax.rsqrt(ms + NORM_EPS) * gf_ref[...]
    hn = hn32.astype(BF16)
    hnt_ref[...] = hn32.T.astype(BF16)
    qy = jnp.dot(hn, wqy_ref[...], preferred_element_type=F32).astype(BF16)
    half = PEER_KEY_DIM // 2
    for hc in range(2 * PEER_HEADS):
        st_ref[hc] = lax.dot_general(sk_ref[hc], qy[:, hc * half:(hc + 1) * half],
                                     (((1,), (1,)), ((), ())), preferred_element_type=F32)


def _mix(x2, o_list, ld_list, yf, sga, sgb, w_attn, w_fourier, w_out, g_ffn, w_query, sub_keys,
         interpret):
    t = x2.shape[0]
    tm = TM_MIX
    spt = o_list[0].shape[2] // tm
    tok = lambda w: pl.BlockSpec((tm, w), lambda i: (i, 0))
    grp = lambda w: [pl.BlockSpec((None, d, tm // d, w), lambda i: (i // spt, 0, i % spt, 0))
                     for _, d in ATTN_GROUPS]
    head_expand = (jnp.arange(LANES)[:, None]
                   == LD_LANES * (jnp.arange(ATTN_GROUP_WIDTH)[None, :] // HEAD_DIM)).astype(BF16)
    n_staged = sum(1 for _, d in ATTN_GROUPS if d > 1)
    sk = jnp.stack([sub_keys[:, 0][:, jnp.array(_PAIR_ORDER)], sub_keys[:, 1]], axis=1)
    sk = sk.reshape(2 * PEER_HEADS, N_KEYS, PEER_KEY_DIM // 2).astype(BF16)
    wat, wfo, wout, wqy = (w_attn.astype(BF16), w_fourier.astype(BF16), w_out.astype(BF16),
                           w_query.astype(BF16))
    return pl.pallas_call(
        _mix_kernel,
        out_shape=(jax.ShapeDtypeStruct((t, D_MODEL), F32),
                   jax.ShapeDtypeStruct((D_MODEL, t), BF16),
                   jax.ShapeDtypeStruct((2 * PEER_HEADS, N_KEYS, t), F32)),
        grid=(t // tm,),
        in_specs=[tok(D_MODEL)] + grp(ATTN_GROUP_WIDTH) + grp(LANES) + [tok(FOURIER_WIDTH),
                  tok(D_MODEL), tok(D_MODEL),
                  _const_spec(wat.shape), _const_spec(wfo.shape), _const_spec(wout.shape),
                  _const_spec((1, D_MODEL)), _const_spec(wqy.shape), _const_spec(sk.shape),
                  _const_spec(head_expand.shape)],
        out_specs=(tok(D_MODEL), pl.BlockSpec((D_MODEL, tm), lambda i: (0, i)),
                   pl.BlockSpec((2 * PEER_HEADS, N_KEYS, tm), lambda i: (0, 0, i))),
        scratch_shapes=[pltpu.VMEM((n_staged, HEADS_PER_GROUP, tm, HEAD_DIM), F32),
                        pltpu.VMEM((n_staged, 1, tm, LANES), F32)],
        compiler_params=_cparams(("arbitrary",)),
        name="mix",
        interpret=interpret,
    )(x2, *o_list, *ld_list, yf, sga, sgb, wat, wfo, wout, g_ffn.reshape(1, D_MODEL), wqy, sk,
      head_expand)


N_PAIRS = N_KEYS // 2
_PAIR_ORDER = tuple(range(0, N_KEYS, 2)) + tuple(range(1, N_KEYS, 2))
SUBLANES = 8


def _batcher_network(n):
    def merge(lo, hi, r):
        step = r * 2
        if step < hi - lo:
            yield from merge(lo, hi, step)
            yield from merge(lo + r, hi, step)
            for i in range(lo + r, hi - r, step):
                yield (i, i + r)
        else:
            yield (lo, lo + r)

    def sort(lo, hi):
        if hi - lo >= 1:
            mid = lo + (hi - lo) // 2
            yield from sort(lo, mid)
            yield from sort(mid + 1, hi)
            yield from merge(lo, hi, 1)

    return tuple(sort(0, n - 1))


_SORT16 = _batcher_network(PEER_TOPK)


def _compare_exchange(v, i, j):
    v[i], v[j] = jnp.maximum(v[i], v[j]), jnp.minimum(v[i], v[j])


def _top16_sorted(keys):
    n = PEER_TOPK
    v = [keys[SUBLANES * k:SUBLANES * (k + 1)] for k in range(n)]
    for i, j in _SORT16:
        _compare_exchange(v, i, j)
    shift = SUBLANES // 2
    while shift >= 1:
        other = [pltpu.roll(x, shift, 0) for x in v]
        v = [jnp.maximum(v[k], other[n - 1 - k]) for k in range(n)]
        d = n // 2
        while d >= 1:
            for i in range(n):
                if not i & d:
                    _compare_exchange(v, i, i + d)
            d //= 2
        shift //= 2
    return [x[0:1] for x in v]


_CAND_PAIRS = tuple((i, j) for i in range(PEER_TOPK) for j in range(PEER_TOPK)
                    if (i + 1) * (j + 1) <= PEER_TOPK)
_CAND_ROWS = -(-len(_CAND_PAIRS) // SUBLANES) * SUBLANES
TIE_STEP = 2.0 ** -17


def _kth_slot_value(vals, k):
    slot = lax.broadcasted_iota(jnp.int32, vals.shape, 0).astype(F32)
    bumped = vals * (1.0 + slot * TIE_STEP)
    cur = bumped
    m = None
    for _ in range(k):
        m = jnp.max(cur, axis=0, keepdims=True)
        cur = jnp.where(cur == m, -1.0, cur)
    return bumped, m


def _candidates(fa, fb):
    rows = [fa[i] * fb[j] for i, j in _CAND_PAIRS]
    rows.append(jnp.zeros((_CAND_ROWS - len(rows),) + rows[0].shape[1:], rows[0].dtype))
    return jnp.concatenate(rows, axis=0)


def _topk_kernel(st_ref, e1_ref, e2_ref, th_ref):
    tm = st_ref.shape[2]
    for h in range(PEER_HEADS):
        for tc in range(tm // LANES):
            lanes = slice(tc * LANES, (tc + 1) * LANES)
            s1 = st_ref[2 * h, :, lanes]
            s2 = st_ref[2 * h + 1, :, lanes]
            a = _top16_sorted(s1)
            b = _top16_sorted(s2)
            ea = [jnp.exp(ai - a[0]) for ai in a]
            eb = [jnp.exp(bi - b[0]) for bi in b]
            cand = _candidates(ea, eb)
            bumped, kth = _kth_slot_value(cand, PEER_TOPK)
            z = jnp.sum(jnp.where(bumped >= kth, cand, 0.0), axis=0, keepdims=True)
            inv_z = 1.0 / z
            cand_b = _candidates([(x * inv_z).astype(BF16) for x in ea],
                                 [x.astype(BF16) for x in eb]).astype(F32)
            _, kth_b = _kth_slot_value(cand_b, PEER_TOPK)
            th_ref[h:h + 1, lanes] = kth_b.astype(BF16).astype(F32)
            e1 = jnp.exp(s1 - a[0]) * inv_z
            e1_ref[h, 0, :, lanes] = e1[:N_PAIRS]
            e1_ref[h, 1, :, lanes] = e1[N_PAIRS:]
            e2_ref[h, :, lanes] = jnp.exp(s2 - b[0]).astype(BF16)


def _topk(st, interpret):
    t = st.shape[2]
    tm = TM_TOPK
    blk = lambda n: pl.BlockSpec((n, N_KEYS, tm), lambda i: (0, 0, i))
    return pl.pallas_call(
        _topk_kernel,
        out_shape=(jax.ShapeDtypeStruct((PEER_HEADS, 2, N_PAIRS, t), F32),
                   jax.ShapeDtypeStruct((PEER_HEADS, N_KEYS, t), BF16),
                   jax.ShapeDtypeStruct((PEER_HEADS, t), F32)),
        grid=(t // tm,),
        in_specs=[blk(2 * PEER_HEADS)],
        out_specs=(pl.BlockSpec((PEER_HEADS, 2, N_PAIRS, tm), lambda i: (0, 0, 0, i)),
                   blk(PEER_HEADS),
                   pl.BlockSpec((PEER_HEADS, tm), lambda i: (0, i))),
        compiler_params=_cparams(("arbitrary",)),
        name="peer_topk",
        interpret=interpret,
    )(st)


SB_PEER = 2 * N_KEYS
KB_PEER = 32


def _peer_kernel(hnt_ref, u_ref, vt_ref, e1_ref, e2_ref, th_ref, h_ref, y_ref,
                 at0_ref, at1_ref, ht0_ref, ht1_ref, acc_ref, join_ref, *, nj):
    s = pl.program_id(0)
    tm = hnt_ref.shape[1]
    j_out = lax.rem(jnp.maximum(s - 2, 0), nj)

    @pl.when(s == 0)
    def _():
        for ref in (at0_ref, at1_ref, ht0_ref, ht1_ref):
            ref[...] = jnp.zeros_like(ref)

    @pl.when(j_out == 0)
    def _():
        acc_ref[...] = jnp.zeros_like(acc_ref)

    def stages(at_w, at_r, ht_w, ht_r):
        at_new = jnp.dot(u_ref[...], hnt_ref[...], preferred_element_type=F32)
        at_w[...] = at_new
        join = at_new[at_new.shape[0] - 8:, tm - LANES:]
        prev_zero = jnp.zeros((KB_PEER, LANES), BF16)
        for key_pair in range(u_ref.shape[0] // SB_PEER):
            for par in range(2):
                for tc in range(tm // LANES):
                    lanes = slice(tc * LANES, (tc + 1) * LANES)
                    e1b = [jnp.broadcast_to(
                        e1_ref[h, par, key_pair:key_pair + 1, lanes].astype(BF16),
                        (KB_PEER, LANES)) for h in range(PEER_HEADS)]
                    thb = [jnp.broadcast_to(th_ref[h:h + 1, lanes].astype(BF16),
                                            (KB_PEER, LANES)) for h in range(PEER_HEADS)]
                    for kb in range(N_KEYS // KB_PEER):
                        keys = slice(kb * KB_PEER, (kb + 1) * KB_PEER)
                        ra = key_pair * SB_PEER + par * N_KEYS + kb * KB_PEER
                        rows = slice(ra, ra + KB_PEER)
                        gate = prev_zero
                        for h in range(PEER_HEADS):
                            p = e1b[h] * e2_ref[h, keys, lanes]
                            gate = gate + jnp.where(p >= thb[h], p, jnp.zeros_like(p))
                        act = jax.nn.gelu(at_r[rows, lanes].astype(BF16)) * gate
                        ht_w[rows, lanes] = act
                        join = join + act[:8].astype(F32)
                        bits = pltpu.bitcast(act, jnp.uint32)
                        prev_zero = pltpu.bitcast((bits >> 16) >> 16, BF16)
        out_new = jnp.dot(vt_ref[...], ht_r[...], preferred_element_type=F32)
        acc_ref[...] += out_new
        join_ref[...] = join + out_new[D_MODEL - 8:, tm - LANES:]

    parity = lax.rem(s, 2)

    @pl.when(parity == 0)
    def _():
        stages(at0_ref, at1_ref, ht1_ref, ht0_ref)

    @pl.when(parity == 1)
    def _():
        stages(at1_ref, at0_ref, ht0_ref, ht1_ref)

    @pl.when((j_out == nj - 1) & (s >= 2))
    def _():
        y_ref[...] = h_ref[...] + acc_ref[...].T


def _expert_table_kernel(u_ref, v_ref, ub_ref, vt_ref):
    ub_ref[...] = u_ref[...].astype(BF16)
    vt_ref[...] = v_ref[...].T.astype(BF16)


def _expert_tables(expert_u, expert_v, layer, eb, interpret):
    nj = N_EXPERTS // eb
    blk = pl.BlockSpec((eb, D_MODEL), lambda j: (j, 0))
    layer_blk = pl.BlockSpec((None, eb, D_MODEL), lambda j: (layer, j, 0))
    return pl.pallas_call(
        _expert_table_kernel,
        out_shape=(jax.ShapeDtypeStruct((N_EXPERTS, D_MODEL), BF16),
                   jax.ShapeDtypeStruct((nj, D_MODEL, eb), BF16)),
        grid=(nj,),
        in_specs=[layer_blk, layer_blk],
        out_specs=(blk, pl.BlockSpec((None, D_MODEL, eb), lambda j: (j, 0, 0))),
        compiler_params=_cparams(("arbitrary",)),
        name="expert_tables",
        interpret=interpret,
    )(expert_u, expert_v)


def _peer(hnt, h, e1, e2, th, u, vt, interpret):
    t = hnt.shape[1]
    tm, eb = TM_PEER, EB_PEER
    nj = N_EXPERTS // eb
    n_pairs = (t // tm) * nj
    tile = lambda s: s // nj
    blk = lambda s: lax.rem(s, nj)
    s1 = lambda s: jnp.minimum(s, n_pairs - 1)
    s2 = lambda s: jnp.clip(s - 1, 0, n_pairs - 1)
    s3 = lambda s: jnp.maximum(s - 2, 0)
    return pl.pallas_call(
        functools.partial(_peer_kernel, nj=nj),
        out_shape=jax.ShapeDtypeStruct((t, D_MODEL), F32),
        grid=(n_pairs + 2,),
        in_specs=[pl.BlockSpec((D_MODEL, tm), lambda s: (0, tile(s1(s)))),
                  pl.BlockSpec((eb, D_MODEL), lambda s: (blk(s1(s)), 0)),
                  pl.BlockSpec((None, D_MODEL, eb), lambda s: (blk(s3(s)), 0, 0)),
                  pl.BlockSpec((PEER_HEADS, 2, eb // SB_PEER, tm),
                               lambda s: (0, 0, blk(s2(s)), tile(s2(s)))),
                  pl.BlockSpec((PEER_HEADS, N_KEYS, tm), lambda s: (0, 0, tile(s2(s)))),
                  pl.BlockSpec((PEER_HEADS, tm), lambda s: (0, tile(s2(s)))),
                  pl.BlockSpec((tm, D_MODEL), lambda s: (tile(s3(s)), 0))],
        out_specs=pl.BlockSpec((tm, D_MODEL), lambda s: (tile(s3(s)), 0)),
        scratch_shapes=[pltpu.VMEM((eb, tm), F32), pltpu.VMEM((eb, tm), F32),
                        pltpu.VMEM((eb, tm), BF16), pltpu.VMEM((eb, tm), BF16),
                        pltpu.VMEM((D_MODEL, tm), F32), pltpu.VMEM((8, LANES), F32)],
        compiler_params=_cparams(("arbitrary",)),
        name="peer_experts",
        interpret=interpret,
    )(hnt, u, vt, e1, e2, th, h)


def _hybrid_block(x, g_mix, w_in, q_gain, k_gain, w_fourier, w_attn, w_out, g_ffn, w_query,
                  sub_keys, expert_u_b, expert_vt_b, interpret=False):
    b, s, _ = x.shape
    x2 = x.reshape(b * s, D_MODEL)
    fa, fb, q, k, v, sga, sgb = _proj(x2, s, g_mix, w_in, q_gain, k_gain, interpret)
    o_list, ld_list = [], []
    for gi, (_, dilation) in enumerate(ATTN_GROUPS):
        o, ld = _attn_group(q[gi], k[gi], v[gi], dilation, interpret)
        o_list.append(o)
        ld_list.append(ld)
    yf = _seq_dft(fa, fb, b, s, interpret)
    h, hnt, st = _mix(x2, o_list, ld_list, yf, sga, sgb, w_attn, w_fourier, w_out, g_ffn,
                      w_query, sub_keys, interpret)
    e1, e2, th = _topk(st, interpret)
    y = _peer(hnt, h, e1, e2, th, expert_u_b, expert_vt_b, interpret)
    return y.reshape(b, s, D_MODEL)


def kernel(x_prompt, x_sample, g_mix, w_in, q_gain, k_gain, w_fourier, w_attn, w_out, g_ffn,
           w_query, sub_keys, expert_u, expert_v):
    y_prompt, y_sample = x_prompt, x_sample
    for layer in range(g_mix.shape[0]):
        params = (g_mix[layer], w_in[layer], q_gain[layer], k_gain[layer], w_fourier[layer],
                  w_attn[layer], w_out[layer], g_ffn[layer], w_query[layer], sub_keys[layer],
                  *_expert_tables(expert_u, expert_v, layer, EB_PEER, False))
        y_prompt = _hybrid_block(y_prompt, *params)
        y_sample = _hybrid_block(y_sample, *params)
    return (y_prompt, y_sample)
```
